```python
import math
import jax, jax.numpy as jnp
from jax import lax
import numpy as np

D_MODEL = 2048
BATCH = 4
SEQ = 2048
DEPTH = 1
DEC_BATCH = 128
DEC_SEQ = 4
PAST_LEN = 2048
PAGE_SIZE = 128

ATTN_WIDTH = D_MODEL // 2
CONV_CH = D_MODEL - ATTN_WIDTH
N_HEADS = 8
HEAD_DIM_V = ATTN_WIDTH // N_HEADS
HEAD_DIM_QK = HEAD_DIM_V // 2
ROPE_DIM = HEAD_DIM_QK // 4
ROPE_THETA = 500000.0
CONV_WIDTH = 31
CONV_GROUPS = 8
N_EXPERTS = 32
TOP_K = 4
D_FF = D_MODEL
SWIGLU_LIMIT = 7.0
SWIGLU_ALPHA = 1.702
MOE_BLOCK = 128
Q_BLOCK = 128
EPS = 1e-5
ATTN_SCALE = HEAD_DIM_QK ** -0.5
PROJ_COLS = 3 * ATTN_WIDTH + 2 * CONV_CH

kernel_name = 'hymba_diffattn_conformer_moe_step'


def rmsnorm(x, w):
    xf = x.astype(jnp.float32)
    y = xf * lax.rsqrt(jnp.mean(xf * xf, axis=-1, keepdims=True) + EPS)
    return (y * w.astype(jnp.float32)).astype(x.dtype)


def layernorm(x, w, b):
    xf = x.astype(jnp.float32)
    mu = jnp.mean(xf, axis=-1, keepdims=True)
    var = jnp.mean(jnp.square(xf - mu), axis=-1, keepdims=True)
    y = (xf - mu) * lax.rsqrt(var + EPS)
    return (y * w.astype(jnp.float32) + b.astype(jnp.float32)).astype(x.dtype)


def rope_partial(x, pos):
    half = ROPE_DIM // 2
    inv = ROPE_THETA ** (-jnp.arange(0, ROPE_DIM, 2, dtype=jnp.float32) / ROPE_DIM)
    ang = pos.astype(jnp.float32)[:, None] * inv[None, :]
    cos = jnp.cos(ang)[None, :, None, :]
    sin = jnp.sin(ang)[None, :, None, :]
    xf = x.astype(jnp.float32)
    x1, x2, rest = xf[..., :half], xf[..., half:ROPE_DIM], xf[..., ROPE_DIM:]
    out = jnp.concatenate([x1 * cos - x2 * sin, x2 * cos + x1 * sin, rest], axis=-1)
    return out.astype(x.dtype)


def project(xn, w_in, pos):
    b, s, _ = xn.shape
    proj = xn @ w_in
    q, k, v, a_val, a_gate = jnp.split(
        proj, [ATTN_WIDTH, 2 * ATTN_WIDTH, 3 * ATTN_WIDTH, 3 * ATTN_WIDTH + CONV_CH], axis=-1)
    q = rope_partial(q.reshape(b, s, 2 * N_HEADS, HEAD_DIM_QK), pos)
    k = rope_partial(k.reshape(b, s, 2 * N_HEADS, HEAD_DIM_QK), pos)
    v = v.reshape(b, s, N_HEADS, HEAD_DIM_V)
    u = a_val * jax.nn.sigmoid(a_gate)
    return q, k, v, u


def diff_attn_core(q, k, v, q_pos, k_pos, lam):
    b, nq = q.shape[:2]
    nk = k.shape[1]
    s = jnp.einsum('bqmd,bkmd->bmqk', q, k).astype(jnp.float32) * ATTN_SCALE
    mask = k_pos[None, :] <= q_pos[:, None]
    s = jnp.where(mask[None, None], s, -jnp.inf)
    p = jax.nn.softmax(s, axis=-1).reshape(b, N_HEADS, 2, nq, nk)
    p = p[:, :, 0] - lam * p[:, :, 1]
    return jnp.einsum('bhqk,bkhd->bqhd', p.astype(v.dtype), v)


def attend_prompt(q, k, v, lam):
    b, s = q.shape[:2]
    nb = s // Q_BLOCK
    pos = jnp.arange(s, dtype=jnp.int32)
    qb = q.reshape(b, nb, Q_BLOCK, 2 * N_HEADS, HEAD_DIM_QK).swapaxes(0, 1)
    pb = pos.reshape(nb, Q_BLOCK)
    ob = lax.map(lambda a: diff_attn_core(a[0], k, v, a[1], pos, lam), (qb, pb))
    return ob.swapaxes(0, 1).reshape(b, s, N_HEADS, HEAD_DIM_V)


def attend_sample(q, k, v, k_past, v_past, lam):
    past = k_past.shape[1]
    n_new = q.shape[1]
    k_all = jnp.concatenate([k_past.astype(k.dtype), k], axis=1)
    v_all = jnp.concatenate([v_past.astype(v.dtype), v], axis=1)
    k_pos = jnp.arange(past + n_new, dtype=jnp.int32)
    q_pos = past + jnp.arange(n_new, dtype=jnp.int32)
    return diff_attn_core(q, k_all, v_all, q_pos, k_pos, lam)


def conv_branch(u, conv_prev, dw_w, dw_b, ln_w, ln_b):
    u_pad = jnp.concatenate([conv_prev.astype(u.dtype), u], axis=1)
    y = lax.conv_general_dilated(
        u_pad, dw_w[:, None, :].astype(u.dtype), window_strides=(1,), padding='VALID',
        dimension_numbers=('NWC', 'WIO', 'NWC'), feature_group_count=CONV_CH)
    y = y + dw_b.astype(u.dtype)
    y = jax.nn.silu(layernorm(y, ln_w, ln_b))
    return y, u_pad[:, -(CONV_WIDTH - 1):]


def moe_ffn(x2d, w_router, b_router, w_gate_up, b_gate_up, w_down, b_down):
    n_tok = x2d.shape[0]
    logits = (x2d @ w_router + b_router).astype(jnp.float32)
    top_logit, top_idx = lax.top_k(logits, TOP_K)
    gates = jax.nn.softmax(top_logit, axis=-1)
    n_asg = n_tok * TOP_K
    flat_e = top_idx.reshape(-1)
    flat_tok = jnp.arange(n_asg, dtype=jnp.int32) // TOP_K
    order = jnp.argsort(flat_e)
    e_sorted = flat_e[order]
    tok_sorted = flat_tok[order]
    gate_sorted = gates.reshape(-1)[order]
    counts = jnp.bincount(flat_e, length=N_EXPERTS)
    start = jnp.cumsum(counts) - counts
    padded = (counts + MOE_BLOCK - 1) // MOE_BLOCK * MOE_BLOCK
    padded_end = jnp.cumsum(padded)
    padded_start = padded_end - padded
    dest = padded_start[e_sorted] + jnp.arange(n_asg, dtype=jnp.int32) - start[e_sorted]
    n_blocks = -(-n_asg // MOE_BLOCK) + N_EXPERTS
    buf_tok = jnp.zeros((n_blocks * MOE_BLOCK,), jnp.int32).at[dest].set(tok_sorted)
    block_start = jnp.arange(n_blocks, dtype=jnp.int32) * MOE_BLOCK
    block_exp = jnp.minimum(jnp.searchsorted(padded_end, block_start, side='right'), N_EXPERTS - 1)
    xb = x2d[buf_tok].reshape(n_blocks, MOE_BLOCK, x2d.shape[-1])

    def expert_block(args):
        xblk, e = args
        h = xblk @ w_gate_up[e] + b_gate_up[e]
        g = jnp.minimum(h[:, :D_FF], SWIGLU_LIMIT)
        lin = jnp.clip(h[:, D_FF:], -SWIGLU_LIMIT, SWIGLU_LIMIT)
        a = g * jax.nn.sigmoid(SWIGLU_ALPHA * g) * (lin + 1)
        return a @ w_down[e] + b_down[e]

    yb = lax.map(expert_block, (xb, block_exp)).reshape(n_blocks * MOE_BLOCK, x2d.shape[-1])
    y_asg = yb[dest] * gate_sorted[:, None].astype(x2d.dtype)
    return jax.ops.segment_sum(y_asg, tok_sorted, num_segments=n_tok)


def setup_inputs(seed: int = 0) -> dict:
    key = jax.random.key(seed)
    ks = jax.random.split(key, 32)
    f32 = jnp.float32
    n_pages = PAST_LEN // PAGE_SIZE
    n_used = DEC_BATCH * n_pages
    n_pool = n_used + max(1, n_used // 4)
    nrm = lambda k, shape, scale: jax.random.normal(k, shape, f32) * scale
    page_table = jax.random.permutation(ks[5], n_pool)[:n_used].reshape(DEC_BATCH, n_pages).astype(jnp.int32)
    return {
        'x_prompt': nrm(ks[0], (BATCH, SEQ, D_MODEL), 1.0),
        'x_sample': nrm(ks[1], (DEC_BATCH, DEC_SEQ, D_MODEL), 1.0),
        'cache_k': nrm(ks[2], (DEPTH, n_pool, PAGE_SIZE, 2 * N_HEADS, HEAD_DIM_QK), 1.0),
        'cache_v': nrm(ks[3], (DEPTH, n_pool, PAGE_SIZE, N_HEADS, HEAD_DIM_V), 1.0),
        'state_conv': nrm(ks[4], (DEPTH, DEC_BATCH, CONV_WIDTH - 1, CONV_CH), 0.5),
        'page_table': page_table,
        'norm_mix_w': 1.0 + nrm(ks[6], (DEPTH, D_MODEL), 0.01),
        'w_in': nrm(ks[7], (DEPTH, D_MODEL, PROJ_COLS), D_MODEL ** -0.5),
        'lambda_q1': nrm(ks[8], (DEPTH, HEAD_DIM_QK), 0.1),
        'lambda_k1': nrm(ks[9], (DEPTH, HEAD_DIM_QK), 0.1),
        'lambda_q2': nrm(ks[10], (DEPTH, HEAD_DIM_QK), 0.1),
        'lambda_k2': nrm(ks[11], (DEPTH, HEAD_DIM_QK), 0.1),
        'subln_w': 1.0 + nrm(ks[12], (DEPTH, HEAD_DIM_V), 0.01),
        'conv_dw_w': nrm(ks[13], (DEPTH, CONV_WIDTH, CONV_CH), CONV_WIDTH ** -0.5),
        'conv_dw_b': nrm(ks[14], (DEPTH, CONV_CH), 0.01),
        'conv_norm_w': 1.0 + nrm(ks[15], (DEPTH, CONV_CH), 0.01),
        'conv_norm_b': nrm(ks[16], (DEPTH, CONV_CH), 0.01),
        'w_o': nrm(ks[17], (DEPTH, D_MODEL, D_MODEL), D_MODEL ** -0.5),
        'norm_ffn_w': 1.0 + nrm(ks[18], (DEPTH, D_MODEL), 0.01),
        'w_router': nrm(ks[19], (DEPTH, D_MODEL, N_EXPERTS), D_MODEL ** -0.5),
        'b_router': nrm(ks[20], (DEPTH, N_EXPERTS), 0.01),
        'w_gate_up': nrm(ks[21], (DEPTH, N_EXPERTS, D_MODEL, 2 * D_FF), D_MODEL ** -0.5),
        'b_gate_up': nrm(ks[22], (DEPTH, N_EXPERTS, 2 * D_FF), 0.01),
        'w_down': nrm(ks[23], (DEPTH, N_EXPERTS, D_FF, D_MODEL), D_FF ** -0.5),
        'b_down': nrm(ks[24], (DEPTH, N_EXPERTS, D_MODEL), 0.01),
        'norm_final_w': 1.0 + nrm(ks[25], (D_MODEL,), 0.01),
    }


def reference(x_prompt, x_sample, cache_k, cache_v, state_conv, page_table,
              norm_mix_w, w_in, lambda_q1, lambda_k1, lambda_q2, lambda_k2, subln_w,
              conv_dw_w, conv_dw_b, conv_norm_w, conv_norm_b, w_o,
              norm_ffn_w, w_router, b_router, w_gate_up, b_gate_up, w_down, b_down,
              norm_final_w):
    f32 = jnp.float32
    n_dec = x_sample.shape[0]
    past_len = page_table.shape[1] * PAGE_SIZE
    pos_p = jnp.arange(x_prompt.shape[1], dtype=jnp.int32)
    pos_s = past_len + jnp.arange(x_sample.shape[1], dtype=jnp.int32)
    h_p, h_s = x_prompt, x_sample
    k_p_l, v_p_l, c_p_l, k_s_l, v_s_l, c_s_l = [], [], [], [], [], []
    for l in range(DEPTH):
        lam_init = 0.8 - 0.6 * math.exp(-0.3 * l)
        lam = (jnp.exp(jnp.sum(lambda_q1[l].astype(f32) * lambda_k1[l].astype(f32)))
               - jnp.exp(jnp.sum(lambda_q2[l].astype(f32) * lambda_k2[l].astype(f32)))
               + lam_init)

        def layer(x, pos, conv_prev, attend):
            b, s, _ = x.shape
            xn = rmsnorm(x, norm_mix_w[l])
            q, k, v, u = project(xn, w_in[l], pos)
            o = rmsnorm(attend(q, k, v), subln_w[l]) * (1.0 - lam_init)
            o = o.reshape(b, s, ATTN_WIDTH)
            c, conv_new = conv_branch(u, conv_prev, conv_dw_w[l], conv_dw_b[l],
                                      conv_norm_w[l], conv_norm_b[l])
            x = x + jnp.concatenate([o, c], axis=-1) @ w_o[l]
            xf = rmsnorm(x, norm_ffn_w[l]).reshape(b * s, D_MODEL)
            x = x + moe_ffn(xf, w_router[l], b_router[l], w_gate_up[l], b_gate_up[l],
                            w_down[l], b_down[l]).reshape(b, s, D_MODEL)
            return x, k, v, conv_new

        conv0 = jnp.zeros((h_p.shape[0], CONV_WIDTH - 1, CONV_CH), h_p.dtype)
        h_p, k_p, v_p, c_p = layer(h_p, pos_p, conv0,
                                   lambda q, k, v: attend_prompt(q, k, v, lam))
        k_past = cache_k[l][page_table].reshape(n_dec, past_len, 2 * N_HEADS, HEAD_DIM_QK)
        v_past = cache_v[l][page_table].reshape(n_dec, past_len, N_HEADS, HEAD_DIM_V)
        h_s, k_s, v_s, c_s = layer(h_s, pos_s, state_conv[l],
                                   lambda q, k, v: attend_sample(q, k, v, k_past, v_past, lam))
        k_p_l.append(k_p); v_p_l.append(v_p); c_p_l.append(c_p)
        k_s_l.append(k_s); v_s_l.append(v_s); c_s_l.append(c_s)
    y_prompt = rmsnorm(h_p, norm_final_w)
    y_sample = rmsnorm(h_s, norm_final_w)
    return (y_prompt, y_sample, jnp.stack(k_p_l), jnp.stack(v_p_l), jnp.stack(c_p_l),
            jnp.stack(k_s_l), jnp.stack(v_s_l), jnp.stack(c_s_l))
```

```python
import functools
import math

import jax
import jax.numpy as jnp
from jax import lax
from jax.experimental import pallas as pl
from jax.experimental.pallas import tpu as pltpu

F32 = jnp.float32
BF16 = jnp.bfloat16
I32 = jnp.int32

EPS = 1e-5
ROPE_THETA = 500000.0
TOP_K = 4
SWIGLU_LIMIT = 7.0
SWIGLU_ALPHA = 1.702
LANES = 128
VMEM_LIMIT = 56 * 1024 * 1024


def _params(sem, vmem=VMEM_LIMIT):
    return pltpu.CompilerParams(dimension_semantics=sem, vmem_limit_bytes=vmem)


def _nt_dot(a, b):
    return lax.dot_general(a, b, (((1,), (1,)), ((), ())), preferred_element_type=F32)


def _rms(x, w):
    ms = jnp.mean(x * x, axis=-1, keepdims=True)
    return x * lax.rsqrt(ms + EPS) * w


def _lambda(lq1, lk1, lq2, lk2, lam_init):
    a = jnp.sum(lq1 * lk1, axis=-1, keepdims=True)
    b = jnp.sum(lq2 * lk2, axis=-1, keepdims=True)
    return jnp.exp(a) - jnp.exp(b) + lam_init


def _proj_body(x_ref, nw_ref, wa_ref, wb_ref, cs_ref, s1_ref, s2_ref,
               q_ref, k_ref, v_ref, u_ref, xn_ref, *, scale, half):
    j = pl.program_id(1)

    @pl.when(j == 0)
    def _():
        xn_ref[...] = _rms(x_ref[...], nw_ref[...]).astype(BF16)

    def rope_store(dst_ref, p, mul):
        cs, s1, s2 = cs_ref[...], s1_ref[...], s2_ref[...]
        for c in range(p.shape[1] // LANES):
            seg = p[:, c * LANES:(c + 1) * LANES]
            r = (seg * cs + pltpu.roll(seg, LANES - half, 1) * s1
                 + pltpu.roll(seg, half, 1) * s2)
            if mul is not None:
                r = r * mul
            dst_ref[:, c * LANES:(c + 1) * LANES] = r.astype(dst_ref.dtype)

    @pl.when(j == 0)
    def _():
        p = jnp.dot(xn_ref[...], wa_ref[...], preferred_element_type=F32)
        rope_store(q_ref, p, scale)

    @pl.when(j == 1)
    def _():
        p = jnp.dot(xn_ref[...], wa_ref[...], preferred_element_type=F32)
        rope_store(k_ref, p, None)

    @pl.when(j == 2)
    def _():
        v_ref[...] = jnp.dot(xn_ref[...], wa_ref[...], preferred_element_type=F32)

    @pl.when(j == 3)
    def _():
        xn = xn_ref[...]
        val = jnp.dot(xn, wa_ref[...], preferred_element_type=F32)
        gate = jnp.dot(xn, wb_ref[...], preferred_element_type=F32)
        u_ref[...] = val * jax.nn.sigmoid(gate)


def _proj(x2d, norm_w, w_in_bf, tables, *, scale, half, tt):
    n, d = x2d.shape
    w = w_in_bf.shape[1] // 5
    cs, s1, s2 = tables
    n_pos_blocks = cs.shape[0] // tt
    tab_spec = pl.BlockSpec((tt, LANES), lambda i, j: (i % n_pos_blocks, 0))
    out_spec = pl.BlockSpec((tt, w), lambda i, j: (i, 0))
    return pl.pallas_call(
        functools.partial(_proj_body, scale=scale, half=half),
        grid=(n // tt, 4),
        in_specs=[
            pl.BlockSpec((tt, d), lambda i, j: (i, 0)),
            pl.BlockSpec((1, d), lambda i, j: (0, 0)),
            pl.BlockSpec((d, w), lambda i, j: (0, j)),
            pl.BlockSpec((d, w), lambda i, j: (0, 4)),
            tab_spec, tab_spec, tab_spec,
        ],
        out_specs=[out_spec, out_spec, out_spec, out_spec],
        out_shape=[
            jax.ShapeDtypeStruct((n, w), BF16),
            jax.ShapeDtypeStruct((n, w), F32),
            jax.ShapeDtypeStruct((n, w), F32),
            jax.ShapeDtypeStruct((n, w), F32),
        ],
        scratch_shapes=[pltpu.VMEM((tt, d), BF16)],
        compiler_params=_params(("arbitrary", "arbitrary")),
        name="proj",
    )(x2d, norm_w, w_in_bf, w_in_bf, cs, s1, s2)


def _rope_tables(pos, d_qk, rope_dim):
    half = rope_dim // 2
    inv = ROPE_THETA ** (-jnp.arange(0, rope_dim, 2, dtype=F32) / rope_dim)
    ang = pos.astype(F32)[:, None] * inv[None, :]
    cos, sin = jnp.cos(ang), jnp.sin(ang)
    p = pos.shape[0]
    zh = jnp.zeros((p, half), F32)
    zr = jnp.zeros((p, d_qk - rope_dim), F32)
    cs = jnp.concatenate([cos, cos, jnp.ones((p, d_qk - rope_dim), F32)], axis=-1)
    s1 = jnp.concatenate([-sin, zh, zr], axis=-1)
    s2 = jnp.concatenate([zh, sin, zr], axis=-1)
    rep = LANES // d_qk
    return tuple(jnp.tile(t, (1, rep)) for t in (cs, s1, s2))


def _attn_prompt_body(q_ref, k_ref, v_ref, lq1_ref, lk1_ref, lq2_ref, lk2_ref, sw_ref,
                      o_ref, kb_ref, vb_ref, *, tq, tk, d_qk, lam_init):
    qi = pl.program_id(2)

    @pl.when(qi == 0)
    def _():
        kb_ref[...] = k_ref[...].astype(BF16)
        vb_ref[...] = v_ref[...].astype(BF16)

    q = q_ref[...]
    lane = lax.broadcasted_iota(I32, q.shape, 1)
    zero = jnp.zeros_like(q)
    qm = (jnp.where(lane < d_qk, q, zero), jnp.where(lane >= d_qk, q, zero))
    row = lax.broadcasted_iota(I32, (tq, tk), 0)
    col = lax.broadcasted_iota(I32, (tq, tk), 1)
    d_v = v_ref.shape[1]

    def step(c, carry, masked):
        k0 = pl.multiple_of(c * tk, tk)
        kc = kb_ref[pl.ds(k0, tk), :]
        vc = vb_ref[pl.ds(k0, tk), :]
        new = []
        for mi in range(2):
            m_p, l_p, a_p = carry[mi]
            s = _nt_dot(qm[mi], kc)
            if masked:
                s = jnp.where(col + c * tk <= row + qi * tq, s, -jnp.inf)
            m_n = jnp.maximum(m_p, jnp.max(s, axis=1, keepdims=True))
            p = jnp.exp(s - m_n)
            al = jnp.exp(m_p - m_n)
            l_n = al * l_p + jnp.sum(p, axis=1, keepdims=True)
            a_n = al * a_p + jnp.dot(p.astype(BF16), vc, preferred_element_type=F32)
            new.append((m_n, l_n, a_n))
        return tuple(new)

    init = tuple((jnp.full((tq, 1), -jnp.inf, F32), jnp.zeros((tq, 1), F32),
                  jnp.zeros((tq, d_v), F32)) for _ in range(2))
    n_full = qi * (tq // tk)
    carry = lax.fori_loop(0, n_full, lambda c, cr: step(c, cr, False), init)
    for dgl in range(tq // tk):
        carry = step(n_full + dgl, carry, True)

    (_, l0, a0), (_, l1, a1) = carry
    lam = _lambda(lq1_ref[...], lk1_ref[...], lq2_ref[...], lk2_ref[...], lam_init)
    o = a0 / l0 - lam * (a1 / l1)
    o_ref[...] = (_rms(o, sw_ref[...]) * (1.0 - lam_init)).astype(o_ref.dtype)


def _attn_prompt(q, k, v, lams, subln_w, *, batch, seq, n_heads, d_qk, lam_init, tq):
    n, w = q.shape
    d_v = w // n_heads
    nq = seq // tq
    vec = pl.BlockSpec((1, d_qk), lambda b, h, i: (0, 0))
    return pl.pallas_call(
        functools.partial(_attn_prompt_body, tq=tq, tk=tq, d_qk=d_qk, lam_init=lam_init),
        grid=(batch, n_heads, nq),
        in_specs=[
            pl.BlockSpec((tq, d_v), lambda b, h, i: (b * nq + i, h)),
            pl.BlockSpec((seq, d_v), lambda b, h, i: (b, h)),
            pl.BlockSpec((seq, d_v), lambda b, h, i: (b, h)),
            vec, vec, vec, vec,
            pl.BlockSpec((1, d_v), lambda b, h, i: (0, 0)),
        ],
        out_specs=pl.BlockSpec((tq, d_v), lambda b, h, i: (b * nq + i, h)),
        out_shape=jax.ShapeDtypeStruct((n, w), BF16),
        scratch_shapes=[pltpu.VMEM((seq, d_v), BF16), pltpu.VMEM((seq, d_v), BF16)],
        compiler_params=_params(("arbitrary", "arbitrary", "arbitrary")),
        name="attn_prompt",
    )(q, k, v, *lams, subln_w)


def _attn_sample_body(pt_ref, qbd_ref, kn_ref, vn_ref, lq1_ref, lk1_ref, lq2_ref, lk2_ref, sw_ref,
                      *rest, n_chunk, n_new, n_heads, lam_init):
    del pt_ref
    k_refs = rest[:n_chunk]
    v_refs = rest[n_chunk:2 * n_chunk]
    o_ref = rest[2 * n_chunk]
    m_ref, l_ref, acc_ref = rest[2 * n_chunk + 1:]
    c = pl.program_id(1)

    @pl.when(c == 0)
    def _():
        m_ref[...] = jnp.full(m_ref.shape, -jnp.inf, F32)
        l_ref[...] = jnp.zeros(l_ref.shape, F32)
        acc_ref[...] = jnp.zeros(acc_ref.shape, F32)

    qbd = qbd_ref[0]
    m_p, l_p, acc = m_ref[...], l_ref[...], acc_ref[...]
    for i in range(n_chunk):
        kp = k_refs[i][0].astype(BF16)
        vp = v_refs[i][0].astype(BF16)
        s = _nt_dot(qbd, kp)
        m_n = jnp.maximum(m_p, jnp.max(s, axis=1, keepdims=True))
        p = jnp.exp(s - m_n)
        al = jnp.exp(m_p - m_n)
        l_p = al * l_p + jnp.sum(p, axis=1, keepdims=True)
        acc = al * acc + jnp.dot(p.astype(BF16), vp, preferred_element_type=F32)
        m_p = m_n
    m_ref[...] = m_p
    l_ref[...] = l_p
    acc_ref[...] = acc

    @pl.when(c == pl.num_programs(1) - 1)
    def _():
        qf = qbd.astype(F32)
        kn, vn = kn_ref[0], vn_ref[0]
        q_tok = lax.broadcasted_iota(I32, (qf.shape[0], 1), 0) % n_new
        s_new = []
        for j in range(n_new):
            sj = jnp.sum(qf * kn[j:j + 1, :], axis=1, keepdims=True)
            s_new.append(jnp.where(q_tok >= j, sj, -jnp.inf))
        m_n = m_p
        for sj in s_new:
            m_n = jnp.maximum(m_n, sj)
        al = jnp.exp(m_p - m_n)
        l_n = al * l_p
        a_n = al * acc
        for j in range(n_new):
            pj = jnp.exp(s_new[j] - m_n)
            l_n = l_n + pj
            a_n = a_n + pj * vn[j:j + 1, :]
        a_n = a_n / l_n
        lam = _lambda(lq1_ref[...], lk1_ref[...], lq2_ref[...], lk2_ref[...], lam_init)
        d_v = a_n.shape[1] // n_heads
        for h in range(n_heads):
            blk = a_n[h * 2 * n_new:(h + 1) * 2 * n_new, h * d_v:(h + 1) * d_v]
            o = blk[:n_new] - lam * blk[n_new:]
            o_ref[0, :, h * d_v:(h + 1) * d_v] = (
                _rms(o, sw_ref[...]) * (1.0 - lam_init)).astype(o_ref.dtype)


def _attn_sample(page_table, qbd, k_new, v_new, ck, cv, lams, subln_w, *, n_heads, lam_init, n_chunk):
    dec, n_pages = page_table.shape
    n_new, w = k_new.shape[1:]
    page = ck.shape[1]
    d_qk = lams[0].shape[1]
    rows = qbd.shape[1]
    pt_flat = page_table.reshape(-1)

    def page_map(i, b, c, pt):
        return (pt[b * n_pages + c * n_chunk + i], 0, 0)

    vec = pl.BlockSpec((1, d_qk), lambda b, c, pt: (0, 0))
    kv_specs = [pl.BlockSpec((1, page, w), functools.partial(page_map, i)) for i in range(n_chunk)]
    grid_spec = pltpu.PrefetchScalarGridSpec(
        num_scalar_prefetch=1,
        grid=(dec, n_pages // n_chunk),
        in_specs=[
            pl.BlockSpec((1, rows, w), lambda b, c, pt: (b, 0, 0)),
            pl.BlockSpec((1, n_new, w), lambda b, c, pt: (b, 0, 0)),
            pl.BlockSpec((1, n_new, w), lambda b, c, pt: (b, 0, 0)),
            vec, vec, vec, vec,
            pl.BlockSpec((1, w // n_heads), lambda b, c, pt: (0, 0)),
        ] + kv_specs + kv_specs,
        out_specs=pl.BlockSpec((1, n_new, w), lambda b, c, pt: (b, 0, 0)),
        scratch_shapes=[pltpu.VMEM((rows, 1), F32), pltpu.VMEM((rows, 1), F32),
                        pltpu.VMEM((rows, w), F32)],
    )
    return pl.pallas_call(
        functools.partial(_attn_sample_body, n_chunk=n_chunk, n_new=n_new, n_heads=n_heads,
                          lam_init=lam_init),
        grid_spec=grid_spec,
        out_shape=jax.ShapeDtypeStruct((dec, n_new, w), F32),
        compiler_params=_params(("arbitrary", "arbitrary")),
        name="attn_sample",
    )(pt_flat, qbd, k_new, v_new, *lams, subln_w, *([ck] * n_chunk), *([cv] * n_chunk))


def _ln_swish(y, lw, lb):
    mu = jnp.mean(y, axis=-1, keepdims=True)
    yc = y - mu
    var = jnp.mean(yc * yc, axis=-1, keepdims=True)
    z = yc * lax.rsqrt(var + EPS) * lw + lb
    return z * jax.nn.sigmoid(z)


def _conv_prompt_body(u_ref, up_ref, dw_ref, db_ref, lw_ref, lb_ref, c_ref, buf_ref, y_ref,
                      *, tt, width, halo, rb):
    t = pl.program_id(1)
    buf_ref[0:halo, :] = jnp.where(t == 0, 0.0, up_ref[...])
    buf_ref[halo:halo + tt, :] = u_ref[...]
    off = halo - (width - 1)

    def col_body(cc, carry):
        c0 = pl.multiple_of(cc * LANES, LANES)
        wts = dw_ref[:, pl.ds(c0, LANES)]
        bias = db_ref[:, pl.ds(c0, LANES)]
        for r in range(tt // rb):
            acc = jnp.zeros((rb, LANES), F32)
            for w in range(width):
                lo = r * rb + off + w
                acc = acc + buf_ref[lo:lo + rb, pl.ds(c0, LANES)] * wts[w:w + 1, :]
            y_ref[r * rb:(r + 1) * rb, pl.ds(c0, LANES)] = acc + bias
        return carry

    lax.fori_loop(0, u_ref.shape[1] // LANES, col_body, 0)
    c_ref[...] = _ln_swish(y_ref[...], lw_ref[...], lb_ref[...]).astype(c_ref.dtype)


def _conv_prompt(u, dw_w, dw_b, ln_w, ln_b, *, batch, seq, tt):
    n, ch = u.shape
    width = dw_w.shape[0]
    halo = 32
    nt = seq // tt
    per = tt // halo
    row = pl.BlockSpec((1, ch), lambda b, t: (0, 0))
    return pl.pallas_call(
        functools.partial(_conv_prompt_body, tt=tt, width=width, halo=halo, rb=64),
        grid=(batch, nt),
        in_specs=[
            pl.BlockSpec((tt, ch), lambda b, t: (b * nt + t, 0)),
            pl.BlockSpec((halo, ch), lambda b, t: (jnp.maximum((b * nt + t) * per - 1, 0), 0)),
            pl.BlockSpec((width, ch), lambda b, t: (0, 0)),
            row, row, row,
        ],
        out_specs=pl.BlockSpec((tt, ch), lambda b, t: (b * nt + t, 0)),
        out_shape=jax.ShapeDtypeStruct((n, ch), BF16),
        scratch_shapes=[pltpu.VMEM((halo + tt, ch), F32), pltpu.VMEM((tt, ch), F32)],
        compiler_params=_params(("arbitrary", "arbitrary")),
        name="conv_prompt",
    )(u, u, dw_w, dw_b, ln_w, ln_b)


def _conv_sample_body(up_ref, dw_ref, db_ref, lw_ref, lb_ref, c_ref, y_ref, *, width, n_new):
    length, tb, ch = up_ref.shape

    def col_body(cc, carry):
        c0 = pl.multiple_of(cc * LANES, LANES)
        wts = dw_ref[:, pl.ds(c0, LANES)]
        bias = db_ref[:, pl.ds(c0, LANES)]
        accs = [jnp.zeros((tb, LANES), F32) for _ in range(n_new)]
        for j in range(length):
            x = up_ref[j, :, pl.ds(c0, LANES)]
            for t in range(n_new):
                w = j - t
                if 0 <= w < width:
                    accs[t] = accs[t] + x * wts[w:w + 1, :]
        for t in range(n_new):
            y_ref[t, :, pl.ds(c0, LANES)] = accs[t] + bias
        return carry

    lax.fori_loop(0, ch // LANES, col_body, 0)
    c_ref[...] = _ln_swish(y_ref[...], lw_ref[...], lb_ref[...]).astype(c_ref.dtype)


def _conv_sample(u_pad_t, dw_w, dw_b, ln_w, ln_b, *, n_new, tb):
    length, dec, ch = u_pad_t.shape
    width = dw_w.shape[0]
    row = pl.BlockSpec((1, ch), lambda b: (0, 0))
    return pl.pallas_call(
        functools.partial(_conv_sample_body, width=width, n_new=n_new),
        grid=(dec // tb,),
        in_specs=[
            pl.BlockSpec((length, tb, ch), lambda b: (0, b, 0)),
            pl.BlockSpec((width, ch), lambda b: (0, 0)),
            row, row, row,
        ],
        out_specs=pl.BlockSpec((n_new, tb, ch), lambda b: (0, b, 0)),
        out_shape=jax.ShapeDtypeStruct((n_new, dec, ch), BF16),
        scratch_shapes=[pltpu.VMEM((n_new, tb, ch), F32)],
        compiler_params=_params(("arbitrary",)),
        name="conv_sample",
    )(u_pad_t, dw_w, dw_b, ln_w, ln_b)


def _wo_body(o_ref, c_ref, x_ref, wo_ref, nw_ref, wr_ref, br_ref,
             x1_ref, xf_ref, idx_ref, gate_ref, *, aw, top_k):
    attn = (jnp.dot(o_ref[...], wo_ref[0:aw, :], preferred_element_type=F32)
            + jnp.dot(c_ref[...], wo_ref[aw:, :], preferred_element_type=F32))
    x1 = x_ref[...] + attn
    x1_ref[...] = x1
    xf = _rms(x1, nw_ref[...])
    xf_ref[...] = xf
    logits = jnp.dot(xf, wr_ref[...], preferred_element_type=F32,
                     precision=lax.Precision.HIGHEST) + br_ref[...]
    n_exp = logits.shape[1]
    lane_e = lax.broadcasted_iota(I32, logits.shape, 1)
    lane_o = lax.broadcasted_iota(I32, idx_ref.shape, 1)
    vals = logits
    tops, idxs = [], []
    for _ in range(top_k):
        mx = jnp.max(vals, axis=1, keepdims=True)
        am = jnp.min(jnp.where(vals == mx, lane_e, n_exp), axis=1, keepdims=True)
        tops.append(mx)
        idxs.append(am)
        vals = jnp.where(lane_e == am, -jnp.inf, vals)
    es = [jnp.exp(t - tops[0]) for t in tops]
    den = es[0]
    for e in es[1:]:
        den = den + e
    idx_out = jnp.zeros(idx_ref.shape, I32)
    gate_out = jnp.zeros(gate_ref.shape, F32)
    for k in range(top_k):
        idx_out = jnp.where(lane_o == k, idxs[k], idx_out)
        gate_out = jnp.where(lane_o == k, es[k] / den, gate_out)
    idx_ref[...] = idx_out
    gate_ref[...] = gate_out


def _wo_router(o, c, x2d, w_o_bf, norm_w, w_router, b_router, *, tt):
    n, d = x2d.shape
    aw = o.shape[1]
    n_exp = w_router.shape[1]
    row = pl.BlockSpec((tt, d), lambda i: (i, 0))
    half = pl.BlockSpec((tt, aw), lambda i: (i, 0))
    small = pl.BlockSpec((tt, LANES), lambda i: (i, 0))
    return pl.pallas_call(
        functools.partial(_wo_body, aw=aw, top_k=TOP_K),
        grid=(n // tt,),
        in_specs=[
            half, pl.BlockSpec((tt, d - aw), lambda i: (i, 0)), row,
            pl.BlockSpec((d, d), lambda i: (0, 0)),
            pl.BlockSpec((1, d), lambda i: (0, 0)),
            pl.BlockSpec((d, n_exp), lambda i: (0, 0)),
            pl.BlockSpec((1, n_exp), lambda i: (0, 0)),
        ],
        out_specs=[row, row, small, small],
        out_shape=[
            jax.ShapeDtypeStruct((n, d), F32),
            jax.ShapeDtypeStruct((n, d), F32),
            jax.ShapeDtypeStruct((n, LANES), I32),
            jax.ShapeDtypeStruct((n, LANES), F32),
        ],
        compiler_params=_params(("arbitrary",)),
        name="wo_router",
    )(o, c, x2d, w_o_bf, norm_w, w_router, b_router)


def _route_plan(idx, n_exp, tm):
    n_tok, top_k = idx.shape
    n_asg = n_tok * top_k
    flat_e = idx.reshape(-1)
    onehot = (flat_e[:, None] == jnp.arange(n_exp, dtype=I32)[None, :]).astype(I32)
    csum = jnp.cumsum(onehot, axis=0)
    rank = jnp.sum((csum - onehot) * onehot, axis=1)
    counts = csum[-1]
    padded = (counts + tm - 1) // tm * tm
    pend = jnp.cumsum(padded)
    pstart = pend - padded
    dest = (pstart[flat_e] + rank).astype(I32)
    n_tiles = -(-n_asg // tm) + n_exp
    tile_start = jnp.arange(n_tiles, dtype=I32) * tm
    tile_exp = jnp.minimum(jnp.searchsorted(pend, tile_start, side='right'), n_exp - 1).astype(I32)
    n_used = (pend[-1] // tm).astype(I32).reshape(1)
    flat_tok = jnp.arange(n_asg, dtype=I32) // top_k
    buf_tok = jnp.zeros((n_tiles * tm,), I32).at[dest].set(flat_tok)
    return dest, buf_tok, tile_exp, n_used


def _gather_body(tok_ref, nused_ref, x_hbm, xs_hbm, sem, *, tm):
    t = pl.program_id(0)

    def row_copy(r):
        tok = tok_ref[t * tm + r]
        return pltpu.make_async_copy(x_hbm.at[pl.ds(tok, 1)], xs_hbm.at[pl.ds(t * tm + r, 1)], sem)

    @pl.when(t < nused_ref[0])
    def _():
        def issue(r, carry):
            row_copy(r).start()
            return carry

        def drain(r, carry):
            row_copy(r).wait()
            return carry

        lax.fori_loop(0, tm, issue, 0)
        lax.fori_loop(0, tm, drain, 0)


def _gather_rows(xf, buf_tok, n_used, *, tm):
    n_rows = buf_tok.shape[0]
    d = xf.shape[1]
    grid_spec = pltpu.PrefetchScalarGridSpec(
        num_scalar_prefetch=2,
        grid=(n_rows // tm,),
        in_specs=[pl.BlockSpec(memory_space=pl.ANY)],
        out_specs=pl.BlockSpec(memory_space=pl.ANY),
        scratch_shapes=[pltpu.SemaphoreType.DMA],
    )
    return pl.pallas_call(
        functools.partial(_gather_body, tm=tm),
        grid_spec=grid_spec,
        out_shape=jax.ShapeDtypeStruct((n_rows, d), xf.dtype),
        compiler_params=_params(("arbitrary",)),
        name="moe_gather",
    )(buf_tok, n_used, xf)


def _moe_body(texp_ref, nused_ref, x_ref, wg_ref, wu_ref, wd_ref, bg_ref, bu_ref, bd_ref, y_ref):
    del texp_ref
    t, f = pl.program_id(0), pl.program_id(1)

    @pl.when(t < nused_ref[0])
    def _():
        x = x_ref[...].astype(BF16)
        g = jnp.dot(x, wg_ref[0].astype(BF16), preferred_element_type=F32) + bg_ref[0]
        lin = jnp.dot(x, wu_ref[0].astype(BF16), preferred_element_type=F32) + bu_ref[0]
        g = jnp.minimum(g, SWIGLU_LIMIT)
        lin = jnp.clip(lin, -SWIGLU_LIMIT, SWIGLU_LIMIT)
        a = g * jax.nn.sigmoid(SWIGLU_ALPHA * g) * (lin + 1.0)
        part = jnp.dot(a.astype(BF16), wd_ref[0].astype(BF16), preferred_element_type=F32)

        @pl.when(f == 0)
        def _():
            y_ref[...] = part + bd_ref[0]

        @pl.when(f > 0)
        def _():
            y_ref[...] += part


def _moe_experts(xs, tile_exp, n_used, w_gate_up, b_gate_up, w_down, b_down, *, tm, tf):
    n_rows, d = xs.shape
    n_exp, _, two_ff = w_gate_up.shape
    d_ff = two_ff // 2
    n_f = d_ff // tf
    bgu = b_gate_up.reshape(n_exp, 1, two_ff)
    bdn = b_down.reshape(n_exp, 1, d)

    def tile(t, nu):
        return jnp.minimum(t, nu[0] - 1)

    def fcol(t, f, nu):
        return jnp.where(t < nu[0], f, n_f - 1)

    grid_spec = pltpu.PrefetchScalarGridSpec(
        num_scalar_prefetch=2,
        grid=(n_rows // tm, n_f),
        in_specs=[
            pl.BlockSpec((tm, d), lambda t, f, te, nu: (tile(t, nu), 0)),
            pl.BlockSpec((1, d, tf), lambda t, f, te, nu: (te[tile(t, nu)], 0, fcol(t, f, nu))),
            pl.BlockSpec((1, d, tf), lambda t, f, te, nu: (te[tile(t, nu)], 0, n_f + fcol(t, f, nu))),
            pl.BlockSpec((1, tf, d), lambda t, f, te, nu: (te[tile(t, nu)], fcol(t, f, nu), 0)),
            pl.BlockSpec((1, 1, tf), lambda t, f, te, nu: (te[tile(t, nu)], 0, fcol(t, f, nu))),
            pl.BlockSpec((1, 1, tf), lambda t, f, te, nu: (te[tile(t, nu)], 0, n_f + fcol(t, f, nu))),
            pl.BlockSpec((1, 1, d), lambda t, f, te, nu: (te[tile(t, nu)], 0, 0)),
        ],
        out_specs=pl.BlockSpec((tm, d), lambda t, f, te, nu: (tile(t, nu), 0)),
    )
    return pl.pallas_call(
        _moe_body,
        grid_spec=grid_spec,
        out_shape=jax.ShapeDtypeStruct((n_rows, d), F32),
        compiler_params=_params(("arbitrary", "arbitrary")),
        name="moe_experts",
    )(tile_exp, n_used, xs, w_gate_up, w_gate_up, w_down, bgu, bgu, bdn)


def _combine_body(dest_ref, x1_ref, gate_ref, nw_ref, yb_hbm, out_ref, rows_ref, sem, *, tt, top_k):
    i = pl.program_id(0)

    def row_copy(r, k):
        d = dest_ref[(i * tt + r) * top_k + k]
        return pltpu.make_async_copy(yb_hbm.at[pl.ds(d, 1)], rows_ref.at[k, pl.ds(r, 1)], sem)

    def issue(r, carry):
        for k in range(top_k):
            row_copy(r, k).start()
        return carry

    def drain(r, carry):
        for k in range(top_k):
            row_copy(r, k).wait()
        return carry

    lax.fori_loop(0, tt, issue, 0)
    lax.fori_loop(0, tt, drain, 0)
    gate = gate_ref[...]
    moe = rows_ref[0] * gate[:, 0:1]
    for k in range(1, top_k):
        moe = moe + rows_ref[k] * gate[:, k:k + 1]
    out_ref[...] = _rms(x1_ref[...] + moe, nw_ref[...])


def _combine(dest, x1, gate, norm_w, yb, *, tt):
    n, d = x1.shape
    grid_spec = pltpu.PrefetchScalarGridSpec(
        num_scalar_prefetch=1,
        grid=(n // tt,),
        in_specs=[
            pl.BlockSpec((tt, d), lambda i, ds: (i, 0)),
            pl.BlockSpec((tt, LANES), lambda i, ds: (i, 0)),
            pl.BlockSpec((1, d), lambda i, ds: (0, 0)),
            pl.BlockSpec(memory_space=pl.ANY),
        ],
        out_specs=pl.BlockSpec((tt, d), lambda i, ds: (i, 0)),
        scratch_shapes=[pltpu.VMEM((TOP_K, tt, d), F32), pltpu.SemaphoreType.DMA],
    )
    return pl.pallas_call(
        functools.partial(_combine_body, tt=tt, top_k=TOP_K),
        grid_spec=grid_spec,
        out_shape=jax.ShapeDtypeStruct((n, d), F32),
        compiler_params=_params(("arbitrary",)),
        name="moe_combine",
    )(dest, x1, gate, norm_w, yb)


def _tile(n, pref):
    t = min(n, pref)
    while n % t:
        t //= 2
    return t


def kernel(x_prompt, x_sample, cache_k, cache_v, state_conv, page_table, norm_mix_w, w_in,
           lambda_q1, lambda_k1, lambda_q2, lambda_k2, subln_w, conv_dw_w, conv_dw_b,
           conv_norm_w, conv_norm_b, w_o, norm_ffn_w, w_router, b_router, w_gate_up, b_gate_up,
           w_down, b_down, norm_final_w):
    batch, seq, d = x_prompt.shape
    dec, n_new, _ = x_sample.shape
    depth, n_pool, page, n_maps, d_qk = cache_k.shape
    n_heads, d_v = cache_v.shape[3:]
    aw = n_heads * d_v
    ch = d - aw
    n_pages = page_table.shape[1]
    past = n_pages * page
    rope_dim = d_qk // 4
    half = rope_dim // 2
    scale = d_qk ** -0.5
    n_exp = w_router.shape[2]
    n_p, n_s = batch * seq, dec * n_new

    h_p = x_prompt.reshape(n_p, d)
    h_s = x_sample.reshape(n_s, d)
    tab_p = _rope_tables(jnp.arange(seq, dtype=I32), d_qk, rope_dim)
    tab_s = _rope_tables(past + jnp.arange(n_s, dtype=I32) % n_new, d_qk, rope_dim)
    outs = {k: [] for k in ("kp", "vp", "cp", "ks", "vs", "cs")}

    for l in range(depth):
        lam_init = 0.8 - 0.6 * math.exp(-0.3 * l)
        lams = tuple(v[l].reshape(1, d_qk) for v in (lambda_q1, lambda_k1, lambda_q2, lambda_k2))
        sw = subln_w[l].reshape(1, d_v)
        w_in_bf = w_in[l].astype(BF16)
        w_o_bf = w_o[l].astype(BF16)
        nmw = norm_mix_w[l].reshape(1, d)
        dw_b = conv_dw_b[l].reshape(1, ch)
        ln_w = conv_norm_w[l].reshape(1, ch)
        ln_b = conv_norm_b[l].reshape(1, ch)

        tt_p = _tile(seq, 512)
        q_p, k_p, v_p, u_p = _proj(h_p, nmw, w_in_bf, tab_p, scale=scale, half=half, tt=tt_p)
        tt_s = _tile(n_s, 512)
        q_s, k_s, v_s, u_s = _proj(h_s, nmw, w_in_bf, tab_s, scale=scale, half=half, tt=tt_s)

        o_p = _attn_prompt(q_p, k_p, v_p, lams, sw, batch=batch, seq=seq, n_heads=n_heads,
                           d_qk=d_qk, lam_init=lam_init, tq=_tile(seq, 256))
        eye = jnp.eye(n_maps, dtype=BF16)
        qbd = jnp.einsum('bqmd,mn->bmqnd', q_s.reshape(dec, n_new, n_maps, d_qk), eye)
        qbd = qbd.reshape(dec, n_maps * n_new, aw)
        ck = cache_k[l].reshape(n_pool, page, aw)
        cv = cache_v[l].reshape(n_pool, page, aw)
        o_s = _attn_sample(page_table, qbd, k_s.reshape(dec, n_new, aw), v_s.reshape(dec, n_new, aw),
                           ck, cv, lams, sw, n_heads=n_heads, lam_init=lam_init,
                           n_chunk=_tile(n_pages, 4))
        o_s = o_s.reshape(n_s, aw).astype(BF16)

        c_p = _conv_prompt(u_p, conv_dw_w[l], dw_b, ln_w, ln_b, batch=batch, seq=seq,
                           tt=_tile(seq, 256))
        u_s3 = u_s.reshape(dec, n_new, ch)
        u_pad_s = jnp.concatenate([state_conv[l], u_s3], axis=1)
        c_s = _conv_sample(u_pad_s.transpose(1, 0, 2), conv_dw_w[l], dw_b, ln_w, ln_b,
                           n_new=n_new, tb=_tile(dec, 16))
        c_s = c_s.transpose(1, 0, 2).reshape(n_s, ch)

        nfw = norm_ffn_w[l].reshape(1, d)
        br = b_router[l].reshape(1, n_exp)
        x1_p, xf_p, idx_p, gate_p = _wo_router(o_p, c_p, h_p, w_o_bf, nfw, w_router[l], br, tt=tt_p)
        x1_s, xf_s, idx_s, gate_s = _wo_router(o_s, c_s, h_s, w_o_bf, nfw, w_router[l], br, tt=tt_s)

        tm = 512
        idx_all = jnp.concatenate([idx_p[:, :TOP_K], idx_s[:, :TOP_K]], axis=0)
        dest, buf_tok, tile_exp, n_used = _route_plan(idx_all, n_exp, tm)
        xf_all = jnp.concatenate([xf_p, xf_s], axis=0)
        xs = _gather_rows(xf_all, buf_tok, n_used, tm=tm)
        yb = _moe_experts(xs, tile_exp, n_used, w_gate_up[l], b_gate_up[l], w_down[l], b_down[l],
                          tm=tm, tf=512)

        last = l == depth - 1
        nw_out = norm_final_w.reshape(1, d) if last else jnp.ones((1, d), F32)
        assert last, "the combine kernel fuses the final norm; deeper stacks need an un-normed variant"
        h_p = _combine(dest[:n_p * TOP_K], x1_p, gate_p, nw_out, yb, tt=_tile(n_p, 256))
        h_s = _combine(dest[n_p * TOP_K:], x1_s, gate_s, nw_out, yb, tt=_tile(n_s, 256))

        outs["kp"].append(k_p.reshape(batch, seq, n_maps, d_qk))
        outs["vp"].append(v_p.reshape(batch, seq, n_heads, d_v))
        outs["cp"].append(u_p.reshape(batch, seq, ch)[:, seq - (conv_dw_w.shape[1] - 1):])
        outs["ks"].append(k_s.reshape(dec, n_new, n_maps, d_qk))
        outs["vs"].append(v_s.reshape(dec, n_new, n_heads, d_v))
        outs["cs"].append(u_pad_s[:, n_new:])

    return (h_p.reshape(batch, seq, d), h_s.reshape(dec, n_new, d),
            jnp.stack(outs["kp"]), jnp.stack(outs["vp"]), jnp.stack(outs["cp"]),
            jnp.stack(outs["ks"]), jnp.stack(outs["vs"]), jnp.stack(outs["cs"]))
```

```python
import functools
import math

import jax
import jax.numpy as jnp
from jax import lax
from jax.experimental import pallas as pl
from jax.experimental.pallas import tpu as pltpu

F32 = jnp.float32
BF16 = jnp.bfloat16
I32 = jnp.int32

EPS = 1e-5
ROPE_THETA = 500000.0
TOP_K = 4
SWIGLU_LIMIT = 7.0
SWIGLU_ALPHA = 1.702
LANES = 128
VMEM_LIMIT = 56 * 1024 * 1024
QR = 16
MOE_TILE = 512
MOE_SUBTILE = 256
GATHER_UNROLL = 8


def _params(sem, vmem=VMEM_LIMIT):
    return pltpu.CompilerParams(dimension_semantics=sem, vmem_limit_bytes=vmem)


def _nt_dot(a, b):
    return lax.dot_general(a, b, (((1,), (1,)), ((), ())), preferred_element_type=F32)


def _rms(x, w):
    ms = jnp.mean(x * x, axis=-1, keepdims=True)
    return x * lax.rsqrt(ms + EPS) * w


def _lambda(lq1, lk1, lq2, lk2, lam_init):
    a = jnp.sum(lq1 * lk1, axis=-1, keepdims=True)
    b = jnp.sum(lq2 * lk2, axis=-1, keepdims=True)
    return jnp.exp(a) - jnp.exp(b) + lam_init


def _proj_body(x_ref, nw_ref, wa_ref, wb_ref, cs_ref, s1_ref, s2_ref,
               q_ref, k_ref, v_ref, u_ref, xn_ref, *, scale, half):
    j = pl.program_id(1)

    @pl.when(j == 0)
    def _():
        xn_ref[...] = _rms(x_ref[...], nw_ref[...]).astype(BF16)

    def rope_store(dst_ref, p, mul):
        cs, s1, s2 = cs_ref[...], s1_ref[...], s2_ref[...]
        for c in range(p.shape[1] // LANES):
            seg = p[:, c * LANES:(c + 1) * LANES]
            r = (seg * cs + pltpu.roll(seg, LANES - half, 1) * s1
                 + pltpu.roll(seg, half, 1) * s2)
            if mul is not None:
                r = r * mul
            dst_ref[:, c * LANES:(c + 1) * LANES] = r.astype(dst_ref.dtype)

    @pl.when(j == 0)
    def _():
        p = jnp.dot(xn_ref[...], wa_ref[...], preferred_element_type=F32)
        rope_store(q_ref, p, scale)

    @pl.when(j == 1)
    def _():
        p = jnp.dot(xn_ref[...], wa_ref[...], preferred_element_type=F32)
        rope_store(k_ref, p, None)

    @pl.when(j == 2)
    def _():
        v_ref[...] = jnp.dot(xn_ref[...], wa_ref[...], preferred_element_type=F32)

    @pl.when(j == 3)
    def _():
        xn = xn_ref[...]
        val = jnp.dot(xn, wa_ref[...], preferred_element_type=F32)
        gate = jnp.dot(xn, wb_ref[...], preferred_element_type=F32)
        u_ref[...] = val * jax.nn.sigmoid(gate)


def _proj(x2d, norm_w, w_in_bf, tables, *, scale, half, tt):
    n, d = x2d.shape
    w = w_in_bf.shape[1] // 5
    cs, s1, s2 = tables
    n_pos_blocks = cs.shape[0] // tt
    tab_spec = pl.BlockSpec((tt, LANES), lambda i, j: (i % n_pos_blocks, 0))
    out_spec = pl.BlockSpec((tt, w), lambda i, j: (i, 0))
    return pl.pallas_call(
        functools.partial(_proj_body, scale=scale, half=half),
        grid=(n // tt, 4),
        in_specs=[
            pl.BlockSpec((tt, d), lambda i, j: (i, 0)),
            pl.BlockSpec((1, d), lambda i, j: (0, 0)),
            pl.BlockSpec((d, w), lambda i, j: (0, j)),
            pl.BlockSpec((d, w), lambda i, j: (0, 4)),
            tab_spec, tab_spec, tab_spec,
        ],
        out_specs=[out_spec, out_spec, out_spec, out_spec],
        out_shape=[
            jax.ShapeDtypeStruct((n, w), BF16),
            jax.ShapeDtypeStruct((n, w), F32),
            jax.ShapeDtypeStruct((n, w), F32),
            jax.ShapeDtypeStruct((n, w), F32),
        ],
        scratch_shapes=[pltpu.VMEM((tt, d), BF16)],
        compiler_params=_params(("arbitrary", "arbitrary")),
        name="proj",
    )(x2d, norm_w, w_in_bf, w_in_bf, cs, s1, s2)


def _rope_tables(pos, d_qk, rope_dim):
    half = rope_dim // 2
    inv = ROPE_THETA ** (-jnp.arange(0, rope_dim, 2, dtype=F32) / rope_dim)
    ang = pos.astype(F32)[:, None] * inv[None, :]
    cos, sin = jnp.cos(ang), jnp.sin(ang)
    p = pos.shape[0]
    zh = jnp.zeros((p, half), F32)
    zr = jnp.zeros((p, d_qk - rope_dim), F32)
    cs = jnp.concatenate([cos, cos, jnp.ones((p, d_qk - rope_dim), F32)], axis=-1)
    s1 = jnp.concatenate([-sin, zh, zr], axis=-1)
    s2 = jnp.concatenate([zh, sin, zr], axis=-1)
    rep = LANES // d_qk
    return tuple(jnp.tile(t, (1, rep)) for t in (cs, s1, s2))


def _attn_prompt_body(q_ref, k_ref, v_ref, lq1_ref, lk1_ref, lq2_ref, lk2_ref, sw_ref,
                      o_ref, kb_ref, vb_ref, m_ref, l_ref, acc_ref, *, tq, tk, d_qk, lam_init):
    qi = pl.program_id(2)

    @pl.when(qi == 0)
    def _():
        kb_ref[...] = k_ref[...].astype(BF16)
        vb_ref[...] = v_ref[...].astype(BF16)

    q = q_ref[...]
    lane = lax.broadcasted_iota(I32, q.shape, 1)
    zero = jnp.zeros_like(q)
    qm = (jnp.where(lane < d_qk, q, zero), jnp.where(lane >= d_qk, q, zero))
    m_ref[...] = jnp.full(m_ref.shape, -jnp.inf, F32)
    l_ref[...] = jnp.zeros(l_ref.shape, F32)
    acc_ref[...] = jnp.zeros(acc_ref.shape, F32)

    def step(c, masked):
        k0 = pl.multiple_of(c * tk, tk)
        kc = kb_ref[pl.ds(k0, tk), :]
        vc = vb_ref[pl.ds(k0, tk), :]
        for mi in range(2):
            s = _nt_dot(qm[mi], kc)
            if masked:
                row = lax.broadcasted_iota(I32, (tq, tk), 0)
                col = lax.broadcasted_iota(I32, (tq, tk), 1)
                s = jnp.where(col + c * tk <= row + qi * tq, s, -jnp.inf)
            m_p = m_ref[mi]
            m_n = jnp.maximum(m_p, jnp.max(s, axis=1, keepdims=True))
            p = jnp.exp(s - jnp.concatenate([m_n] * (tk // LANES), axis=1))
            al = jnp.exp(m_p - m_n)
            l_ref[mi] = al * l_ref[mi] + jnp.sum(p, axis=1, keepdims=True)
            acc_ref[mi] = al * acc_ref[mi] + jnp.dot(p.astype(BF16), vc, preferred_element_type=F32)
            m_ref[mi] = m_n

    def full_step(c, carry):
        step(c, False)
        return carry

    n_full = qi * (tq // tk)
    lax.fori_loop(0, n_full, full_step, 0)
    for dgl in range(tq // tk):
        step(n_full + dgl, True)

    lam = _lambda(lq1_ref[...], lk1_ref[...], lq2_ref[...], lk2_ref[...], lam_init)
    o = acc_ref[0] / l_ref[0] - lam * (acc_ref[1] / l_ref[1])
    o_ref[...] = (_rms(o, sw_ref[...]) * (1.0 - lam_init)).astype(o_ref.dtype)


def _attn_prompt(q, k, v, lams, subln_w, *, batch, seq, n_heads, d_qk, lam_init, tq):
    n, w = q.shape
    d_v = w // n_heads
    nq = seq // tq
    vec = pl.BlockSpec((1, d_qk), lambda b, h, i: (0, 0))
    return pl.pallas_call(
        functools.partial(_attn_prompt_body, tq=tq, tk=tq, d_qk=d_qk, lam_init=lam_init),
        grid=(batch, n_heads, nq),
        in_specs=[
            pl.BlockSpec((tq, d_v), lambda b, h, i: (b * nq + i, h)),
            pl.BlockSpec((seq, d_v), lambda b, h, i: (b, h)),
            pl.BlockSpec((seq, d_v), lambda b, h, i: (b, h)),
            vec, vec, vec, vec,
            pl.BlockSpec((1, d_v), lambda b, h, i: (0, 0)),
        ],
        out_specs=pl.BlockSpec((tq, d_v), lambda b, h, i: (b * nq + i, h)),
        out_shape=jax.ShapeDtypeStruct((n, w), BF16),
        scratch_shapes=[pltpu.VMEM((seq, d_v), BF16), pltpu.VMEM((seq, d_v), BF16),
                        pltpu.VMEM((2, tq, LANES), F32), pltpu.VMEM((2, tq, LANES), F32),
                        pltpu.VMEM((2, tq, d_v), F32)],
        compiler_params=_params(("arbitrary", "arbitrary", "arbitrary")),
        name="attn_prompt",
    )(q, k, v, *lams, subln_w)


def _attn_sample_body(pt_ref, q_ref, kn_ref, vn_ref, lq1_ref, lk1_ref, lq2_ref, lk2_ref, sw_ref,
                      *rest, n_chunk, n_new, n_maps, n_heads, qr, lam_init):
    del pt_ref
    k_refs = rest[:n_chunk]
    v_refs = rest[n_chunk:2 * n_chunk]
    o_ref = rest[2 * n_chunk]
    m_ref, l_ref, acc_ref = rest[2 * n_chunk + 1:]
    c = pl.program_id(1)
    page = k_refs[0].shape[2]
    d_qk = k_refs[0].shape[1] // n_maps
    hr = 2 * qr

    @pl.when(c == 0)
    def _():
        m_ref[...] = jnp.full(m_ref.shape, -jnp.inf, F32)
        l_ref[...] = jnp.zeros(l_ref.shape, F32)
        acc_ref[...] = jnp.zeros(acc_ref.shape, F32)

    q_all = q_ref[0]
    m_p, l_p, acc = m_ref[...], l_ref[...], acc_ref[...]
    s = jnp.concatenate([
        jnp.concatenate([
            jnp.dot(q_all[m * qr:(m + 1) * qr],
                    k_refs[i][0, m * d_qk:(m + 1) * d_qk, :].astype(BF16),
                    preferred_element_type=F32)
            for i in range(n_chunk)], axis=1)
        for m in range(n_maps)], axis=0)
    m_n = jnp.maximum(m_p, jnp.max(s, axis=1, keepdims=True))
    p = jnp.exp(s - m_n)
    al = jnp.exp(m_p - m_n)
    l_p = al * l_p + jnp.sum(p, axis=1, keepdims=True)
    pb = p.astype(BF16)
    pv = []
    for h in range(n_heads):
        vh = jnp.concatenate(
            [v_refs[i][0, pl.ds(h, page, stride=n_heads), :].astype(BF16) for i in range(n_chunk)],
            axis=0)
        pv.append(jnp.dot(pb[h * hr:(h + 1) * hr], vh, preferred_element_type=F32))
    acc = al * acc + jnp.concatenate(pv, axis=0)
    m_p = m_n
    m_ref[...] = m_p
    l_ref[...] = l_p
    acc_ref[...] = acc

    @pl.when(c == pl.num_programs(1) - 1)
    def _():
        qf = q_all.astype(F32)
        kn, vn = kn_ref[0], vn_ref[0]
        q_tok = lax.broadcasted_iota(I32, (qf.shape[0], 1), 0) % qr
        s_new = []
        for j in range(n_new):
            parts = []
            for m in range(n_maps):
                krow = kn[j * n_maps + m:j * n_maps + m + 1, :]
                parts.append(jnp.sum(qf[m * qr:(m + 1) * qr] * krow, axis=1, keepdims=True))
            s_new.append(jnp.where(q_tok >= j, jnp.concatenate(parts, axis=0), -jnp.inf))
        m_n = m_p
        for sj in s_new:
            m_n = jnp.maximum(m_n, sj)
        al = jnp.exp(m_p - m_n)
        l_n = al * l_p
        a_n = al * acc
        for j in range(n_new):
            pj = jnp.exp(s_new[j] - m_n)
            l_n = l_n + pj
            a_n = a_n + jnp.concatenate(
                [pj[h * hr:(h + 1) * hr] * vn[j * n_heads + h:j * n_heads + h + 1, :]
                 for h in range(n_heads)], axis=0)
        a_n = a_n / l_n
        lam = _lambda(lq1_ref[...], lk1_ref[...], lq2_ref[...], lk2_ref[...], lam_init)
        d_v = a_n.shape[1]
        for h in range(n_heads):
            o = a_n[h * hr:h * hr + n_new] - lam * a_n[h * hr + qr:h * hr + qr + n_new]
            o_ref[0, :, h * d_v:(h + 1) * d_v] = (
                _rms(o, sw_ref[...]) * (1.0 - lam_init)).astype(o_ref.dtype)


def _attn_sample(pages, q_rows, k_new, v_new, ck, cv, lams, subln_w, *, n_new, n_maps, n_heads, qr,
                 lam_init, n_chunk):
    dec, n_pages = pages.shape
    d_qk = ck.shape[1] // n_maps
    d_v = cv.shape[2]
    rows = n_maps * qr
    pt_flat = pages.reshape(-1)

    def page_map(i, b, c, pt):
        return (pt[b * n_pages + c * n_chunk + i], 0, 0)

    vec = pl.BlockSpec((1, d_qk), lambda b, c, pt: (0, 0))
    k_specs = [pl.BlockSpec((1,) + ck.shape[1:], functools.partial(page_map, i)) for i in range(n_chunk)]
    v_specs = [pl.BlockSpec((1,) + cv.shape[1:], functools.partial(page_map, i)) for i in range(n_chunk)]
    grid_spec = pltpu.PrefetchScalarGridSpec(
        num_scalar_prefetch=1,
        grid=(dec, n_pages // n_chunk),
        in_specs=[
            pl.BlockSpec((1, rows, d_qk), lambda b, c, pt: (b, 0, 0)),
            pl.BlockSpec((1,) + k_new.shape[1:], lambda b, c, pt: (b, 0, 0)),
            pl.BlockSpec((1,) + v_new.shape[1:], lambda b, c, pt: (b, 0, 0)),
            vec, vec, vec, vec,
            pl.BlockSpec((1, d_v), lambda b, c, pt: (0, 0)),
        ] + k_specs + v_specs,
        out_specs=pl.BlockSpec((1, n_new, n_heads * d_v), lambda b, c, pt: (b, 0, 0)),
        scratch_shapes=[pltpu.VMEM((rows, 1), F32), pltpu.VMEM((rows, 1), F32),
                        pltpu.VMEM((rows, d_v), F32)],
    )
    return pl.pallas_call(
        functools.partial(_attn_sample_body, n_chunk=n_chunk, n_new=n_new, n_maps=n_maps,
                          n_heads=n_heads, qr=qr, lam_init=lam_init),
        grid_spec=grid_spec,
        out_shape=jax.ShapeDtypeStruct((dec, n_new, n_heads * d_v), F32),
        compiler_params=_params(("arbitrary", "arbitrary")),
        name="attn_sample",
    )(pt_flat, q_rows, k_new, v_new, *lams, subln_w, *([ck] * n_chunk), *([cv] * n_chunk))


def _ln_swish(y, lw, lb):
    mu = jnp.mean(y, axis=-1, keepdims=True)
    yc = y - mu
    var = jnp.mean(yc * yc, axis=-1, keepdims=True)
    z = yc * lax.rsqrt(var + EPS) * lw + lb
    return z * jax.nn.sigmoid(z)


def _conv_prompt_body(u_ref, up_ref, dw_ref, db_ref, lw_ref, lb_ref, c_ref, buf_ref, y_ref,
                      *, tt, width, halo, rb):
    t = pl.program_id(1)
    buf_ref[0:halo, :] = jnp.where(t == 0, 0.0, up_ref[...])
    buf_ref[halo:halo + tt, :] = u_ref[...]
    off = halo - (width - 1)

    def col_body(cc, carry):
        c0 = pl.multiple_of(cc * LANES, LANES)
        wts = dw_ref[:, pl.ds(c0, LANES)]
        bias = db_ref[:, pl.ds(c0, LANES)]
        for r in range(tt // rb):
            acc = jnp.zeros((rb, LANES), F32)
            for w in range(width):
                lo = r * rb + off + w
                acc = acc + buf_ref[lo:lo + rb, pl.ds(c0, LANES)] * wts[w:w + 1, :]
            y_ref[r * rb:(r + 1) * rb, pl.ds(c0, LANES)] = acc + bias
        return carry

    lax.fori_loop(0, u_ref.shape[1] // LANES, col_body, 0)
    c_ref[...] = _ln_swish(y_ref[...], lw_ref[...], lb_ref[...]).astype(c_ref.dtype)


def _conv_prompt(u, dw_w, dw_b, ln_w, ln_b, *, batch, seq, tt):
    n, ch = u.shape
    width = dw_w.shape[0]
    halo = 32
    nt = seq // tt
    per = tt // halo
    row = pl.BlockSpec((1, ch), lambda b, t: (0, 0))
    return pl.pallas_call(
        functools.partial(_conv_prompt_body, tt=tt, width=width, halo=halo, rb=64),
        grid=(batch, nt),
        in_specs=[
            pl.BlockSpec((tt, ch), lambda b, t: (b * nt + t, 0)),
            pl.BlockSpec((halo, ch), lambda b, t: (jnp.maximum((b * nt + t) * per - 1, 0), 0)),
            pl.BlockSpec((width, ch), lambda b, t: (0, 0)),
            row, row, row,
        ],
        out_specs=pl.BlockSpec((tt, ch), lambda b, t: (b * nt + t, 0)),
        out_shape=jax.ShapeDtypeStruct((n, ch), BF16),
        scratch_shapes=[pltpu.VMEM((halo + tt, ch), F32), pltpu.VMEM((tt, ch), F32)],
        compiler_params=_params(("arbitrary", "arbitrary")),
        name="conv_prompt",
    )(u, u, dw_w, dw_b, ln_w, ln_b)


def _conv_sample_body(up_ref, dw_ref, db_ref, lw_ref, lb_ref, c_ref, y_ref, *, width, n_new):
    length, tb, ch = up_ref.shape

    def col_body(cc, carry):
        c0 = pl.multiple_of(cc * LANES, LANES)
        wts = dw_ref[:, pl.ds(c0, LANES)]
        bias = db_ref[:, pl.ds(c0, LANES)]
        accs = [jnp.zeros((tb, LANES), F32) for _ in range(n_new)]
        for j in range(length):
            x = up_ref[j, :, pl.ds(c0, LANES)]
            for t in range(n_new):
                w = j - t
                if 0 <= w < width:
                    accs[t] = accs[t] + x * wts[w:w + 1, :]
        for t in range(n_new):
            y_ref[t, :, pl.ds(c0, LANES)] = accs[t] + bias
        return carry

    lax.fori_loop(0, ch // LANES, col_body, 0)
    c_ref[...] = _ln_swish(y_ref[...], lw_ref[...], lb_ref[...]).astype(c_ref.dtype)


def _conv_sample(u_pad_t, dw_w, dw_b, ln_w, ln_b, *, n_new, tb):
    length, dec, ch = u_pad_t.shape
    width = dw_w.shape[0]
    row = pl.BlockSpec((1, ch), lambda b: (0, 0))
    return pl.pallas_call(
        functools.partial(_conv_sample_body, width=width, n_new=n_new),
        grid=(dec // tb,),
        in_specs=[
            pl.BlockSpec((length, tb, ch), lambda b: (0, b, 0)),
            pl.BlockSpec((width, ch), lambda b: (0, 0)),
            row, row, row,
        ],
        out_specs=pl.BlockSpec((n_new, tb, ch), lambda b: (0, b, 0)),
        out_shape=jax.ShapeDtypeStruct((n_new, dec, ch), BF16),
        scratch_shapes=[pltpu.VMEM((n_new, tb, ch), F32)],
        compiler_params=_params(("arbitrary",)),
        name="conv_sample",
    )(u_pad_t, dw_w, dw_b, ln_w, ln_b)


def _wo_body(o_ref, c_ref, x_ref, wo_ref, nw_ref, wr_ref, br_ref,
             x1_ref, xf_ref, idx_ref, gate_ref, *, aw, top_k):
    attn = (jnp.dot(o_ref[...], wo_ref[0:aw, :], preferred_element_type=F32)
            + jnp.dot(c_ref[...], wo_ref[aw:, :], preferred_element_type=F32))
    x1 = x_ref[...] + attn
    x1_ref[...] = x1
    xf = _rms(x1, nw_ref[...])
    xf_ref[...] = xf
    logits = jnp.dot(xf, wr_ref[...], preferred_element_type=F32,
                     precision=lax.Precision.HIGHEST) + br_ref[...]
    n_exp = logits.shape[1]
    lane_e = lax.broadcasted_iota(I32, logits.shape, 1)
    lane_o = lax.broadcasted_iota(I32, idx_ref.shape, 1)
    vals = logits
    tops, idxs = [], []
    for _ in range(top_k):
        mx = jnp.max(vals, axis=1, keepdims=True)
        am = jnp.min(jnp.where(vals == mx, lane_e, n_exp), axis=1, keepdims=True)
        tops.append(mx)
        idxs.append(am)
        vals = jnp.where(lane_e == am, -jnp.inf, vals)
    es = [jnp.exp(t - tops[0]) for t in tops]
    den = es[0]
    for e in es[1:]:
        den = den + e
    idx_out = jnp.zeros(idx_ref.shape, I32)
    gate_out = jnp.zeros(gate_ref.shape, F32)
    for k in range(top_k):
        idx_out = jnp.where(lane_o == k, idxs[k], idx_out)
        gate_out = jnp.where(lane_o == k, es[k] / den, gate_out)
    idx_ref[...] = idx_out
    gate_ref[...] = gate_out


def _wo_router(o, c, x2d, w_o_bf, norm_w, w_router, b_router, *, tt):
    n, d = x2d.shape
    aw = o.shape[1]
    n_exp = w_router.shape[1]
    row = pl.BlockSpec((tt, d), lambda i: (i, 0))
    half = pl.BlockSpec((tt, aw), lambda i: (i, 0))
    small = pl.BlockSpec((tt, LANES), lambda i: (i, 0))
    return pl.pallas_call(
        functools.partial(_wo_body, aw=aw, top_k=TOP_K),
        grid=(n // tt,),
        in_specs=[
            half, pl.BlockSpec((tt, d - aw), lambda i: (i, 0)), row,
            pl.BlockSpec((d, d), lambda i: (0, 0)),
            pl.BlockSpec((1, d), lambda i: (0, 0)),
            pl.BlockSpec((d, n_exp), lambda i: (0, 0)),
            pl.BlockSpec((1, n_exp), lambda i: (0, 0)),
        ],
        out_specs=[row, row, small, small],
        out_shape=[
            jax.ShapeDtypeStruct((n, d), F32),
            jax.ShapeDtypeStruct((n, d), F32),
            jax.ShapeDtypeStruct((n, LANES), I32),
            jax.ShapeDtypeStruct((n, LANES), F32),
        ],
        compiler_params=_params(("arbitrary",)),
        name="wo_router",
    )(o, c, x2d, w_o_bf, norm_w, w_router, b_router)


def _route_plan(idx, n_exp, tm, ts):
    n_tok, top_k = idx.shape
    n_asg = n_tok * top_k
    nsub = tm // ts
    flat_e = idx.reshape(-1)
    onehot = (flat_e[:, None] == jnp.arange(n_exp, dtype=I32)[None, :]).astype(I32)
    csum = jnp.cumsum(onehot, axis=0)
    rank = jnp.sum((csum - onehot) * onehot, axis=1)
    counts = csum[-1]
    ntile = (counts + tm - 1) // tm
    first_rows = counts - jnp.maximum(ntile - 1, 0) * tm
    tile_end = jnp.cumsum(ntile)
    tile_beg = tile_end - ntile
    fr = first_rows[flat_e]
    dest = (tile_beg[flat_e] * tm + jnp.where(rank < fr, rank, rank - fr + tm)).astype(I32)

    n_tiles = -(-n_asg // tm) + n_exp
    t_ids = jnp.arange(n_tiles, dtype=I32)
    tile_exp = jnp.minimum(jnp.sum(t_ids[:, None] >= tile_end[None, :], axis=1), n_exp - 1).astype(I32)
    n_used = tile_end[-1]
    first_sub = (first_rows[tile_exp] + ts - 1) // ts
    tile_nsub = jnp.where(t_ids < n_used, jnp.where(t_ids == tile_beg[tile_exp], first_sub, nsub), 0)
    tile_nsub = tile_nsub.astype(I32)

    n_cs = -(-n_asg // ts) + n_exp
    sub_end = jnp.cumsum(tile_nsub)
    n_csub = sub_end[-1]
    i_ids = jnp.minimum(jnp.arange(n_cs, dtype=I32), n_csub - 1)
    cs_tile = jnp.minimum(jnp.sum(i_ids[:, None] >= sub_end[None, :], axis=1), n_tiles - 1)
    cs_blk = (cs_tile * nsub + i_ids - (sub_end - tile_nsub)[cs_tile]).astype(I32)
    flat_tok = jnp.arange(n_asg, dtype=I32) // top_k
    buf_tok = jnp.zeros((n_tiles * tm,), I32).at[dest].set(flat_tok)
    tok_c = buf_tok.reshape(n_tiles * nsub, ts)[cs_blk].reshape(-1)
    return (dest, tile_exp, tile_nsub, n_used.astype(I32).reshape(1), tok_c, cs_blk,
            n_csub.astype(I32).reshape(1))


def _gather_body(tok_ref, blk_ref, ncs_ref, x_hbm, xs_ref, land_ref, sem, *, ts):
    del blk_ref
    i = pl.program_id(0)
    n = ncs_ref[0]

    def row_copy(step, slot, r):
        tok = tok_ref[step * ts + r]
        return pltpu.make_async_copy(x_hbm.at[pl.ds(tok, 1)], land_ref.at[slot, pl.ds(r, 1)],
                                     sem.at[slot])

    def start_all(step, slot):
        def body(g, carry):
            for u in range(GATHER_UNROLL):
                row_copy(step, slot, g * GATHER_UNROLL + u).start()
            return carry
        lax.fori_loop(0, ts // GATHER_UNROLL, body, 0)

    def wait_all(step, slot):
        def body(g, carry):
            for u in range(GATHER_UNROLL):
                row_copy(step, slot, g * GATHER_UNROLL + u).wait()
            return carry
        lax.fori_loop(0, ts // GATHER_UNROLL, body, 0)

    @pl.when(i == 0)
    def _():
        start_all(0, 0)

    @pl.when(i + 1 < n)
    def _():
        start_all(i + 1, (i + 1) % 2)

    @pl.when(i < n)
    def _():
        wait_all(i, i % 2)
        xs_ref[...] = land_ref[i % 2].astype(xs_ref.dtype)


def _gather_rows(xf, tok_c, cs_blk, n_csub, *, n_rows, ts):
    d = xf.shape[1]
    n_cs = cs_blk.shape[0]
    grid_spec = pltpu.PrefetchScalarGridSpec(
        num_scalar_prefetch=3,
        grid=(n_cs,),
        in_specs=[pl.BlockSpec(memory_space=pl.ANY)],
        out_specs=pl.BlockSpec((ts, d), lambda i, tok, blk, ncs: (blk[i], 0)),
        scratch_shapes=[pltpu.VMEM((2, ts, d), xf.dtype), pltpu.SemaphoreType.DMA((2,))],
    )
    return pl.pallas_call(
        functools.partial(_gather_body, ts=ts),
        grid_spec=grid_spec,
        out_shape=jax.ShapeDtypeStruct((n_rows, d), BF16),
        compiler_params=_params(("arbitrary",)),
        name="moe_gather",
    )(tok_c, cs_blk, n_csub, xf)


def _expert_changed(texp_ref, t):
    return (t == 0) | (texp_ref[t] != texp_ref[jnp.maximum(t - 1, 0)])


def _moe_up_body(texp_ref, tns_ref, nused_ref, x_ref, wg_ref, wu_ref, bg_ref, bu_ref, a_ref, wbf_ref,
                 *, ts):
    t = pl.program_id(1)

    @pl.when(t < nused_ref[0])
    def _():
        @pl.when(_expert_changed(texp_ref, t))
        def _():
            wbf_ref[0] = wg_ref[0].astype(BF16)
            wbf_ref[1] = wu_ref[0].astype(BF16)

        def sub(j, carry):
            r0 = pl.multiple_of(j * ts, ts)
            x = x_ref[pl.ds(r0, ts), :]
            g = jnp.dot(x, wbf_ref[0], preferred_element_type=F32) + bg_ref[0]
            lin = jnp.dot(x, wbf_ref[1], preferred_element_type=F32) + bu_ref[0]
            g = jnp.minimum(g, SWIGLU_LIMIT)
            lin = jnp.clip(lin, -SWIGLU_LIMIT, SWIGLU_LIMIT)
            a = g * jax.nn.sigmoid(SWIGLU_ALPHA * g) * (lin + 1.0)
            a_ref[pl.ds(r0, ts), :] = a.astype(a_ref.dtype)
            return carry

        lax.fori_loop(0, tns_ref[t], sub, 0)


def _moe_down_body(texp_ref, tns_ref, nused_ref, a_ref, wd_ref, bd_ref, y_ref, wbf_ref, *, ts):
    t = pl.program_id(1)

    @pl.when(t < nused_ref[0])
    def _():
        @pl.when(_expert_changed(texp_ref, t))
        def _():
            wbf_ref[...] = wd_ref[0].astype(BF16)

        def sub(j, carry):
            r0 = pl.multiple_of(j * ts, ts)
            y_ref[pl.ds(r0, ts), :] = jnp.dot(a_ref[pl.ds(r0, ts), :], wbf_ref[...],
                                              preferred_element_type=F32) + bd_ref[0]
            return carry

        lax.fori_loop(0, tns_ref[t], sub, 0)


def _moe_experts(xs, tile_exp, tile_nsub, n_used, w_gate_up, b_gate_up, w_down, b_down, *, tm, ts, tf, tn):
    n_rows, d = xs.shape
    n_exp, _, two_ff = w_gate_up.shape
    d_ff = two_ff // 2
    n_f = d_ff // tf
    n_t = n_rows // tm
    bgu = b_gate_up.reshape(n_exp, 1, two_ff)
    bdn = b_down.reshape(n_exp, 1, d)

    def tile(t, nu):
        return jnp.minimum(t, nu[0] - 1)

    up_spec = pltpu.PrefetchScalarGridSpec(
        num_scalar_prefetch=3,
        grid=(n_f, n_t),
        in_specs=[
            pl.BlockSpec((tm, d), lambda f, t, te, ns, nu: (tile(t, nu), 0)),
            pl.BlockSpec((1, d, tf), lambda f, t, te, ns, nu: (te[tile(t, nu)], 0, f)),
            pl.BlockSpec((1, d, tf), lambda f, t, te, ns, nu: (te[tile(t, nu)], 0, n_f + f)),
            pl.BlockSpec((1, 1, tf), lambda f, t, te, ns, nu: (te[tile(t, nu)], 0, f)),
            pl.BlockSpec((1, 1, tf), lambda f, t, te, ns, nu: (te[tile(t, nu)], 0, n_f + f)),
        ],
        out_specs=pl.BlockSpec((tm, tf), lambda f, t, te, ns, nu: (tile(t, nu), f)),
        scratch_shapes=[pltpu.VMEM((2, d, tf), BF16)],
    )
    act = pl.pallas_call(
        functools.partial(_moe_up_body, ts=ts),
        grid_spec=up_spec,
        out_shape=jax.ShapeDtypeStruct((n_rows, d_ff), BF16),
        compiler_params=_params(("arbitrary", "arbitrary")),
        name="moe_up",
    )(tile_exp, tile_nsub, n_used, xs, w_gate_up, w_gate_up, bgu, bgu)

    n_n = d // tn
    down_spec = pltpu.PrefetchScalarGridSpec(
        num_scalar_prefetch=3,
        grid=(n_n, n_t),
        in_specs=[
            pl.BlockSpec((tm, d_ff), lambda n, t, te, ns, nu: (tile(t, nu), 0)),
            pl.BlockSpec((1, d_ff, tn), lambda n, t, te, ns, nu: (te[tile(t, nu)], 0, n)),
            pl.BlockSpec((1, 1, tn), lambda n, t, te, ns, nu: (te[tile(t, nu)], 0, n)),
        ],
        out_specs=pl.BlockSpec((tm, tn), lambda n, t, te, ns, nu: (tile(t, nu), n)),
        scratch_shapes=[pltpu.VMEM((d_ff, tn), BF16)],
    )
    return pl.pallas_call(
        functools.partial(_moe_down_body, ts=ts),
        grid_spec=down_spec,
        out_shape=jax.ShapeDtypeStruct((n_rows, d), F32),
        compiler_params=_params(("arbitrary", "arbitrary")),
        name="moe_down",
    )(tile_exp, tile_nsub, n_used, act, w_down, bdn)


def _combine_body(dest_ref, x1_ref, gate_ref, nw_ref, yb_hbm, out_ref, rows_ref, sem, *, tt, top_k):
    i = pl.program_id(0)

    def row_copy(r, k):
        d = dest_ref[(i * tt + r) * top_k + k]
        return pltpu.make_async_copy(yb_hbm.at[pl.ds(d, 1)], rows_ref.at[k, pl.ds(r, 1)], sem)

    rows_per_iter = GATHER_UNROLL // top_k

    def issue(g, carry):
        for u in range(rows_per_iter):
            for k in range(top_k):
                row_copy(g * rows_per_iter + u, k).start()
        return carry

    def drain(g, carry):
        for u in range(rows_per_iter):
            for k in range(top_k):
                row_copy(g * rows_per_iter + u, k).wait()
        return carry

    lax.fori_loop(0, tt // rows_per_iter, issue, 0)
    lax.fori_loop(0, tt // rows_per_iter, drain, 0)
    gate = gate_ref[...]
    moe = rows_ref[0] * gate[:, 0:1]
    for k in range(1, top_k):
        moe = moe + rows_ref[k] * gate[:, k:k + 1]
    out_ref[...] = _rms(x1_ref[...] + moe, nw_ref[...])


def _combine(dest, x1, gate, norm_w, yb, *, tt):
    n, d = x1.shape
    grid_spec = pltpu.PrefetchScalarGridSpec(
        num_scalar_prefetch=1,
        grid=(n // tt,),
        in_specs=[
            pl.BlockSpec((tt, d), lambda i, ds: (i, 0)),
            pl.BlockSpec((tt, LANES), lambda i, ds: (i, 0)),
            pl.BlockSpec((1, d), lambda i, ds: (0, 0)),
            pl.BlockSpec(memory_space=pl.ANY),
        ],
        out_specs=pl.BlockSpec((tt, d), lambda i, ds: (i, 0)),
        scratch_shapes=[pltpu.VMEM((TOP_K, tt, d), F32), pltpu.SemaphoreType.DMA],
    )
    return pl.pallas_call(
        functools.partial(_combine_body, tt=tt, top_k=TOP_K),
        grid_spec=grid_spec,
        out_shape=jax.ShapeDtypeStruct((n, d), F32),
        compiler_params=_params(("arbitrary",)),
        name="moe_combine",
    )(dest, x1, gate, norm_w, yb)


def _tile(n, pref):
    t = min(n, pref)
    while n % t:
        t //= 2
    return t


def kernel(x_prompt, x_sample, cache_k, cache_v, state_conv, page_table, norm_mix_w, w_in,
           lambda_q1, lambda_k1, lambda_q2, lambda_k2, subln_w, conv_dw_w, conv_dw_b,
           conv_norm_w, conv_norm_b, w_o, norm_ffn_w, w_router, b_router, w_gate_up, b_gate_up,
           w_down, b_down, norm_final_w):
    batch, seq, d = x_prompt.shape
    dec, n_new, _ = x_sample.shape
    depth, n_pool, page, n_maps, d_qk = cache_k.shape
    n_heads, d_v = cache_v.shape[3:]
    aw = n_heads * d_v
    ch = d - aw
    n_pages = page_table.shape[1]
    past = n_pages * page
    rope_dim = d_qk // 4
    half = rope_dim // 2
    scale = d_qk ** -0.5
    n_exp = w_router.shape[2]
    n_p, n_s = batch * seq, dec * n_new

    h_p = x_prompt.reshape(n_p, d)
    h_s = x_sample.reshape(n_s, d)
    tab_p = _rope_tables(jnp.arange(seq, dtype=I32), d_qk, rope_dim)
    tab_s = _rope_tables(past + jnp.arange(n_s, dtype=I32) % n_new, d_qk, rope_dim)
    outs = {k: [] for k in ("kp", "vp", "cp", "ks", "vs", "cs")}

    for l in range(depth):
        lam_init = 0.8 - 0.6 * math.exp(-0.3 * l)
        lams = tuple(v[l].reshape(1, d_qk) for v in (lambda_q1, lambda_k1, lambda_q2, lambda_k2))
        sw = subln_w[l].reshape(1, d_v)
        w_in_bf = w_in[l].astype(BF16)
        w_o_bf = w_o[l].astype(BF16)
        nmw = norm_mix_w[l].reshape(1, d)
        dw_b = conv_dw_b[l].reshape(1, ch)
        ln_w = conv_norm_w[l].reshape(1, ch)
        ln_b = conv_norm_b[l].reshape(1, ch)

        tt_p = _tile(seq, 512)
        q_p, k_p, v_p, u_p = _proj(h_p, nmw, w_in_bf, tab_p, scale=scale, half=half, tt=tt_p)
        tt_s = _tile(n_s, 512)
        q_s, k_s, v_s, u_s = _proj(h_s, nmw, w_in_bf, tab_s, scale=scale, half=half, tt=tt_s)

        o_p = _attn_prompt(q_p, k_p, v_p, lams, sw, batch=batch, seq=seq, n_heads=n_heads,
                           d_qk=d_qk, lam_init=lam_init, tq=_tile(seq, 512))
        q_rows = q_s.reshape(dec, n_new, n_maps, d_qk).transpose(0, 2, 1, 3)
        q_rows = jnp.pad(q_rows, ((0, 0), (0, 0), (0, QR - n_new), (0, 0))).reshape(dec, n_maps * QR, d_qk)
        ck = cache_k.transpose(0, 1, 3, 4, 2).reshape(depth * n_pool, n_maps * d_qk, page)
        cv = cache_v.reshape(depth * n_pool, page * n_heads, d_v)
        o_s = _attn_sample(page_table + l * n_pool, q_rows, k_s.reshape(dec, n_new * n_maps, d_qk),
                           v_s.reshape(dec, n_new * n_heads, d_v), ck, cv, lams, sw, n_new=n_new,
                           n_maps=n_maps, n_heads=n_heads, qr=QR, lam_init=lam_init,
                           n_chunk=_tile(n_pages, 8))
        o_s = o_s.reshape(n_s, aw).astype(BF16)

        c_p = _conv_prompt(u_p, conv_dw_w[l], dw_b, ln_w, ln_b, batch=batch, seq=seq,
                           tt=_tile(seq, 256))
        u_s3 = u_s.reshape(dec, n_new, ch)
        u_pad_s = jnp.concatenate([state_conv[l], u_s3], axis=1)
        c_s = _conv_sample(u_pad_s.transpose(1, 0, 2), conv_dw_w[l], dw_b, ln_w, ln_b,
                           n_new=n_new, tb=_tile(dec, 16))
        c_s = c_s.transpose(1, 0, 2).reshape(n_s, ch)

        nfw = norm_ffn_w[l].reshape(1, d)
        br = b_router[l].reshape(1, n_exp)
        x1_p, xf_p, idx_p, gate_p = _wo_router(o_p, c_p, h_p, w_o_bf, nfw, w_router[l], br, tt=tt_p)
        x1_s, xf_s, idx_s, gate_s = _wo_router(o_s, c_s, h_s, w_o_bf, nfw, w_router[l], br, tt=tt_s)

        idx_all = jnp.concatenate([idx_p[:, :TOP_K], idx_s[:, :TOP_K]], axis=0)
        dest, tile_exp, tile_nsub, n_used, tok_c, cs_blk, n_csub = _route_plan(
            idx_all, n_exp, MOE_TILE, MOE_SUBTILE)
        xf_all = jnp.concatenate([xf_p, xf_s], axis=0)
        xs = _gather_rows(xf_all, tok_c, cs_blk, n_csub, n_rows=tile_exp.shape[0] * MOE_TILE,
                          ts=MOE_SUBTILE)
        yb = _moe_experts(xs, tile_exp, tile_nsub, n_used, w_gate_up[l], b_gate_up[l], w_down[l],
                          b_down[l], tm=MOE_TILE, ts=MOE_SUBTILE, tf=_tile(w_down.shape[2], 1024),
                          tn=_tile(d, 1024))

        last = l == depth - 1
        nw_out = norm_final_w.reshape(1, d) if last else jnp.ones((1, d), F32)
        assert last, "the combine kernel fuses the final norm; deeper stacks need an un-normed variant"
        h_p = _combine(dest[:n_p * TOP_K], x1_p, gate_p, nw_out, yb, tt=_tile(n_p, 256))
        h_s = _combine(dest[n_p * TOP_K:], x1_s, gate_s, nw_out, yb, tt=_tile(n_s, 256))

        outs["kp"].append(k_p.reshape(batch, seq, n_maps, d_qk))
        outs["vp"].append(v_p.reshape(batch, seq, n_heads, d_v))
        outs["cp"].append(u_p.reshape(batch, seq, ch)[:, seq - (conv_dw_w.shape[1] - 1):])
        outs["ks"].append(k_s.reshape(dec, n_new, n_maps, d_qk))
        outs["vs"].append(v_s.reshape(dec, n_new, n_heads, d_v))
        outs["cs"].append(u_pad_s[:, n_new:])

    return (h_p.reshape(batch, seq, d), h_s.reshape(dec, n_new, d),
            jnp.stack(outs["kp"]), jnp.stack(outs["vp"]), jnp.stack(outs["cp"]),
            jnp.stack(outs["ks"]), jnp.stack(outs["vs"]), jnp.stack(outs["cs"]))
```

```python
import functools
import math

import jax
import jax.numpy as jnp
from jax import lax
from jax.experimental import pallas as pl
from jax.experimental.pallas import tpu as pltpu

F32 = jnp.float32
BF16 = jnp.bfloat16
I32 = jnp.int32

EPS = 1e-5
ROPE_THETA = 500000.0
TOP_K = 4
SWIGLU_LIMIT = 7.0
SWIGLU_ALPHA = 1.702
LANES = 128
VMEM_LIMIT = 56 * 1024 * 1024
QR = 8
MOE_TILE = 768
MOE_SUBTILE = 256
GATHER_UNROLL = 8


def _params(sem, vmem=VMEM_LIMIT):
    return pltpu.CompilerParams(dimension_semantics=sem, vmem_limit_bytes=vmem)


def _nt_dot(a, b):
    return lax.dot_general(a, b, (((1,), (1,)), ((), ())), preferred_element_type=F32)


def _rms(x, w):
    ms = jnp.mean(x * x, axis=-1, keepdims=True)
    return x * lax.rsqrt(ms + EPS) * w


def _lambda(lq1, lk1, lq2, lk2, lam_init):
    a = jnp.sum(lq1 * lk1, axis=-1, keepdims=True)
    b = jnp.sum(lq2 * lk2, axis=-1, keepdims=True)
    return jnp.exp(a) - jnp.exp(b) + lam_init


def _proj_body(x_ref, nw_ref, wa_ref, wb_ref, cs_ref, s1_ref, s2_ref,
               q_ref, k_ref, v_ref, u_ref, xn_ref, *, scale, half):
    j = pl.program_id(1)

    @pl.when(j == 0)
    def _():
        xn_ref[...] = _rms(x_ref[...], nw_ref[...]).astype(BF16)

    def rope_store(dst_ref, p, mul):
        cs, s1, s2 = cs_ref[...], s1_ref[...], s2_ref[...]
        for c in range(p.shape[1] // LANES):
            seg = p[:, c * LANES:(c + 1) * LANES]
            r = (seg * cs + pltpu.roll(seg, LANES - half, 1) * s1
                 + pltpu.roll(seg, half, 1) * s2)
            if mul is not None:
                r = r * mul
            dst_ref[:, c * LANES:(c + 1) * LANES] = r.astype(dst_ref.dtype)

    @pl.when(j == 0)
    def _():
        p = jnp.dot(xn_ref[...], wa_ref[...], preferred_element_type=F32)
        rope_store(q_ref, p, scale)

    @pl.when(j == 1)
    def _():
        p = jnp.dot(xn_ref[...], wa_ref[...], preferred_element_type=F32)
        rope_store(k_ref, p, None)

    @pl.when(j == 2)
    def _():
        v_ref[...] = jnp.dot(xn_ref[...], wa_ref[...], preferred_element_type=F32)

    @pl.when(j == 3)
    def _():
        xn = xn_ref[...]
        val = jnp.dot(xn, wa_ref[...], preferred_element_type=F32)
        gate = jnp.dot(xn, wb_ref[...], preferred_element_type=F32)
        u_ref[...] = val * jax.nn.sigmoid(gate)


def _proj(x2d, norm_w, w_in_bf, tables, *, scale, half, tt):
    n, d = x2d.shape
    w = w_in_bf.shape[1] // 5
    cs, s1, s2 = tables
    n_pos_blocks = cs.shape[0] // tt
    tab_spec = pl.BlockSpec((tt, LANES), lambda i, j: (i % n_pos_blocks, 0))
    out_spec = pl.BlockSpec((tt, w), lambda i, j: (i, 0))
    return pl.pallas_call(
        functools.partial(_proj_body, scale=scale, half=half),
        grid=(n // tt, 4),
        in_specs=[
            pl.BlockSpec((tt, d), lambda i, j: (i, 0)),
            pl.BlockSpec((1, d), lambda i, j: (0, 0)),
            pl.BlockSpec((d, w), lambda i, j: (0, j)),
            pl.BlockSpec((d, w), lambda i, j: (0, 4)),
            tab_spec, tab_spec, tab_spec,
        ],
        out_specs=[out_spec, out_spec, out_spec, out_spec],
        out_shape=[
            jax.ShapeDtypeStruct((n, w), BF16),
            jax.ShapeDtypeStruct((n, w), F32),
            jax.ShapeDtypeStruct((n, w), F32),
            jax.ShapeDtypeStruct((n, w), F32),
        ],
        scratch_shapes=[pltpu.VMEM((tt, d), BF16)],
        compiler_params=_params(("arbitrary", "arbitrary")),
        name="proj",
    )(x2d, norm_w, w_in_bf, w_in_bf, cs, s1, s2)


def _rope_tables(pos, d_qk, rope_dim):
    half = rope_dim // 2
    inv = ROPE_THETA ** (-jnp.arange(0, rope_dim, 2, dtype=F32) / rope_dim)
    ang = pos.astype(F32)[:, None] * inv[None, :]
    cos, sin = jnp.cos(ang), jnp.sin(ang)
    p = pos.shape[0]
    zh = jnp.zeros((p, half), F32)
    zr = jnp.zeros((p, d_qk - rope_dim), F32)
    cs = jnp.concatenate([cos, cos, jnp.ones((p, d_qk - rope_dim), F32)], axis=-1)
    s1 = jnp.concatenate([-sin, zh, zr], axis=-1)
    s2 = jnp.concatenate([zh, sin, zr], axis=-1)
    rep = LANES // d_qk
    return tuple(jnp.tile(t, (1, rep)) for t in (cs, s1, s2))


def _attn_prompt_body(q_ref, k_ref, v_ref, lq1_ref, lk1_ref, lq2_ref, lk2_ref, sw_ref,
                      o_ref, kb_ref, vb_ref, m_ref, l_ref, acc_ref, *, tq, tk, d_qk, lam_init):
    qi = pl.program_id(2)

    @pl.when(qi == 0)
    def _():
        kb_ref[...] = k_ref[...].astype(BF16)
        vb_ref[...] = v_ref[...].astype(BF16)

    q = q_ref[...]
    lane = lax.broadcasted_iota(I32, q.shape, 1)
    zero = jnp.zeros_like(q)
    qm = (jnp.where(lane < d_qk, q, zero), jnp.where(lane >= d_qk, q, zero))
    m_ref[...] = jnp.full(m_ref.shape, -jnp.inf, F32)
    l_ref[...] = jnp.zeros(l_ref.shape, F32)
    acc_ref[...] = jnp.zeros(acc_ref.shape, F32)

    def step(c, masked):
        k0 = pl.multiple_of(c * tk, tk)
        kc = kb_ref[pl.ds(k0, tk), :]
        vc = vb_ref[pl.ds(k0, tk), :]
        for mi in range(2):
            s = _nt_dot(qm[mi], kc)
            if masked:
                row = lax.broadcasted_iota(I32, (tq, tk), 0)
                col = lax.broadcasted_iota(I32, (tq, tk), 1)
                s = jnp.where(col + c * tk <= row + qi * tq, s, -jnp.inf)
            m_p = m_ref[mi]
            m_n = jnp.maximum(m_p, jnp.max(s, axis=1, keepdims=True))
            p = jnp.exp(s - jnp.concatenate([m_n] * (tk // LANES), axis=1))
            al = jnp.exp(m_p - m_n)
            l_ref[mi] = al * l_ref[mi] + jnp.sum(p, axis=1, keepdims=True)
            acc_ref[mi] = al * acc_ref[mi] + jnp.dot(p.astype(BF16), vc, preferred_element_type=F32)
            m_ref[mi] = m_n

    def full_step(c, carry):
        step(c, False)
        return carry

    n_full = qi * (tq // tk)
    lax.fori_loop(0, n_full, full_step, 0)
    for dgl in range(tq // tk):
        step(n_full + dgl, True)

    lam = _lambda(lq1_ref[...], lk1_ref[...], lq2_ref[...], lk2_ref[...], lam_init)
    o = acc_ref[0] / l_ref[0] - lam * (acc_ref[1] / l_ref[1])
    o_ref[...] = (_rms(o, sw_ref[...]) * (1.0 - lam_init)).astype(o_ref.dtype)


def _attn_prompt(q, k, v, lams, subln_w, *, batch, seq, n_heads, d_qk, lam_init, tq):
    n, w = q.shape
    d_v = w // n_heads
    nq = seq // tq
    vec = pl.BlockSpec((1, d_qk), lambda b, h, i: (0, 0))
    return pl.pallas_call(
        functools.partial(_attn_prompt_body, tq=tq, tk=tq, d_qk=d_qk, lam_init=lam_init),
        grid=(batch, n_heads, nq),
        in_specs=[
            pl.BlockSpec((tq, d_v), lambda b, h, i: (b * nq + i, h)),
            pl.BlockSpec((seq, d_v), lambda b, h, i: (b, h)),
            pl.BlockSpec((seq, d_v), lambda b, h, i: (b, h)),
            vec, vec, vec, vec,
            pl.BlockSpec((1, d_v), lambda b, h, i: (0, 0)),
        ],
        out_specs=pl.BlockSpec((tq, d_v), lambda b, h, i: (b * nq + i, h)),
        out_shape=jax.ShapeDtypeStruct((n, w), BF16),
        scratch_shapes=[pltpu.VMEM((seq, d_v), BF16), pltpu.VMEM((seq, d_v), BF16),
                        pltpu.VMEM((2, tq, LANES), F32), pltpu.VMEM((2, tq, LANES), F32),
                        pltpu.VMEM((2, tq, d_v), F32)],
        compiler_params=_params(("arbitrary", "arbitrary", "arbitrary")),
        name="attn_prompt",
    )(q, k, v, *lams, subln_w)


def _attn_sample_body(pt_ref, q_ref, kn_ref, vn_ref, lq1_ref, lk1_ref, lq2_ref, lk2_ref, sw_ref,
                      *rest, n_chunk, n_new, n_maps, n_heads, qr, lam_init):
    del pt_ref
    k_refs = rest[:n_chunk]
    v_refs = rest[n_chunk:2 * n_chunk]
    o_ref = rest[2 * n_chunk]
    m_ref, l_ref, acc_ref, qbd_ref = rest[2 * n_chunk + 1:]
    c = pl.program_id(1)
    page = k_refs[0].shape[2]
    d_qk = k_refs[0].shape[1] // n_maps
    hr = 2 * qr

    @pl.when(c == 0)
    def _():
        m_ref[...] = jnp.full(m_ref.shape, -jnp.inf, F32)
        l_ref[...] = jnp.zeros(l_ref.shape, F32)
        acc_ref[...] = jnp.zeros(acc_ref.shape, F32)

    q_all = q_ref[0]

    @pl.when(c == 0)
    def _():
        q_rep = jnp.concatenate([q_all] * n_maps, axis=1)
        row_map = lax.broadcasted_iota(I32, q_rep.shape, 0) // qr
        col_map = lax.broadcasted_iota(I32, q_rep.shape, 1) // d_qk
        qbd_ref[...] = jnp.where(row_map == col_map, q_rep, jnp.zeros_like(q_rep))

    m_p, l_p, acc = m_ref[...], l_ref[...], acc_ref[...]
    k_cat = jnp.concatenate([k_refs[i][0].astype(BF16) for i in range(n_chunk)], axis=1)
    s = jnp.dot(qbd_ref[...], k_cat, preferred_element_type=F32)
    m_n = jnp.maximum(m_p, jnp.max(s, axis=1, keepdims=True))
    p = jnp.exp(s - m_n)
    al = jnp.exp(m_p - m_n)
    l_p = al * l_p + jnp.sum(p, axis=1, keepdims=True)
    pb = p.astype(BF16)
    pv = []
    for h in range(n_heads):
        vh = jnp.concatenate(
            [v_refs[i][0, pl.ds(h, page, stride=n_heads), :].astype(BF16) for i in range(n_chunk)],
            axis=0)
        pv.append(jnp.dot(pb[h * hr:(h + 1) * hr], vh, preferred_element_type=F32))
    acc = al * acc + jnp.concatenate(pv, axis=0)
    m_p = m_n
    m_ref[...] = m_p
    l_ref[...] = l_p
    acc_ref[...] = acc

    @pl.when(c == pl.num_programs(1) - 1)
    def _():
        qf = q_all.astype(F32)
        kn, vn = kn_ref[0], vn_ref[0]
        q_tok = lax.broadcasted_iota(I32, (qf.shape[0], 1), 0) % qr
        s_new = []
        for j in range(n_new):
            parts = []
            for m in range(n_maps):
                krow = kn[j * n_maps + m:j * n_maps + m + 1, :]
                parts.append(jnp.sum(qf[m * qr:(m + 1) * qr] * krow, axis=1, keepdims=True))
            s_new.append(jnp.where(q_tok >= j, jnp.concatenate(parts, axis=0), -jnp.inf))
        m_n = m_p
        for sj in s_new:
            m_n = jnp.maximum(m_n, sj)
        al = jnp.exp(m_p - m_n)
        l_n = al * l_p
        a_n = al * acc
        for j in range(n_new):
            pj = jnp.exp(s_new[j] - m_n)
            l_n = l_n + pj
            a_n = a_n + jnp.concatenate(
                [pj[h * hr:(h + 1) * hr] * vn[j * n_heads + h:j * n_heads + h + 1, :]
                 for h in range(n_heads)], axis=0)
        a_n = a_n / l_n
        lam = _lambda(lq1_ref[...], lk1_ref[...], lq2_ref[...], lk2_ref[...], lam_init)
        d_v = a_n.shape[1]
        for h in range(n_heads):
            o = a_n[h * hr:h * hr + n_new] - lam * a_n[h * hr + qr:h * hr + qr + n_new]
            o_ref[0, :, h * d_v:(h + 1) * d_v] = (
                _rms(o, sw_ref[...]) * (1.0 - lam_init)).astype(o_ref.dtype)


def _attn_sample(pages, q_rows, k_new, v_new, ck, cv, lams, subln_w, *, n_new, n_maps, n_heads, qr,
                 lam_init, n_chunk):
    dec, n_pages = pages.shape
    d_qk = ck.shape[1] // n_maps
    d_v = cv.shape[2]
    rows = n_maps * qr
    pt_flat = pages.reshape(-1)

    def page_map(i, b, c, pt):
        return (pt[b * n_pages + c * n_chunk + i], 0, 0)

    vec = pl.BlockSpec((1, d_qk), lambda b, c, pt: (0, 0))
    k_specs = [pl.BlockSpec((1,) + ck.shape[1:], functools.partial(page_map, i)) for i in range(n_chunk)]
    v_specs = [pl.BlockSpec((1,) + cv.shape[1:], functools.partial(page_map, i)) for i in range(n_chunk)]
    grid_spec = pltpu.PrefetchScalarGridSpec(
        num_scalar_prefetch=1,
        grid=(dec, n_pages // n_chunk),
        in_specs=[
            pl.BlockSpec((1, rows, d_qk), lambda b, c, pt: (b, 0, 0)),
            pl.BlockSpec((1,) + k_new.shape[1:], lambda b, c, pt: (b, 0, 0)),
            pl.BlockSpec((1,) + v_new.shape[1:], lambda b, c, pt: (b, 0, 0)),
            vec, vec, vec, vec,
            pl.BlockSpec((1, d_v), lambda b, c, pt: (0, 0)),
        ] + k_specs + v_specs,
        out_specs=pl.BlockSpec((1, n_new, n_heads * d_v), lambda b, c, pt: (b, 0, 0)),
        scratch_shapes=[pltpu.VMEM((rows, 1), F32), pltpu.VMEM((rows, 1), F32),
                        pltpu.VMEM((rows, d_v), F32), pltpu.VMEM((rows, n_maps * d_qk), BF16)],
    )
    return pl.pallas_call(
        functools.partial(_attn_sample_body, n_chunk=n_chunk, n_new=n_new, n_maps=n_maps,
                          n_heads=n_heads, qr=qr, lam_init=lam_init),
        grid_spec=grid_spec,
        out_shape=jax.ShapeDtypeStruct((dec, n_new, n_heads * d_v), F32),
        compiler_params=_params(("arbitrary", "arbitrary")),
        name="attn_sample",
    )(pt_flat, q_rows, k_new, v_new, *lams, subln_w, *([ck] * n_chunk), *([cv] * n_chunk))


def _ln_swish(y, lw, lb):
    mu = jnp.mean(y, axis=-1, keepdims=True)
    yc = y - mu
    var = jnp.mean(yc * yc, axis=-1, keepdims=True)
    z = yc * lax.rsqrt(var + EPS) * lw + lb
    return z * jax.nn.sigmoid(z)


def _conv_prompt_body(u_ref, up_ref, dw_ref, db_ref, lw_ref, lb_ref, c_ref, buf_ref, y_ref,
                      *, tt, width, halo, rb):
    t = pl.program_id(1)
    buf_ref[0:halo, :] = jnp.where(t == 0, 0.0, up_ref[...])
    buf_ref[halo:halo + tt, :] = u_ref[...]
    off = halo - (width - 1)

    def col_body(cc, carry):
        c0 = pl.multiple_of(cc * LANES, LANES)
        wts = dw_ref[:, pl.ds(c0, LANES)]
        bias = db_ref[:, pl.ds(c0, LANES)]
        for r in range(tt // rb):
            acc = jnp.zeros((rb, LANES), F32)
            for w in range(width):
                lo = r * rb + off + w
                acc = acc + buf_ref[lo:lo + rb, pl.ds(c0, LANES)] * wts[w:w + 1, :]
            y_ref[r * rb:(r + 1) * rb, pl.ds(c0, LANES)] = acc + bias
        return carry

    lax.fori_loop(0, u_ref.shape[1] // LANES, col_body, 0)
    c_ref[...] = _ln_swish(y_ref[...], lw_ref[...], lb_ref[...]).astype(c_ref.dtype)


def _conv_prompt(u, dw_w, dw_b, ln_w, ln_b, *, batch, seq, tt):
    n, ch = u.shape
    width = dw_w.shape[0]
    halo = 32
    nt = seq // tt
    per = tt // halo
    row = pl.BlockSpec((1, ch), lambda b, t: (0, 0))
    return pl.pallas_call(
        functools.partial(_conv_prompt_body, tt=tt, width=width, halo=halo, rb=64),
        grid=(batch, nt),
        in_specs=[
            pl.BlockSpec((tt, ch), lambda b, t: (b * nt + t, 0)),
            pl.BlockSpec((halo, ch), lambda b, t: (jnp.maximum((b * nt + t) * per - 1, 0), 0)),
            pl.BlockSpec((width, ch), lambda b, t: (0, 0)),
            row, row, row,
        ],
        out_specs=pl.BlockSpec((tt, ch), lambda b, t: (b * nt + t, 0)),
        out_shape=jax.ShapeDtypeStruct((n, ch), BF16),
        scratch_shapes=[pltpu.VMEM((halo + tt, ch), F32), pltpu.VMEM((tt, ch), F32)],
        compiler_params=_params(("arbitrary", "arbitrary")),
        name="conv_prompt",
    )(u, u, dw_w, dw_b, ln_w, ln_b)


def _conv_sample_body(up_ref, dw_ref, db_ref, lw_ref, lb_ref, c_ref, y_ref, *, width, n_new):
    length, tb, ch = up_ref.shape

    def col_body(cc, carry):
        c0 = pl.multiple_of(cc * LANES, LANES)
        wts = dw_ref[:, pl.ds(c0, LANES)]
        bias = db_ref[:, pl.ds(c0, LANES)]
        accs = [jnp.zeros((tb, LANES), F32) for _ in range(n_new)]
        for j in range(length):
            x = up_ref[j, :, pl.ds(c0, LANES)]
            for t in range(n_new):
                w = j - t
                if 0 <= w < width:
                    accs[t] = accs[t] + x * wts[w:w + 1, :]
        for t in range(n_new):
            y_ref[t, :, pl.ds(c0, LANES)] = accs[t] + bias
        return carry

    lax.fori_loop(0, ch // LANES, col_body, 0)
    c_ref[...] = _ln_swish(y_ref[...], lw_ref[...], lb_ref[...]).astype(c_ref.dtype)


def _conv_sample(u_pad_t, dw_w, dw_b, ln_w, ln_b, *, n_new, tb):
    length, dec, ch = u_pad_t.shape
    width = dw_w.shape[0]
    row = pl.BlockSpec((1, ch), lambda b: (0, 0))
    return pl.pallas_call(
        functools.partial(_conv_sample_body, width=width, n_new=n_new),
        grid=(dec // tb,),
        in_specs=[
            pl.BlockSpec((length, tb, ch), lambda b: (0, b, 0)),
            pl.BlockSpec((width, ch), lambda b: (0, 0)),
            row, row, row,
        ],
        out_specs=pl.BlockSpec((n_new, tb, ch), lambda b: (0, b, 0)),
        out_shape=jax.ShapeDtypeStruct((n_new, dec, ch), BF16),
        scratch_shapes=[pltpu.VMEM((n_new, tb, ch), F32)],
        compiler_params=_params(("arbitrary",)),
        name="conv_sample",
    )(u_pad_t, dw_w, dw_b, ln_w, ln_b)


def _wo_body(o_ref, c_ref, x_ref, wo_ref, nw_ref, wr_ref, br_ref,
             x1_ref, xf_ref, idx_ref, gate_ref, *, aw, top_k):
    attn = (jnp.dot(o_ref[...], wo_ref[0:aw, :], preferred_element_type=F32)
            + jnp.dot(c_ref[...], wo_ref[aw:, :], preferred_element_type=F32))
    x1 = x_ref[...] + attn
    x1_ref[...] = x1
    xf = _rms(x1, nw_ref[...])
    xf_ref[...] = xf
    xh = xf.astype(BF16)
    xl = (xf - xh.astype(F32)).astype(BF16)
    wr = wr_ref[...]
    wh = wr.astype(BF16)
    wl = (wr - wh.astype(F32)).astype(BF16)
    logits = (jnp.dot(xh, wh, preferred_element_type=F32) + jnp.dot(xl, wh, preferred_element_type=F32)
              + jnp.dot(xh, wl, preferred_element_type=F32)) + br_ref[...]
    n_exp = logits.shape[1]
    lane_e = lax.broadcasted_iota(I32, logits.shape, 1)
    lane_o = lax.broadcasted_iota(I32, idx_ref.shape, 1)
    vals = logits
    tops, idxs = [], []
    for _ in range(top_k):
        mx = jnp.max(vals, axis=1, keepdims=True)
        am = jnp.min(jnp.where(vals == mx, lane_e, n_exp), axis=1, keepdims=True)
        tops.append(mx)
        idxs.append(am)
        vals = jnp.where(lane_e == am, -jnp.inf, vals)
    es = [jnp.exp(t - tops[0]) for t in tops]
    den = es[0]
    for e in es[1:]:
        den = den + e
    idx_out = jnp.zeros(idx_ref.shape, I32)
    gate_out = jnp.zeros(gate_ref.shape, F32)
    for k in range(top_k):
        idx_out = jnp.where(lane_o == k, idxs[k], idx_out)
        gate_out = jnp.where(lane_o == k, es[k] / den, gate_out)
    idx_ref[...] = idx_out
    gate_ref[...] = gate_out


def _wo_router(o, c, x2d, w_o_bf, norm_w, w_router, b_router, *, tt):
    n, d = x2d.shape
    aw = o.shape[1]
    n_exp = w_router.shape[1]
    row = pl.BlockSpec((tt, d), lambda i: (i, 0))
    half = pl.BlockSpec((tt, aw), lambda i: (i, 0))
    small = pl.BlockSpec((tt, LANES), lambda i: (i, 0))
    return pl.pallas_call(
        functools.partial(_wo_body, aw=aw, top_k=TOP_K),
        grid=(n // tt,),
        in_specs=[
            half, pl.BlockSpec((tt, d - aw), lambda i: (i, 0)), row,
            pl.BlockSpec((d, d), lambda i: (0, 0)),
            pl.BlockSpec((1, d), lambda i: (0, 0)),
            pl.BlockSpec((d, n_exp), lambda i: (0, 0)),
            pl.BlockSpec((1, n_exp), lambda i: (0, 0)),
        ],
        out_specs=[row, row, small, small],
        out_shape=[
            jax.ShapeDtypeStruct((n, d), F32),
            jax.ShapeDtypeStruct((n, d), F32),
            jax.ShapeDtypeStruct((n, LANES), I32),
            jax.ShapeDtypeStruct((n, LANES), F32),
        ],
        compiler_params=_params(("arbitrary",)),
        name="wo_router",
    )(o, c, x2d, w_o_bf, norm_w, w_router, b_router)


def _route_plan(idx, n_exp, tm, ts):
    n_tok, top_k = idx.shape
    n_asg = n_tok * top_k
    nsub = tm // ts
    flat_e = idx.reshape(-1)
    onehot = (flat_e[:, None] == jnp.arange(n_exp, dtype=I32)[None, :]).astype(I32)
    csum = jnp.cumsum(onehot, axis=0)
    rank = jnp.sum((csum - onehot) * onehot, axis=1)
    counts = csum[-1]
    ntile = (counts + tm - 1) // tm
    first_rows = counts - jnp.maximum(ntile - 1, 0) * tm
    tile_end = jnp.cumsum(ntile)
    tile_beg = tile_end - ntile
    fr = first_rows[flat_e]
    dest = (tile_beg[flat_e] * tm + jnp.where(rank < fr, rank, rank - fr + tm)).astype(I32)

    n_tiles = -(-n_asg // tm) + n_exp
    t_ids = jnp.arange(n_tiles, dtype=I32)
    tile_exp = jnp.minimum(jnp.sum(t_ids[:, None] >= tile_end[None, :], axis=1), n_exp - 1).astype(I32)
    n_used = tile_end[-1]
    first_sub = (first_rows[tile_exp] + ts - 1) // ts
    tile_nsub = jnp.where(t_ids < n_used, jnp.where(t_ids == tile_beg[tile_exp], first_sub, nsub), 0)
    tile_nsub = tile_nsub.astype(I32)

    n_cs = -(-n_asg // ts) + n_exp
    sub_end = jnp.cumsum(tile_nsub)
    n_csub = sub_end[-1]
    i_ids = jnp.minimum(jnp.arange(n_cs, dtype=I32), n_csub - 1)
    cs_tile = jnp.minimum(jnp.sum(i_ids[:, None] >= sub_end[None, :], axis=1), n_tiles - 1)
    cs_blk = (cs_tile * nsub + i_ids - (sub_end - tile_nsub)[cs_tile]).astype(I32)
    flat_tok = jnp.arange(n_asg, dtype=I32) // top_k
    buf_tok = jnp.zeros((n_tiles * tm,), I32).at[dest].set(flat_tok)
    tok_c = buf_tok.reshape(n_tiles * nsub, ts)[cs_blk].reshape(-1)
    return (dest, tile_exp, tile_nsub, n_used.astype(I32).reshape(1), tok_c, cs_blk,
            n_csub.astype(I32).reshape(1))


def _gather_body(tok_ref, blk_ref, ncs_ref, x_hbm, xs_ref, land_ref, sem, *, ts):
    del blk_ref
    i = pl.program_id(0)
    n = ncs_ref[0]

    def row_copy(step, slot, r):
        tok = tok_ref[step * ts + r]
        return pltpu.make_async_copy(x_hbm.at[pl.ds(tok, 1)], land_ref.at[slot, pl.ds(r, 1)],
                                     sem.at[slot])

    def start_all(step, slot):
        def body(g, carry):
            for u in range(GATHER_UNROLL):
                row_copy(step, slot, g * GATHER_UNROLL + u).start(priority=u % 2)
            return carry
        lax.fori_loop(0, ts // GATHER_UNROLL, body, 0)

    def wait_all(step, slot):
        def body(g, carry):
            for u in range(GATHER_UNROLL):
                row_copy(step, slot, g * GATHER_UNROLL + u).wait()
            return carry
        lax.fori_loop(0, ts // GATHER_UNROLL, body, 0)

    @pl.when(i == 0)
    def _():
        start_all(0, 0)

    @pl.when(i + 1 < n)
    def _():
        start_all(i + 1, (i + 1) % 2)

    @pl.when(i < n)
    def _():
        wait_all(i, i % 2)
        xs_ref[...] = land_ref[i % 2].astype(xs_ref.dtype)


def _gather_rows(xf, tok_c, cs_blk, n_csub, *, n_rows, ts):
    d = xf.shape[1]
    n_cs = cs_blk.shape[0]
    grid_spec = pltpu.PrefetchScalarGridSpec(
        num_scalar_prefetch=3,
        grid=(n_cs,),
        in_specs=[pl.BlockSpec(memory_space=pl.ANY)],
        out_specs=pl.BlockSpec((ts, d), lambda i, tok, blk, ncs: (blk[i], 0)),
        scratch_shapes=[pltpu.VMEM((2, ts, d), xf.dtype), pltpu.SemaphoreType.DMA((2,))],
    )
    return pl.pallas_call(
        functools.partial(_gather_body, ts=ts),
        grid_spec=grid_spec,
        out_shape=jax.ShapeDtypeStruct((n_rows, d), BF16),
        compiler_params=_params(("arbitrary",)),
        name="moe_gather",
    )(tok_c, cs_blk, n_csub, xf)


def _expert_changed(texp_ref, t):
    return (t == 0) | (texp_ref[t] != texp_ref[jnp.maximum(t - 1, 0)])


def _moe_up_body(texp_ref, tns_ref, nused_ref, x_ref, wg_ref, wu_ref, bg_ref, bu_ref, a_ref, wbf_ref,
                 *, ts):
    t = pl.program_id(1)

    @pl.when(t < nused_ref[0])
    def _():
        @pl.when(_expert_changed(texp_ref, t))
        def _():
            wbf_ref[0] = wg_ref[0].astype(BF16)
            wbf_ref[1] = wu_ref[0].astype(BF16)

        def sub(j, carry):
            r0 = pl.multiple_of(j * ts, ts)
            x = x_ref[pl.ds(r0, ts), :]
            g = jnp.dot(x, wbf_ref[0], preferred_element_type=F32) + bg_ref[0]
            lin = jnp.dot(x, wbf_ref[1], preferred_element_type=F32) + bu_ref[0]
            g = jnp.minimum(g, SWIGLU_LIMIT)
            lin = jnp.clip(lin, -SWIGLU_LIMIT, SWIGLU_LIMIT)
            a = g * jax.nn.sigmoid(SWIGLU_ALPHA * g) * (lin + 1.0)
            a_ref[pl.ds(r0, ts), :] = a.astype(a_ref.dtype)
            return carry

        lax.fori_loop(0, tns_ref[t], sub, 0)


def _moe_down_body(texp_ref, tns_ref, nused_ref, a_ref, wd_ref, bd_ref, y_ref, wbf_ref, *, ts):
    t = pl.program_id(1)

    @pl.when(t < nused_ref[0])
    def _():
        @pl.when(_expert_changed(texp_ref, t))
        def _():
            wbf_ref[...] = wd_ref[0].astype(BF16)

        def sub(j, carry):
            r0 = pl.multiple_of(j * ts, ts)
            y_ref[pl.ds(r0, ts), :] = jnp.dot(a_ref[pl.ds(r0, ts), :], wbf_ref[...],
                                              preferred_element_type=F32) + bd_ref[0]
            return carry

        lax.fori_loop(0, tns_ref[t], sub, 0)


def _moe_experts(xs, tile_exp, tile_nsub, n_used, w_gate_up, b_gate_up, w_down, b_down, *, tm, ts, tf, tn):
    n_rows, d = xs.shape
    n_exp, _, two_ff = w_gate_up.shape
    d_ff = two_ff // 2
    n_f = d_ff // tf
    n_t = n_rows // tm
    bgu = b_gate_up.reshape(n_exp, 1, two_ff)
    bdn = b_down.reshape(n_exp, 1, d)

    def tile(t, nu):
        return jnp.minimum(t, nu[0] - 1)

    up_spec = pltpu.PrefetchScalarGridSpec(
        num_scalar_prefetch=3,
        grid=(n_f, n_t),
        in_specs=[
            pl.BlockSpec((tm, d), lambda f, t, te, ns, nu: (tile(t, nu), 0)),
            pl.BlockSpec((1, d, tf), lambda f, t, te, ns, nu: (te[tile(t, nu)], 0, f)),
            pl.BlockSpec((1, d, tf), lambda f, t, te, ns, nu: (te[tile(t, nu)], 0, n_f + f)),
            pl.BlockSpec((1, 1, tf), lambda f, t, te, ns, nu: (te[tile(t, nu)], 0, f)),
            pl.BlockSpec((1, 1, tf), lambda f, t, te, ns, nu: (te[tile(t, nu)], 0, n_f + f)),
        ],
        out_specs=pl.BlockSpec((tm, tf), lambda f, t, te, ns, nu: (tile(t, nu), f)),
        scratch_shapes=[pltpu.VMEM((2, d, tf), BF16)],
    )
    act = pl.pallas_call(
        functools.partial(_moe_up_body, ts=ts),
        grid_spec=up_spec,
        out_shape=jax.ShapeDtypeStruct((n_rows, d_ff), BF16),
        compiler_params=_params(("arbitrary", "arbitrary")),
        name="moe_up",
    )(tile_exp, tile_nsub, n_used, xs, w_gate_up, w_gate_up, bgu, bgu)

    n_n = d // tn
    down_spec = pltpu.PrefetchScalarGridSpec(
        num_scalar_prefetch=3,
        grid=(n_n, n_t),
        in_specs=[
            pl.BlockSpec((tm, d_ff), lambda n, t, te, ns, nu: (tile(t, nu), 0)),
            pl.BlockSpec((1, d_ff, tn), lambda n, t, te, ns, nu: (te[tile(t, nu)], 0, n)),
            pl.BlockSpec((1, 1, tn), lambda n, t, te, ns, nu: (te[tile(t, nu)], 0, n)),
        ],
        out_specs=pl.BlockSpec((tm, tn), lambda n, t, te, ns, nu: (tile(t, nu), n)),
        scratch_shapes=[pltpu.VMEM((d_ff, tn), BF16)],
    )
    return pl.pallas_call(
        functools.partial(_moe_down_body, ts=ts),
        grid_spec=down_spec,
        out_shape=jax.ShapeDtypeStruct((n_rows, d), F32),
        compiler_params=_params(("arbitrary", "arbitrary")),
        name="moe_down",
    )(tile_exp, tile_nsub, n_used, act, w_down, bdn)


def _combine_body(dest_ref, x1_ref, gate_ref, nw_ref, yb_hbm, out_ref, rows_ref, sem, *, tt, top_k):
    i = pl.program_id(0)

    def row_copy(r, k):
        d = dest_ref[(i * tt + r) * top_k + k]
        return pltpu.make_async_copy(yb_hbm.at[pl.ds(d, 1)], rows_ref.at[k, pl.ds(r, 1)], sem)

    rows_per_iter = GATHER_UNROLL // top_k

    def issue(g, carry):
        for u in range(rows_per_iter):
            for k in range(top_k):
                row_copy(g * rows_per_iter + u, k).start(priority=k % 2)
        return carry

    def drain(g, carry):
        for u in range(rows_per_iter):
            for k in range(top_k):
                row_copy(g * rows_per_iter + u, k).wait()
        return carry

    lax.fori_loop(0, tt // rows_per_iter, issue, 0)
    lax.fori_loop(0, tt // rows_per_iter, drain, 0)
    gate = gate_ref[...]
    moe = rows_ref[0] * gate[:, 0:1]
    for k in range(1, top_k):
        moe = moe + rows_ref[k] * gate[:, k:k + 1]
    out_ref[...] = _rms(x1_ref[...] + moe, nw_ref[...])


def _combine(dest, x1, gate, norm_w, yb, *, tt):
    n, d = x1.shape
    grid_spec = pltpu.PrefetchScalarGridSpec(
        num_scalar_prefetch=1,
        grid=(n // tt,),
        in_specs=[
            pl.BlockSpec((tt, d), lambda i, ds: (i, 0)),
            pl.BlockSpec((tt, LANES), lambda i, ds: (i, 0)),
            pl.BlockSpec((1, d), lambda i, ds: (0, 0)),
            pl.BlockSpec(memory_space=pl.ANY),
        ],
        out_specs=pl.BlockSpec((tt, d), lambda i, ds: (i, 0)),
        scratch_shapes=[pltpu.VMEM((TOP_K, tt, d), F32), pltpu.SemaphoreType.DMA],
    )
    return pl.pallas_call(
        functools.partial(_combine_body, tt=tt, top_k=TOP_K),
        grid_spec=grid_spec,
        out_shape=jax.ShapeDtypeStruct((n, d), F32),
        compiler_params=_params(("arbitrary",)),
        name="moe_combine",
    )(dest, x1, gate, norm_w, yb)


def _tile(n, pref):
    t = min(n, pref)
    while n % t:
        t //= 2
    return t


def kernel(x_prompt, x_sample, cache_k, cache_v, state_conv, page_table, norm_mix_w, w_in,
           lambda_q1, lambda_k1, lambda_q2, lambda_k2, subln_w, conv_dw_w, conv_dw_b,
           conv_norm_w, conv_norm_b, w_o, norm_ffn_w, w_router, b_router, w_gate_up, b_gate_up,
           w_down, b_down, norm_final_w):
    batch, seq, d = x_prompt.shape
    dec, n_new, _ = x_sample.shape
    depth, n_pool, page, n_maps, d_qk = cache_k.shape
    n_heads, d_v = cache_v.shape[3:]
    aw = n_heads * d_v
    ch = d - aw
    n_pages = page_table.shape[1]
    past = n_pages * page
    rope_dim = d_qk // 4
    half = rope_dim // 2
    scale = d_qk ** -0.5
    n_exp = w_router.shape[2]
    n_p, n_s = batch * seq, dec * n_new

    h_p = x_prompt.reshape(n_p, d)
    h_s = x_sample.reshape(n_s, d)
    tab_p = _rope_tables(jnp.arange(seq, dtype=I32), d_qk, rope_dim)
    tab_s = _rope_tables(past + jnp.arange(n_s, dtype=I32) % n_new, d_qk, rope_dim)
    outs = {k: [] for k in ("kp", "vp", "cp", "ks", "vs", "cs")}

    for l in range(depth):
        lam_init = 0.8 - 0.6 * math.exp(-0.3 * l)
        lams = tuple(v[l].reshape(1, d_qk) for v in (lambda_q1, lambda_k1, lambda_q2, lambda_k2))
        sw = subln_w[l].reshape(1, d_v)
        w_in_bf = w_in[l].astype(BF16)
        w_o_bf = w_o[l].astype(BF16)
        nmw = norm_mix_w[l].reshape(1, d)
        dw_b = conv_dw_b[l].reshape(1, ch)
        ln_w = conv_norm_w[l].reshape(1, ch)
        ln_b = conv_norm_b[l].reshape(1, ch)

        tt_p = _tile(seq, 512)
        q_p, k_p, v_p, u_p = _proj(h_p, nmw, w_in_bf, tab_p, scale=scale, half=half, tt=tt_p)
        tt_s = _tile(n_s, 512)
        q_s, k_s, v_s, u_s = _proj(h_s, nmw, w_in_bf, tab_s, scale=scale, half=half, tt=tt_s)

        o_p = _attn_prompt(q_p, k_p, v_p, lams, sw, batch=batch, seq=seq, n_heads=n_heads,
                           d_qk=d_qk, lam_init=lam_init, tq=_tile(seq, 512))
        q_rows = q_s.reshape(dec, n_new, n_maps, d_qk).transpose(0, 2, 1, 3)
        q_rows = jnp.pad(q_rows, ((0, 0), (0, 0), (0, QR - n_new), (0, 0))).reshape(dec, n_maps * QR, d_qk)
        ck = cache_k.transpose(0, 1, 3, 4, 2).reshape(depth * n_pool, n_maps * d_qk, page)
        cv = cache_v.reshape(depth * n_pool, page * n_heads, d_v)
        o_s = _attn_sample(page_table + l * n_pool, q_rows, k_s.reshape(dec, n_new * n_maps, d_qk),
                           v_s.reshape(dec, n_new * n_heads, d_v), ck, cv, lams, sw, n_new=n_new,
                           n_maps=n_maps, n_heads=n_heads, qr=QR, lam_init=lam_init,
                           n_chunk=_tile(n_pages, 8))
        o_s = o_s.reshape(n_s, aw).astype(BF16)

        c_p = _conv_prompt(u_p, conv_dw_w[l], dw_b, ln_w, ln_b, batch=batch, seq=seq,
                           tt=_tile(seq, 256))
        u_s3 = u_s.reshape(dec, n_new, ch)
        u_pad_s = jnp.concatenate([state_conv[l], u_s3], axis=1)
        c_s = _conv_sample(u_pad_s.transpose(1, 0, 2), conv_dw_w[l], dw_b, ln_w, ln_b,
                           n_new=n_new, tb=_tile(dec, 16))
        c_s = c_s.transpose(1, 0, 2).reshape(n_s, ch)

        nfw = norm_ffn_w[l].reshape(1, d)
        br = b_router[l].reshape(1, n_exp)
        x1_p, xf_p, idx_p, gate_p = _wo_router(o_p, c_p, h_p, w_o_bf, nfw, w_router[l], br, tt=tt_p)
        x1_s, xf_s, idx_s, gate_s = _wo_router(o_s, c_s, h_s, w_o_bf, nfw, w_router[l], br, tt=tt_s)

        idx_all = jnp.concatenate([idx_p[:, :TOP_K], idx_s[:, :TOP_K]], axis=0)
        dest, tile_exp, tile_nsub, n_used, tok_c, cs_blk, n_csub = _route_plan(
            idx_all, n_exp, MOE_TILE, MOE_SUBTILE)
        xf_all = jnp.concatenate([xf_p, xf_s], axis=0)
        xs = _gather_rows(xf_all, tok_c, cs_blk, n_csub, n_rows=tile_exp.shape[0] * MOE_TILE,
                          ts=MOE_SUBTILE)
        yb = _moe_experts(xs, tile_exp, tile_nsub, n_used, w_gate_up[l], b_gate_up[l], w_down[l],
                          b_down[l], tm=MOE_TILE, ts=MOE_SUBTILE, tf=_tile(w_down.shape[2], 1024),
                          tn=_tile(d, 1024))

        last = l == depth - 1
        nw_out = norm_final_w.reshape(1, d) if last else jnp.ones((1, d), F32)
        assert last, "the combine kernel fuses the final norm; deeper stacks need an un-normed variant"
        h_p = _combine(dest[:n_p * TOP_K], x1_p, gate_p, nw_out, yb, tt=_tile(n_p, 256))
        h_s = _combine(dest[n_p * TOP_K:], x1_s, gate_s, nw_out, yb, tt=_tile(n_s, 256))

        outs["kp"].append(k_p.reshape(batch, seq, n_maps, d_qk))
        outs["vp"].append(v_p.reshape(batch, seq, n_heads, d_v))
        outs["cp"].append(u_p.reshape(batch, seq, ch)[:, seq - (conv_dw_w.shape[1] - 1):])
        outs["ks"].append(k_s.reshape(dec, n_new, n_maps, d_qk))
        outs["vs"].append(v_s.reshape(dec, n_new, n_heads, d_v))
        outs["cs"].append(u_pad_s[:, n_new:])

    return (h_p.reshape(batch, seq, d), h_s.reshape(dec, n_new, d),
            jnp.stack(outs["kp"]), jnp.stack(outs["vp"]), jnp.stack(outs["cp"]),
            jnp.stack(outs["ks"]), jnp.stack(outs["vs"]), jnp.stack(outs["cs"]))
```

```python
import functools
import math

import jax
import jax.numpy as jnp
from jax import lax
from jax.experimental import pallas as pl
from jax.experimental.pallas import tpu as pltpu

F32 = jnp.float32
BF16 = jnp.bfloat16
I32 = jnp.int32

EPS = 1e-5
ROPE_THETA = 500000.0
TOP_K = 4
SWIGLU_LIMIT = 7.0
SWIGLU_ALPHA = 1.702
LANES = 128
VMEM_LIMIT = 56 * 1024 * 1024
QR = 8
MOE_TILE = 768
MOE_SUBTILE = 256
GATHER_UNROLL = 8


def _params(sem, vmem=VMEM_LIMIT):
    return pltpu.CompilerParams(dimension_semantics=sem, vmem_limit_bytes=vmem)


def _nt_dot(a, b):
    return lax.dot_general(a, b, (((1,), (1,)), ((), ())), preferred_element_type=F32)


def _rms(x, w):
    ms = jnp.mean(x * x, axis=-1, keepdims=True)
    return x * lax.rsqrt(ms + EPS) * w


def _lambda(lq1, lk1, lq2, lk2, lam_init):
    a = jnp.sum(lq1 * lk1, axis=-1, keepdims=True)
    b = jnp.sum(lq2 * lk2, axis=-1, keepdims=True)
    return jnp.exp(a) - jnp.exp(b) + lam_init


def _proj_body(x_ref, nw_ref, wa_ref, wb_ref, cs_ref, s1_ref, s2_ref,
               q_ref, k_ref, v_ref, u_ref, xn_ref, *, scale, half):
    j = pl.program_id(1)

    @pl.when(j == 0)
    def _():
        xn_ref[...] = _rms(x_ref[...], nw_ref[...]).astype(BF16)

    def rope_store(dst_ref, p, mul):
        cs, s1, s2 = cs_ref[...], s1_ref[...], s2_ref[...]
        for c in range(p.shape[1] // LANES):
            seg = p[:, c * LANES:(c + 1) * LANES]
            r = (seg * cs + pltpu.roll(seg, LANES - half, 1) * s1
                 + pltpu.roll(seg, half, 1) * s2)
            if mul is not None:
                r = r * mul
            dst_ref[:, c * LANES:(c + 1) * LANES] = r.astype(dst_ref.dtype)

    @pl.when(j == 0)
    def _():
        p = jnp.dot(xn_ref[...], wa_ref[...], preferred_element_type=F32)
        rope_store(q_ref, p, scale)

    @pl.when(j == 1)
    def _():
        p = jnp.dot(xn_ref[...], wa_ref[...], preferred_element_type=F32)
        rope_store(k_ref, p, None)

    @pl.when(j == 2)
    def _():
        v_ref[...] = jnp.dot(xn_ref[...], wa_ref[...], preferred_element_type=F32)

    @pl.when(j == 3)
    def _():
        xn = xn_ref[...]
        val = jnp.dot(xn, wa_ref[...], preferred_element_type=F32)
        gate = jnp.dot(xn, wb_ref[...], preferred_element_type=F32)
        u_ref[...] = val * jax.nn.sigmoid(gate)


def _proj(x2d, norm_w, w_in_bf, tables, *, scale, half, tt):
    n, d = x2d.shape
    w = w_in_bf.shape[1] // 5
    cs, s1, s2 = tables
    n_pos_blocks = cs.shape[0] // tt
    tab_spec = pl.BlockSpec((tt, LANES), lambda i, j: (i % n_pos_blocks, 0))
    out_spec = pl.BlockSpec((tt, w), lambda i, j: (i, 0))
    return pl.pallas_call(
        functools.partial(_proj_body, scale=scale, half=half),
        grid=(n // tt, 4),
        in_specs=[
            pl.BlockSpec((tt, d), lambda i, j: (i, 0)),
            pl.BlockSpec((1, d), lambda i, j: (0, 0)),
            pl.BlockSpec((d, w), lambda i, j: (0, j)),
            pl.BlockSpec((d, w), lambda i, j: (0, 4)),
            tab_spec, tab_spec, tab_spec,
        ],
        out_specs=[out_spec, out_spec, out_spec, out_spec],
        out_shape=[
            jax.ShapeDtypeStruct((n, w), BF16),
            jax.ShapeDtypeStruct((n, w), F32),
            jax.ShapeDtypeStruct((n, w), F32),
            jax.ShapeDtypeStruct((n, w), F32),
        ],
        scratch_shapes=[pltpu.VMEM((tt, d), BF16)],
        compiler_params=_params(("arbitrary", "arbitrary")),
        name="proj",
    )(x2d, norm_w, w_in_bf, w_in_bf, cs, s1, s2)


def _rope_tables(pos, d_qk, rope_dim):
    half = rope_dim // 2
    inv = ROPE_THETA ** (-jnp.arange(0, rope_dim, 2, dtype=F32) / rope_dim)
    ang = pos.astype(F32)[:, None] * inv[None, :]
    cos, sin = jnp.cos(ang), jnp.sin(ang)
    p = pos.shape[0]
    zh = jnp.zeros((p, half), F32)
    zr = jnp.zeros((p, d_qk - rope_dim), F32)
    cs = jnp.concatenate([cos, cos, jnp.ones((p, d_qk - rope_dim), F32)], axis=-1)
    s1 = jnp.concatenate([-sin, zh, zr], axis=-1)
    s2 = jnp.concatenate([zh, sin, zr], axis=-1)
    rep = LANES // d_qk
    return tuple(jnp.tile(t, (1, rep)) for t in (cs, s1, s2))


def _attn_prompt_body(q_ref, k_ref, v_ref, lq1_ref, lk1_ref, lq2_ref, lk2_ref, sw_ref,
                      o_ref, kb_ref, vb_ref, m_ref, l_ref, acc_ref, *, tq, tk, d_qk, lam_init):
    qi = pl.program_id(2)

    @pl.when(qi == 0)
    def _():
        kb_ref[...] = k_ref[...].astype(BF16)
        vb_ref[...] = v_ref[...].astype(BF16)

    q = q_ref[...]
    lane = lax.broadcasted_iota(I32, q.shape, 1)
    zero = jnp.zeros_like(q)
    qm = (jnp.where(lane < d_qk, q, zero), jnp.where(lane >= d_qk, q, zero))
    m_ref[...] = jnp.full(m_ref.shape, -jnp.inf, F32)
    l_ref[...] = jnp.zeros(l_ref.shape, F32)
    acc_ref[...] = jnp.zeros(acc_ref.shape, F32)

    def step(c, masked):
        k0 = pl.multiple_of(c * tk, tk)
        kc = kb_ref[pl.ds(k0, tk), :]
        vc = vb_ref[pl.ds(k0, tk), :]
        for mi in range(2):
            s = _nt_dot(qm[mi], kc)
            if masked:
                row = lax.broadcasted_iota(I32, (tq, tk), 0)
                col = lax.broadcasted_iota(I32, (tq, tk), 1)
                s = jnp.where(col + c * tk <= row + qi * tq, s, -jnp.inf)
            m_p = m_ref[mi]
            m_n = jnp.maximum(m_p, jnp.max(s, axis=1, keepdims=True))
            p = jnp.exp(s - jnp.concatenate([m_n] * (tk // LANES), axis=1))
            al = jnp.exp(m_p - m_n)
            l_ref[mi] = al * l_ref[mi] + jnp.sum(p, axis=1, keepdims=True)
            acc_ref[mi] = al * acc_ref[mi] + jnp.dot(p.astype(BF16), vc, preferred_element_type=F32)
            m_ref[mi] = m_n

    def full_step(c, carry):
        step(c, False)
        return carry

    n_full = qi * (tq // tk)
    lax.fori_loop(0, n_full, full_step, 0)
    for dgl in range(tq // tk):
        step(n_full + dgl, True)

    lam = _lambda(lq1_ref[...], lk1_ref[...], lq2_ref[...], lk2_ref[...], lam_init)
    o = acc_ref[0] / l_ref[0] - lam * (acc_ref[1] / l_ref[1])
    o_ref[...] = (_rms(o, sw_ref[...]) * (1.0 - lam_init)).astype(o_ref.dtype)


def _attn_prompt(q, k, v, lams, subln_w, *, batch, seq, n_heads, d_qk, lam_init, tq):
    n, w = q.shape
    d_v = w // n_heads
    nq = seq // tq
    vec = pl.BlockSpec((1, d_qk), lambda b, h, i: (0, 0))
    return pl.pallas_call(
        functools.partial(_attn_prompt_body, tq=tq, tk=tq, d_qk=d_qk, lam_init=lam_init),
        grid=(batch, n_heads, nq),
        in_specs=[
            pl.BlockSpec((tq, d_v), lambda b, h, i: (b * nq + i, h)),
            pl.BlockSpec((seq, d_v), lambda b, h, i: (b, h)),
            pl.BlockSpec((seq, d_v), lambda b, h, i: (b, h)),
            vec, vec, vec, vec,
            pl.BlockSpec((1, d_v), lambda b, h, i: (0, 0)),
        ],
        out_specs=pl.BlockSpec((tq, d_v), lambda b, h, i: (b * nq + i, h)),
        out_shape=jax.ShapeDtypeStruct((n, w), BF16),
        scratch_shapes=[pltpu.VMEM((seq, d_v), BF16), pltpu.VMEM((seq, d_v), BF16),
                        pltpu.VMEM((2, tq, LANES), F32), pltpu.VMEM((2, tq, LANES), F32),
                        pltpu.VMEM((2, tq, d_v), F32)],
        compiler_params=_params(("arbitrary", "arbitrary", "arbitrary")),
        name="attn_prompt",
    )(q, k, v, *lams, subln_w)


def _attn_sample_body(pt_ref, q_ref, kn_ref, vn_ref, lq1_ref, lk1_ref, lq2_ref, lk2_ref, sw_ref,
                      *rest, n_chunk, n_new, n_maps, n_heads, qr, lam_init):
    del pt_ref
    k_refs = rest[:n_chunk]
    v_refs = rest[n_chunk:2 * n_chunk]
    o_ref = rest[2 * n_chunk]
    m_ref, l_ref, acc_ref, qbd_ref = rest[2 * n_chunk + 1:]
    c = pl.program_id(1)
    page = k_refs[0].shape[2]
    d_qk = k_refs[0].shape[1] // n_maps
    hr = 2 * qr

    @pl.when(c == 0)
    def _():
        m_ref[...] = jnp.full(m_ref.shape, -jnp.inf, F32)
        l_ref[...] = jnp.zeros(l_ref.shape, F32)
        acc_ref[...] = jnp.zeros(acc_ref.shape, F32)

    q_all = q_ref[0]

    @pl.when(c == 0)
    def _():
        q_rep = jnp.concatenate([q_all] * n_maps, axis=1)
        row_map = lax.broadcasted_iota(I32, q_rep.shape, 0) // qr
        col_map = lax.broadcasted_iota(I32, q_rep.shape, 1) // d_qk
        qbd_ref[...] = jnp.where(row_map == col_map, q_rep, jnp.zeros_like(q_rep))

    m_p, l_p, acc = m_ref[...], l_ref[...], acc_ref[...]
    k_cat = jnp.concatenate([k_refs[i][0].astype(BF16) for i in range(n_chunk)], axis=1)
    s = jnp.dot(qbd_ref[...], k_cat, preferred_element_type=F32)
    m_n = jnp.maximum(m_p, jnp.max(s, axis=1, keepdims=True))
    p = jnp.exp(s - m_n)
    al = jnp.exp(m_p - m_n)
    l_p = al * l_p + jnp.sum(p, axis=1, keepdims=True)
    pb = p.astype(BF16)
    pv = []
    for h in range(n_heads):
        vh = jnp.concatenate(
            [v_refs[i][0, pl.ds(h, page, stride=n_heads), :].astype(BF16) for i in range(n_chunk)],
            axis=0)
        pv.append(jnp.dot(pb[h * hr:(h + 1) * hr], vh, preferred_element_type=F32))
    acc = al * acc + jnp.concatenate(pv, axis=0)
    m_p = m_n
    m_ref[...] = m_p
    l_ref[...] = l_p
    acc_ref[...] = acc

    @pl.when(c == pl.num_programs(1) - 1)
    def _():
        qf = q_all.astype(F32)
        kn, vn = kn_ref[0], vn_ref[0]
        q_tok = lax.broadcasted_iota(I32, (qf.shape[0], 1), 0) % qr
        s_new = []
        for j in range(n_new):
            parts = []
            for m in range(n_maps):
                krow = kn[j * n_maps + m:j * n_maps + m + 1, :]
                parts.append(jnp.sum(qf[m * qr:(m + 1) * qr] * krow, axis=1, keepdims=True))
            s_new.append(jnp.where(q_tok >= j, jnp.concatenate(parts, axis=0), -jnp.inf))
        m_n = m_p
        for sj in s_new:
            m_n = jnp.maximum(m_n, sj)
        al = jnp.exp(m_p - m_n)
        l_n = al * l_p
        a_n = al * acc
        for j in range(n_new):
            pj = jnp.exp(s_new[j] - m_n)
            l_n = l_n + pj
            a_n = a_n + jnp.concatenate(
                [pj[h * hr:(h + 1) * hr] * vn[j * n_heads + h:j * n_heads + h + 1, :]
                 for h in range(n_heads)], axis=0)
        a_n = a_n / l_n
        lam = _lambda(lq1_ref[...], lk1_ref[...], lq2_ref[...], lk2_ref[...], lam_init)
        d_v = a_n.shape[1]
        for h in range(n_heads):
            o = a_n[h * hr:h * hr + n_new] - lam * a_n[h * hr + qr:h * hr + qr + n_new]
            o_ref[0, :, h * d_v:(h + 1) * d_v] = (
                _rms(o, sw_ref[...]) * (1.0 - lam_init)).astype(o_ref.dtype)


def _attn_sample(pages, q_rows, k_new, v_new, ck, cv, lams, subln_w, *, n_new, n_maps, n_heads, qr,
                 lam_init, n_chunk):
    dec, n_pages = pages.shape
    d_qk = ck.shape[1] // n_maps
    d_v = cv.shape[2]
    rows = n_maps * qr
    pt_flat = pages.reshape(-1)

    def page_map(i, b, c, pt):
        return (pt[b * n_pages + c * n_chunk + i], 0, 0)

    vec = pl.BlockSpec((1, d_qk), lambda b, c, pt: (0, 0))
    k_specs = [pl.BlockSpec((1,) + ck.shape[1:], functools.partial(page_map, i)) for i in range(n_chunk)]
    v_specs = [pl.BlockSpec((1,) + cv.shape[1:], functools.partial(page_map, i)) for i in range(n_chunk)]
    grid_spec = pltpu.PrefetchScalarGridSpec(
        num_scalar_prefetch=1,
        grid=(dec, n_pages // n_chunk),
        in_specs=[
            pl.BlockSpec((1, rows, d_qk), lambda b, c, pt: (b, 0, 0)),
            pl.BlockSpec((1,) + k_new.shape[1:], lambda b, c, pt: (b, 0, 0)),
            pl.BlockSpec((1,) + v_new.shape[1:], lambda b, c, pt: (b, 0, 0)),
            vec, vec, vec, vec,
            pl.BlockSpec((1, d_v), lambda b, c, pt: (0, 0)),
        ] + k_specs + v_specs,
        out_specs=pl.BlockSpec((1, n_new, n_heads * d_v), lambda b, c, pt: (b, 0, 0)),
        scratch_shapes=[pltpu.VMEM((rows, 1), F32), pltpu.VMEM((rows, 1), F32),
                        pltpu.VMEM((rows, d_v), F32), pltpu.VMEM((rows, n_maps * d_qk), BF16)],
    )
    return pl.pallas_call(
        functools.partial(_attn_sample_body, n_chunk=n_chunk, n_new=n_new, n_maps=n_maps,
                          n_heads=n_heads, qr=qr, lam_init=lam_init),
        grid_spec=grid_spec,
        out_shape=jax.ShapeDtypeStruct((dec, n_new, n_heads * d_v), F32),
        compiler_params=_params(("arbitrary", "arbitrary")),
        name="attn_sample",
    )(pt_flat, q_rows, k_new, v_new, *lams, subln_w, *([ck] * n_chunk), *([cv] * n_chunk))


def _ln_swish(y, lw, lb):
    mu = jnp.mean(y, axis=-1, keepdims=True)
    yc = y - mu
    var = jnp.mean(yc * yc, axis=-1, keepdims=True)
    z = yc * lax.rsqrt(var + EPS) * lw + lb
    return z * jax.nn.sigmoid(z)


def _conv_prompt_body(u_ref, up_ref, dw_ref, db_ref, lw_ref, lb_ref, c_ref, buf_ref, y_ref,
                      *, tt, width, halo, rb):
    t = pl.program_id(1)
    buf_ref[0:halo, :] = jnp.where(t == 0, 0.0, up_ref[...])
    buf_ref[halo:halo + tt, :] = u_ref[...]
    off = halo - (width - 1)

    def col_body(cc, carry):
        c0 = pl.multiple_of(cc * LANES, LANES)
        wts = dw_ref[:, pl.ds(c0, LANES)]
        bias = db_ref[:, pl.ds(c0, LANES)]
        for r in range(tt // rb):
            acc = jnp.zeros((rb, LANES), F32)
            for w in range(width):
                lo = r * rb + off + w
                acc = acc + buf_ref[lo:lo + rb, pl.ds(c0, LANES)] * wts[w:w + 1, :]
            y_ref[r * rb:(r + 1) * rb, pl.ds(c0, LANES)] = acc + bias
        return carry

    lax.fori_loop(0, u_ref.shape[1] // LANES, col_body, 0)
    c_ref[...] = _ln_swish(y_ref[...], lw_ref[...], lb_ref[...]).astype(c_ref.dtype)


def _conv_prompt(u, dw_w, dw_b, ln_w, ln_b, *, batch, seq, tt):
    n, ch = u.shape
    width = dw_w.shape[0]
    halo = 32
    nt = seq // tt
    per = tt // halo
    row = pl.BlockSpec((1, ch), lambda b, t: (0, 0))
    return pl.pallas_call(
        functools.partial(_conv_prompt_body, tt=tt, width=width, halo=halo, rb=64),
        grid=(batch, nt),
        in_specs=[
            pl.BlockSpec((tt, ch), lambda b, t: (b * nt + t, 0)),
            pl.BlockSpec((halo, ch), lambda b, t: (jnp.maximum((b * nt + t) * per - 1, 0), 0)),
            pl.BlockSpec((width, ch), lambda b, t: (0, 0)),
            row, row, row,
        ],
        out_specs=pl.BlockSpec((tt, ch), lambda b, t: (b * nt + t, 0)),
        out_shape=jax.ShapeDtypeStruct((n, ch), BF16),
        scratch_shapes=[pltpu.VMEM((halo + tt, ch), F32), pltpu.VMEM((tt, ch), F32)],
        compiler_params=_params(("arbitrary", "arbitrary")),
        name="conv_prompt",
    )(u, u, dw_w, dw_b, ln_w, ln_b)


def _conv_sample_body(up_ref, dw_ref, db_ref, lw_ref, lb_ref, c_ref, y_ref, *, width, n_new):
    length, tb, ch = up_ref.shape

    def col_body(cc, carry):
        c0 = pl.multiple_of(cc * LANES, LANES)
        wts = dw_ref[:, pl.ds(c0, LANES)]
        bias = db_ref[:, pl.ds(c0, LANES)]
        accs = [jnp.zeros((tb, LANES), F32) for _ in range(n_new)]
        for j in range(length):
            x = up_ref[j, :, pl.ds(c0, LANES)]
            for t in range(n_new):
                w = j - t
                if 0 <= w < width:
                    accs[t] = accs[t] + x * wts[w:w + 1, :]
        for t in range(n_new):
            y_ref[t, :, pl.ds(c0, LANES)] = accs[t] + bias
        return carry

    lax.fori_loop(0, ch // LANES, col_body, 0)
    c_ref[...] = _ln_swish(y_ref[...], lw_ref[...], lb_ref[...]).astype(c_ref.dtype)


def _conv_sample(u_pad_t, dw_w, dw_b, ln_w, ln_b, *, n_new, tb):
    length, dec, ch = u_pad_t.shape
    width = dw_w.shape[0]
    row = pl.BlockSpec((1, ch), lambda b: (0, 0))
    return pl.pallas_call(
        functools.partial(_conv_sample_body, width=width, n_new=n_new),
        grid=(dec // tb,),
        in_specs=[
            pl.BlockSpec((length, tb, ch), lambda b: (0, b, 0)),
            pl.BlockSpec((width, ch), lambda b: (0, 0)),
            row, row, row,
        ],
        out_specs=pl.BlockSpec((n_new, tb, ch), lambda b: (0, b, 0)),
        out_shape=jax.ShapeDtypeStruct((n_new, dec, ch), BF16),
        scratch_shapes=[pltpu.VMEM((n_new, tb, ch), F32)],
        compiler_params=_params(("arbitrary",)),
        name="conv_sample",
    )(u_pad_t, dw_w, dw_b, ln_w, ln_b)


def _wo_body(o_ref, c_ref, x_ref, wo_ref, nw_ref, wr_ref, br_ref, *rest, aw, top_k):
    x1_ref, xf_ref, idx_ref, gate_ref = rest[-4:]
    attn = (jnp.dot(o_ref[...], wo_ref[0:aw, :], preferred_element_type=F32)
            + jnp.dot(c_ref[...], wo_ref[aw:, :], preferred_element_type=F32))
    x1 = x_ref[...] + attn
    x1_ref[...] = x1
    xf = _rms(x1, nw_ref[...])
    xf_ref[...] = xf
    xh = xf.astype(BF16)
    xl = (xf - xh.astype(F32)).astype(BF16)
    wr = wr_ref[...]
    wh = wr.astype(BF16)
    wl = (wr - wh.astype(F32)).astype(BF16)
    logits = (jnp.dot(xh, wh, preferred_element_type=F32) + jnp.dot(xl, wh, preferred_element_type=F32)
              + jnp.dot(xh, wl, preferred_element_type=F32)) + br_ref[...]
    n_exp = logits.shape[1]
    lane_e = lax.broadcasted_iota(I32, logits.shape, 1)
    lane_o = lax.broadcasted_iota(I32, idx_ref.shape, 1)
    vals = logits
    tops, idxs = [], []
    for _ in range(top_k):
        mx = jnp.max(vals, axis=1, keepdims=True)
        am = jnp.min(jnp.where(vals == mx, lane_e, n_exp), axis=1, keepdims=True)
        tops.append(mx)
        idxs.append(am)
        vals = jnp.where(lane_e == am, -jnp.inf, vals)
    es = [jnp.exp(t - tops[0]) for t in tops]
    den = es[0]
    for e in es[1:]:
        den = den + e
    idx_out = jnp.zeros(idx_ref.shape, I32)
    gate_out = jnp.zeros(gate_ref.shape, F32)
    for k in range(top_k):
        idx_out = jnp.where(lane_o == k, idxs[k], idx_out)
        gate_out = jnp.where(lane_o == k, es[k] / den, gate_out)
    idx_ref[...] = idx_out
    gate_ref[...] = gate_out


def _wo_router(o, c, x2d, w_o_bf, norm_w, w_router, b_router, *, tt, xf_rows, xf_row0=0, xf_buf=None):
    n, d = x2d.shape
    aw = o.shape[1]
    n_exp = w_router.shape[1]
    blk0 = xf_row0 // tt
    row = pl.BlockSpec((tt, d), lambda i: (i, 0))
    half = pl.BlockSpec((tt, aw), lambda i: (i, 0))
    small = pl.BlockSpec((tt, LANES), lambda i: (i, 0))
    in_specs = [
        half, pl.BlockSpec((tt, d - aw), lambda i: (i, 0)), row,
        pl.BlockSpec((d, d), lambda i: (0, 0)),
        pl.BlockSpec((1, d), lambda i: (0, 0)),
        pl.BlockSpec((d, n_exp), lambda i: (0, 0)),
        pl.BlockSpec((1, n_exp), lambda i: (0, 0)),
    ]
    args = [o, c, x2d, w_o_bf, norm_w, w_router, b_router]
    aliases = {}
    if xf_buf is not None:
        in_specs.append(pl.BlockSpec(memory_space=pl.ANY))
        args.append(xf_buf)
        aliases = {len(args) - 1: 1}
    return pl.pallas_call(
        functools.partial(_wo_body, aw=aw, top_k=TOP_K),
        grid=(n // tt,),
        in_specs=in_specs,
        out_specs=[row, pl.BlockSpec((tt, d), lambda i: (blk0 + i, 0)), small, small],
        out_shape=[
            jax.ShapeDtypeStruct((n, d), F32),
            jax.ShapeDtypeStruct((xf_rows, d), F32),
            jax.ShapeDtypeStruct((n, LANES), I32),
            jax.ShapeDtypeStruct((n, LANES), F32),
        ],
        input_output_aliases=aliases,
        compiler_params=_params(("arbitrary",)),
        name="wo_router",
    )(*args)


def _route_plan(idx, n_exp, tm, ts):
    n_tok, top_k = idx.shape
    n_asg = n_tok * top_k
    nsub = tm // ts
    flat_e = idx.reshape(-1)
    onehot = (flat_e[:, None] == jnp.arange(n_exp, dtype=I32)[None, :]).astype(I32)
    csum = jnp.cumsum(onehot, axis=0)
    rank = jnp.sum((csum - onehot) * onehot, axis=1)
    counts = csum[-1]
    ntile = (counts + tm - 1) // tm
    first_rows = counts - jnp.maximum(ntile - 1, 0) * tm
    tile_end = jnp.cumsum(ntile)
    tile_beg = tile_end - ntile
    fr = first_rows[flat_e]
    dest = (tile_beg[flat_e] * tm + jnp.where(rank < fr, rank, rank - fr + tm)).astype(I32)

    n_tiles = -(-n_asg // tm) + n_exp
    t_ids = jnp.arange(n_tiles, dtype=I32)
    tile_exp = jnp.minimum(jnp.sum(t_ids[:, None] >= tile_end[None, :], axis=1), n_exp - 1).astype(I32)
    n_used = tile_end[-1]
    first_sub = (first_rows[tile_exp] + ts - 1) // ts
    tile_nsub = jnp.where(t_ids < n_used, jnp.where(t_ids == tile_beg[tile_exp], first_sub, nsub), 0)
    tile_nsub = tile_nsub.astype(I32)

    n_cs = -(-n_asg // ts) + n_exp
    sub_end = jnp.cumsum(tile_nsub)
    n_csub = sub_end[-1]
    i_ids = jnp.minimum(jnp.arange(n_cs, dtype=I32), n_csub - 1)
    cs_tile = jnp.minimum(jnp.sum(i_ids[:, None] >= sub_end[None, :], axis=1), n_tiles - 1)
    cs_blk = (cs_tile * nsub + i_ids - (sub_end - tile_nsub)[cs_tile]).astype(I32)
    flat_tok = jnp.arange(n_asg, dtype=I32) // top_k
    buf_tok = jnp.zeros((n_tiles * tm,), I32).at[dest].set(flat_tok)
    tok_c = buf_tok.reshape(n_tiles * nsub, ts)[cs_blk].reshape(-1)
    return (dest, tile_exp, tile_nsub, n_used.astype(I32).reshape(1), tok_c, cs_blk,
            n_csub.astype(I32).reshape(1))


def _gather_body(tok_ref, blk_ref, ncs_ref, x_hbm, xs_ref, land_ref, sem, *, ts):
    del blk_ref
    i = pl.program_id(0)
    n = ncs_ref[0]

    def row_copy(step, slot, r):
        tok = tok_ref[step * ts + r]
        return pltpu.make_async_copy(x_hbm.at[pl.ds(tok, 1)], land_ref.at[slot, pl.ds(r, 1)],
                                     sem.at[slot])

    def start_all(step, slot):
        def body(g, carry):
            for u in range(GATHER_UNROLL):
                row_copy(step, slot, g * GATHER_UNROLL + u).start()
            return carry
        lax.fori_loop(0, ts // GATHER_UNROLL, body, 0)

    def wait_all(step, slot):
        def body(g, carry):
            for u in range(GATHER_UNROLL):
                row_copy(step, slot, g * GATHER_UNROLL + u).wait()
            return carry
        lax.fori_loop(0, ts // GATHER_UNROLL, body, 0)

    @pl.when(i == 0)
    def _():
        start_all(0, 0)

    @pl.when(i + 1 < n)
    def _():
        start_all(i + 1, (i + 1) % 2)

    @pl.when(i < n)
    def _():
        wait_all(i, i % 2)
        xs_ref[...] = land_ref[i % 2].astype(xs_ref.dtype)


def _gather_rows(xf, tok_c, cs_blk, n_csub, *, n_rows, ts):
    d = xf.shape[1]
    n_cs = cs_blk.shape[0]
    grid_spec = pltpu.PrefetchScalarGridSpec(
        num_scalar_prefetch=3,
        grid=(n_cs,),
        in_specs=[pl.BlockSpec(memory_space=pl.ANY)],
        out_specs=pl.BlockSpec((ts, d), lambda i, tok, blk, ncs: (blk[i], 0)),
        scratch_shapes=[pltpu.VMEM((2, ts, d), xf.dtype), pltpu.SemaphoreType.DMA((2,))],
    )
    return pl.pallas_call(
        functools.partial(_gather_body, ts=ts),
        grid_spec=grid_spec,
        out_shape=jax.ShapeDtypeStruct((n_rows, d), BF16),
        compiler_params=_params(("arbitrary",)),
        name="moe_gather",
    )(tok_c, cs_blk, n_csub, xf)


def _expert_changed(texp_ref, t):
    return (t == 0) | (texp_ref[t] != texp_ref[jnp.maximum(t - 1, 0)])


def _moe_up_body(texp_ref, tns_ref, nused_ref, x_ref, wg_ref, wu_ref, bg_ref, bu_ref, a_ref, wbf_ref,
                 *, ts):
    t = pl.program_id(1)

    @pl.when(t < nused_ref[0])
    def _():
        @pl.when(_expert_changed(texp_ref, t))
        def _():
            wbf_ref[0] = wg_ref[0].astype(BF16)
            wbf_ref[1] = wu_ref[0].astype(BF16)

        def sub(j, carry):
            r0 = pl.multiple_of(j * ts, ts)
            x = x_ref[pl.ds(r0, ts), :]
            g = jnp.dot(x, wbf_ref[0], preferred_element_type=F32) + bg_ref[0]
            lin = jnp.dot(x, wbf_ref[1], preferred_element_type=F32) + bu_ref[0]
            g = jnp.minimum(g, SWIGLU_LIMIT)
            lin = jnp.clip(lin, -SWIGLU_LIMIT, SWIGLU_LIMIT)
            a = g * jax.nn.sigmoid(SWIGLU_ALPHA * g) * (lin + 1.0)
            a_ref[pl.ds(r0, ts), :] = a.astype(a_ref.dtype)
            return carry

        lax.fori_loop(0, tns_ref[t], sub, 0)


def _moe_down_body(texp_ref, tns_ref, nused_ref, a_ref, wd_ref, bd_ref, y_ref, wbf_ref, *, ts):
    t = pl.program_id(1)

    @pl.when(t < nused_ref[0])
    def _():
        @pl.when(_expert_changed(texp_ref, t))
        def _():
            wbf_ref[...] = wd_ref[0].astype(BF16)

        def sub(j, carry):
            r0 = pl.multiple_of(j * ts, ts)
            y_ref[pl.ds(r0, ts), :] = jnp.dot(a_ref[pl.ds(r0, ts), :], wbf_ref[...],
                                              preferred_element_type=F32) + bd_ref[0]
            return carry

        lax.fori_loop(0, tns_ref[t], sub, 0)


def _moe_experts(xs, tile_exp, tile_nsub, n_used, w_gate_up, b_gate_up, w_down, b_down, *, tm, ts, tf, tn):
    n_rows, d = xs.shape
    n_exp, _, two_ff = w_gate_up.shape
    d_ff = two_ff // 2
    n_f = d_ff // tf
    n_t = n_rows // tm
    bgu = b_gate_up.reshape(n_exp, 1, two_ff)
    bdn = b_down.reshape(n_exp, 1, d)

    def tile(t, nu):
        return jnp.minimum(t, nu[0] - 1)

    up_spec = pltpu.PrefetchScalarGridSpec(
        num_scalar_prefetch=3,
        grid=(n_f, n_t),
        in_specs=[
            pl.BlockSpec((tm, d), lambda f, t, te, ns, nu: (tile(t, nu), 0)),
            pl.BlockSpec((1, d, tf), lambda f, t, te, ns, nu: (te[tile(t, nu)], 0, f)),
            pl.BlockSpec((1, d, tf), lambda f, t, te, ns, nu: (te[tile(t, nu)], 0, n_f + f)),
            pl.BlockSpec((1, 1, tf), lambda f, t, te, ns, nu: (te[tile(t, nu)], 0, f)),
            pl.BlockSpec((1, 1, tf), lambda f, t, te, ns, nu: (te[tile(t, nu)], 0, n_f + f)),
        ],
        out_specs=pl.BlockSpec((tm, tf), lambda f, t, te, ns, nu: (tile(t, nu), f)),
        scratch_shapes=[pltpu.VMEM((2, d, tf), BF16)],
    )
    act = pl.pallas_call(
        functools.partial(_moe_up_body, ts=ts),
        grid_spec=up_spec,
        out_shape=jax.ShapeDtypeStruct((n_rows, d_ff), BF16),
        compiler_params=_params(("arbitrary", "arbitrary")),
        name="moe_up",
    )(tile_exp, tile_nsub, n_used, xs, w_gate_up, w_gate_up, bgu, bgu)

    n_n = d // tn
    down_spec = pltpu.PrefetchScalarGridSpec(
        num_scalar_prefetch=3,
        grid=(n_n, n_t),
        in_specs=[
            pl.BlockSpec((tm, d_ff), lambda n, t, te, ns, nu: (tile(t, nu), 0)),
            pl.BlockSpec((1, d_ff, tn), lambda n, t, te, ns, nu: (te[tile(t, nu)], 0, n)),
            pl.BlockSpec((1, 1, tn), lambda n, t, te, ns, nu: (te[tile(t, nu)], 0, n)),
        ],
        out_specs=pl.BlockSpec((tm, tn), lambda n, t, te, ns, nu: (tile(t, nu), n)),
        scratch_shapes=[pltpu.VMEM((d_ff, tn), BF16)],
    )
    return pl.pallas_call(
        functools.partial(_moe_down_body, ts=ts),
        grid_spec=down_spec,
        out_shape=jax.ShapeDtypeStruct((n_rows, d), F32),
        compiler_params=_params(("arbitrary", "arbitrary")),
        name="moe_down",
    )(tile_exp, tile_nsub, n_used, act, w_down, bdn)


def _combine_body(dest_ref, x1_ref, gate_ref, nw_ref, yb_hbm, out_ref, rows_ref, sem, *, tt, top_k):
    i = pl.program_id(0)

    def row_copy(r, k):
        d = dest_ref[(i * tt + r) * top_k + k]
        return pltpu.make_async_copy(yb_hbm.at[pl.ds(d, 1)], rows_ref.at[k, pl.ds(r, 1)], sem)

    rows_per_iter = GATHER_UNROLL // top_k

    def issue(g, carry):
        for u in range(rows_per_iter):
            for k in range(top_k):
                row_copy(g * rows_per_iter + u, k).start()
        return carry

    def drain(g, carry):
        for u in range(rows_per_iter):
            for k in range(top_k):
                row_copy(g * rows_per_iter + u, k).wait()
        return carry

    lax.fori_loop(0, tt // rows_per_iter, issue, 0)
    lax.fori_loop(0, tt // rows_per_iter, drain, 0)
    gate = gate_ref[...]
    moe = rows_ref[0] * gate[:, 0:1]
    for k in range(1, top_k):
        moe = moe + rows_ref[k] * gate[:, k:k + 1]
    out_ref[...] = _rms(x1_ref[...] + moe, nw_ref[...])


def _combine(dest, x1, gate, norm_w, yb, *, tt):
    n, d = x1.shape
    grid_spec = pltpu.PrefetchScalarGridSpec(
        num_scalar_prefetch=1,
        grid=(n // tt,),
        in_specs=[
            pl.BlockSpec((tt, d), lambda i, ds: (i, 0)),
            pl.BlockSpec((tt, LANES), lambda i, ds: (i, 0)),
            pl.BlockSpec((1, d), lambda i, ds: (0, 0)),
            pl.BlockSpec(memory_space=pl.ANY),
        ],
        out_specs=pl.BlockSpec((tt, d), lambda i, ds: (i, 0)),
        scratch_shapes=[pltpu.VMEM((TOP_K, tt, d), F32), pltpu.SemaphoreType.DMA],
    )
    return pl.pallas_call(
        functools.partial(_combine_body, tt=tt, top_k=TOP_K),
        grid_spec=grid_spec,
        out_shape=jax.ShapeDtypeStruct((n, d), F32),
        compiler_params=_params(("arbitrary",)),
        name="moe_combine",
    )(dest, x1, gate, norm_w, yb)


def _tile(n, pref):
    t = min(n, pref)
    while n % t:
        t //= 2
    return t


def kernel(x_prompt, x_sample, cache_k, cache_v, state_conv, page_table, norm_mix_w, w_in,
           lambda_q1, lambda_k1, lambda_q2, lambda_k2, subln_w, conv_dw_w, conv_dw_b,
           conv_norm_w, conv_norm_b, w_o, norm_ffn_w, w_router, b_router, w_gate_up, b_gate_up,
           w_down, b_down, norm_final_w):
    batch, seq, d = x_prompt.shape
    dec, n_new, _ = x_sample.shape
    depth, n_pool, page, n_maps, d_qk = cache_k.shape
    n_heads, d_v = cache_v.shape[3:]
    aw = n_heads * d_v
    ch = d - aw
    n_pages = page_table.shape[1]
    past = n_pages * page
    rope_dim = d_qk // 4
    half = rope_dim // 2
    scale = d_qk ** -0.5
    n_exp = w_router.shape[2]
    n_p, n_s = batch * seq, dec * n_new

    h_p = x_prompt.reshape(n_p, d)
    h_s = x_sample.reshape(n_s, d)
    tab_p = _rope_tables(jnp.arange(seq, dtype=I32), d_qk, rope_dim)
    tab_s = _rope_tables(past + jnp.arange(n_s, dtype=I32) % n_new, d_qk, rope_dim)
    outs = {k: [] for k in ("kp", "vp", "cp", "ks", "vs", "cs")}

    for l in range(depth):
        lam_init = 0.8 - 0.6 * math.exp(-0.3 * l)
        lams = tuple(v[l].reshape(1, d_qk) for v in (lambda_q1, lambda_k1, lambda_q2, lambda_k2))
        sw = subln_w[l].reshape(1, d_v)
        w_in_bf = w_in[l].astype(BF16)
        w_o_bf = w_o[l].astype(BF16)
        nmw = norm_mix_w[l].reshape(1, d)
        dw_b = conv_dw_b[l].reshape(1, ch)
        ln_w = conv_norm_w[l].reshape(1, ch)
        ln_b = conv_norm_b[l].reshape(1, ch)

        tt_p = _tile(seq, 512)
        q_p, k_p, v_p, u_p = _proj(h_p, nmw, w_in_bf, tab_p, scale=scale, half=half, tt=tt_p)
        tt_s = _tile(n_s, 512)
        q_s, k_s, v_s, u_s = _proj(h_s, nmw, w_in_bf, tab_s, scale=scale, half=half, tt=tt_s)

        o_p = _attn_prompt(q_p, k_p, v_p, lams, sw, batch=batch, seq=seq, n_heads=n_heads,
                           d_qk=d_qk, lam_init=lam_init, tq=_tile(seq, 512))
        q_rows = q_s.reshape(dec, n_new, n_maps, d_qk).transpose(0, 2, 1, 3)
        q_rows = jnp.pad(q_rows, ((0, 0), (0, 0), (0, QR - n_new), (0, 0))).reshape(dec, n_maps * QR, d_qk)
        ck = cache_k.transpose(0, 1, 3, 4, 2).reshape(depth * n_pool, n_maps * d_qk, page)
        cv = cache_v.reshape(depth * n_pool, page * n_heads, d_v)
        o_s = _attn_sample(page_table + l * n_pool, q_rows, k_s.reshape(dec, n_new * n_maps, d_qk),
                           v_s.reshape(dec, n_new * n_heads, d_v), ck, cv, lams, sw, n_new=n_new,
                           n_maps=n_maps, n_heads=n_heads, qr=QR, lam_init=lam_init,
                           n_chunk=_tile(n_pages, 8))
        o_s = o_s.reshape(n_s, aw).astype(BF16)

        c_p = _conv_prompt(u_p, conv_dw_w[l], dw_b, ln_w, ln_b, batch=batch, seq=seq,
                           tt=_tile(seq, 256))
        u_s3 = u_s.reshape(dec, n_new, ch)
        u_pad_s = jnp.concatenate([state_conv[l], u_s3], axis=1)
        c_s = _conv_sample(u_pad_s.transpose(1, 0, 2), conv_dw_w[l], dw_b, ln_w, ln_b,
                           n_new=n_new, tb=_tile(dec, 16))
        c_s = c_s.transpose(1, 0, 2).reshape(n_s, ch)

        nfw = norm_ffn_w[l].reshape(1, d)
        br = b_router[l].reshape(1, n_exp)
        assert n_p % tt_s == 0
        x1_p, xf_all, idx_p, gate_p = _wo_router(o_p, c_p, h_p, w_o_bf, nfw, w_router[l], br, tt=tt_p,
                                                 xf_rows=n_p + n_s)
        x1_s, xf_all, idx_s, gate_s = _wo_router(o_s, c_s, h_s, w_o_bf, nfw, w_router[l], br, tt=tt_s,
                                                 xf_rows=n_p + n_s, xf_row0=n_p, xf_buf=xf_all)

        idx_all = jnp.concatenate([idx_p[:, :TOP_K], idx_s[:, :TOP_K]], axis=0)
        dest, tile_exp, tile_nsub, n_used, tok_c, cs_blk, n_csub = _route_plan(
            idx_all, n_exp, MOE_TILE, MOE_SUBTILE)
        xs = _gather_rows(xf_all, tok_c, cs_blk, n_csub, n_rows=tile_exp.shape[0] * MOE_TILE,
                          ts=MOE_SUBTILE)
        yb = _moe_experts(xs, tile_exp, tile_nsub, n_used, w_gate_up[l], b_gate_up[l], w_down[l],
                          b_down[l], tm=MOE_TILE, ts=MOE_SUBTILE, tf=_tile(w_down.shape[2], 1024),
                          tn=_tile(d, 1024))

        last = l == depth - 1
        nw_out = norm_final_w.reshape(1, d) if last else jnp.ones((1, d), F32)
        assert last, "the combine kernel fuses the final norm; deeper stacks need an un-normed variant"
        h_p = _combine(dest[:n_p * TOP_K], x1_p, gate_p, nw_out, yb, tt=_tile(n_p, 256))
        h_s = _combine(dest[n_p * TOP_K:], x1_s, gate_s, nw_out, yb, tt=_tile(n_s, 256))

        outs["kp"].append(k_p.reshape(batch, seq, n_maps, d_qk))
        outs["vp"].append(v_p.reshape(batch, seq, n_heads, d_v))
        outs["cp"].append(u_p.reshape(batch, seq, ch)[:, seq - (conv_dw_w.shape[1] - 1):])
        outs["ks"].append(k_s.reshape(dec, n_new, n_maps, d_qk))
        outs["vs"].append(v_s.reshape(dec, n_new, n_heads, d_v))
        outs["cs"].append(u_pad_s[:, n_new:])

    return (h_p.reshape(batch, seq, d), h_s.reshape(dec, n_new, d),
            jnp.stack(outs["kp"]), jnp.stack(outs["vp"]), jnp.stack(outs["cp"]),
            jnp.stack(outs["ks"]), jnp.stack(outs["vs"]), jnp.stack(outs["cs"]))
```

```python
import functools
import math

import jax
import jax.numpy as jnp
from jax import lax
from jax.experimental import pallas as pl
from jax.experimental.pallas import tpu as pltpu

F32 = jnp.float32
BF16 = jnp.bfloat16
I32 = jnp.int32

EPS = 1e-5
ROPE_THETA = 500000.0
TOP_K = 4
SWIGLU_LIMIT = 7.0
SWIGLU_ALPHA = 1.702
LANES = 128
VMEM_LIMIT = 56 * 1024 * 1024
QR = 8
MOE_TILE = 768
MOE_SUBTILE = 256
GATHER_UNROLL = 8
GATHER_SLOTS = 3


def _params(sem, vmem=VMEM_LIMIT):
    return pltpu.CompilerParams(dimension_semantics=sem, vmem_limit_bytes=vmem)


def _nt_dot(a, b):
    return lax.dot_general(a, b, (((1,), (1,)), ((), ())), preferred_element_type=F32)


def _rms(x, w):
    ms = jnp.mean(x * x, axis=-1, keepdims=True)
    return x * lax.rsqrt(ms + EPS) * w


def _lambda(lq1, lk1, lq2, lk2, lam_init):
    a = jnp.sum(lq1 * lk1, axis=-1, keepdims=True)
    b = jnp.sum(lq2 * lk2, axis=-1, keepdims=True)
    return jnp.exp(a) - jnp.exp(b) + lam_init


def _proj_body(x_ref, nw_ref, wa_ref, wb_ref, cs_ref, s1_ref, s2_ref,
               q_ref, k_ref, v_ref, u_ref, xn_ref, *, scale, half):
    j = pl.program_id(1)

    @pl.when(j == 0)
    def _():
        xn_ref[...] = _rms(x_ref[...], nw_ref[...]).astype(BF16)

    def rope_store(dst_ref, p, mul):
        cs, s1, s2 = cs_ref[...], s1_ref[...], s2_ref[...]
        for c in range(p.shape[1] // LANES):
            seg = p[:, c * LANES:(c + 1) * LANES]
            r = (seg * cs + pltpu.roll(seg, LANES - half, 1) * s1
                 + pltpu.roll(seg, half, 1) * s2)
            if mul is not None:
                r = r * mul
            dst_ref[:, c * LANES:(c + 1) * LANES] = r.astype(dst_ref.dtype)

    @pl.when(j == 0)
    def _():
        p = jnp.dot(xn_ref[...], wa_ref[...], preferred_element_type=F32)
        rope_store(q_ref, p, scale)

    @pl.when(j == 1)
    def _():
        p = jnp.dot(xn_ref[...], wa_ref[...], preferred_element_type=F32)
        rope_store(k_ref, p, None)

    @pl.when(j == 2)
    def _():
        v_ref[...] = jnp.dot(xn_ref[...], wa_ref[...], preferred_element_type=F32)

    @pl.when(j == 3)
    def _():
        xn = xn_ref[...]
        val = jnp.dot(xn, wa_ref[...], preferred_element_type=F32)
        gate = jnp.dot(xn, wb_ref[...], preferred_element_type=F32)
        u_ref[...] = val * jax.nn.sigmoid(gate)


def _proj(x2d, norm_w, w_in_bf, tables, *, scale, half, tt):
    n, d = x2d.shape
    w = w_in_bf.shape[1] // 5
    cs, s1, s2 = tables
    n_pos_blocks = cs.shape[0] // tt
    tab_spec = pl.BlockSpec((tt, LANES), lambda i, j: (i % n_pos_blocks, 0))
    out_spec = pl.BlockSpec((tt, w), lambda i, j: (i, 0))
    return pl.pallas_call(
        functools.partial(_proj_body, scale=scale, half=half),
        grid=(n // tt, 4),
        in_specs=[
            pl.BlockSpec((tt, d), lambda i, j: (i, 0)),
            pl.BlockSpec((1, d), lambda i, j: (0, 0)),
            pl.BlockSpec((d, w), lambda i, j: (0, j)),
            pl.BlockSpec((d, w), lambda i, j: (0, 4)),
            tab_spec, tab_spec, tab_spec,
        ],
        out_specs=[out_spec, out_spec, out_spec, out_spec],
        out_shape=[
            jax.ShapeDtypeStruct((n, w), BF16),
            jax.ShapeDtypeStruct((n, w), F32),
            jax.ShapeDtypeStruct((n, w), F32),
            jax.ShapeDtypeStruct((n, w), F32),
        ],
        scratch_shapes=[pltpu.VMEM((tt, d), BF16)],
        compiler_params=_params(("arbitrary", "arbitrary")),
        name="proj",
    )(x2d, norm_w, w_in_bf, w_in_bf, cs, s1, s2)


def _rope_tables(pos, d_qk, rope_dim):
    half = rope_dim // 2
    inv = ROPE_THETA ** (-jnp.arange(0, rope_dim, 2, dtype=F32) / rope_dim)
    ang = pos.astype(F32)[:, None] * inv[None, :]
    cos, sin = jnp.cos(ang), jnp.sin(ang)
    p = pos.shape[0]
    zh = jnp.zeros((p, half), F32)
    zr = jnp.zeros((p, d_qk - rope_dim), F32)
    cs = jnp.concatenate([cos, cos, jnp.ones((p, d_qk - rope_dim), F32)], axis=-1)
    s1 = jnp.concatenate([-sin, zh, zr], axis=-1)
    s2 = jnp.concatenate([zh, sin, zr], axis=-1)
    rep = LANES // d_qk
    return tuple(jnp.tile(t, (1, rep)) for t in (cs, s1, s2))


def _attn_prompt_body(q_ref, k_ref, v_ref, lq1_ref, lk1_ref, lq2_ref, lk2_ref, sw_ref,
                      o_ref, kb_ref, vb_ref, m_ref, l_ref, acc_ref, *, tq, tk, d_qk, lam_init):
    qi = pl.program_id(2)

    @pl.when(qi == 0)
    def _():
        kb_ref[...] = k_ref[...].astype(BF16)
        vb_ref[...] = v_ref[...].astype(BF16)

    q = q_ref[...]
    lane = lax.broadcasted_iota(I32, q.shape, 1)
    zero = jnp.zeros_like(q)
    qm = (jnp.where(lane < d_qk, q, zero), jnp.where(lane >= d_qk, q, zero))
    m_ref[...] = jnp.full(m_ref.shape, -jnp.inf, F32)
    l_ref[...] = jnp.zeros(l_ref.shape, F32)
    acc_ref[...] = jnp.zeros(acc_ref.shape, F32)

    def step(c, masked):
        k0 = pl.multiple_of(c * tk, tk)
        kc = kb_ref[pl.ds(k0, tk), :]
        vc = vb_ref[pl.ds(k0, tk), :]
        for mi in range(2):
            s = _nt_dot(qm[mi], kc)
            if masked:
                row = lax.broadcasted_iota(I32, (tq, tk), 0)
                col = lax.broadcasted_iota(I32, (tq, tk), 1)
                s = jnp.where(col + c * tk <= row + qi * tq, s, -jnp.inf)
            m_p = m_ref[mi]
            m_n = jnp.maximum(m_p, jnp.max(s, axis=1, keepdims=True))
            p = jnp.exp(s - jnp.concatenate([m_n] * (tk // LANES), axis=1))
            al = jnp.exp(m_p - m_n)
            l_ref[mi] = al * l_ref[mi] + jnp.sum(p, axis=1, keepdims=True)
            acc_ref[mi] = al * acc_ref[mi] + jnp.dot(p.astype(BF16), vc, preferred_element_type=F32)
            m_ref[mi] = m_n

    def full_step(c, carry):
        step(c, False)
        return carry

    n_full = qi * (tq // tk)
    lax.fori_loop(0, n_full, full_step, 0)
    for dgl in range(tq // tk):
        step(n_full + dgl, True)

    lam = _lambda(lq1_ref[...], lk1_ref[...], lq2_ref[...], lk2_ref[...], lam_init)
    o = acc_ref[0] / l_ref[0] - lam * (acc_ref[1] / l_ref[1])
    o_ref[...] = (_rms(o, sw_ref[...]) * (1.0 - lam_init)).astype(o_ref.dtype)


def _attn_prompt(q, k, v, lams, subln_w, *, batch, seq, n_heads, d_qk, lam_init, tq):
    n, w = q.shape
    d_v = w // n_heads
    nq = seq // tq
    vec = pl.BlockSpec((1, d_qk), lambda b, h, i: (0, 0))
    return pl.pallas_call(
        functools.partial(_attn_prompt_body, tq=tq, tk=tq, d_qk=d_qk, lam_init=lam_init),
        grid=(batch, n_heads, nq),
        in_specs=[
            pl.BlockSpec((tq, d_v), lambda b, h, i: (b * nq + i, h)),
            pl.BlockSpec((seq, d_v), lambda b, h, i: (b, h)),
            pl.BlockSpec((seq, d_v), lambda b, h, i: (b, h)),
            vec, vec, vec, vec,
            pl.BlockSpec((1, d_v), lambda b, h, i: (0, 0)),
        ],
        out_specs=pl.BlockSpec((tq, d_v), lambda b, h, i: (b * nq + i, h)),
        out_shape=jax.ShapeDtypeStruct((n, w), BF16),
        scratch_shapes=[pltpu.VMEM((seq, d_v), BF16), pltpu.VMEM((seq, d_v), BF16),
                        pltpu.VMEM((2, tq, LANES), F32), pltpu.VMEM((2, tq, LANES), F32),
                        pltpu.VMEM((2, tq, d_v), F32)],
        compiler_params=_params(("arbitrary", "arbitrary", "arbitrary")),
        name="attn_prompt",
    )(q, k, v, *lams, subln_w)


def _attn_sample_body(pt_ref, q_ref, kn_ref, vn_ref, lq1_ref, lk1_ref, lq2_ref, lk2_ref, sw_ref,
                      *rest, n_chunk, n_new, n_maps, n_heads, qr, lam_init):
    del pt_ref
    k_refs = rest[:n_chunk]
    v_refs = rest[n_chunk:2 * n_chunk]
    o_ref = rest[2 * n_chunk]
    m_ref, l_ref, acc_ref, qbd_ref = rest[2 * n_chunk + 1:]
    c = pl.program_id(1)
    page = k_refs[0].shape[2]
    d_qk = k_refs[0].shape[1] // n_maps
    hr = 2 * qr

    @pl.when(c == 0)
    def _():
        m_ref[...] = jnp.full(m_ref.shape, -jnp.inf, F32)
        l_ref[...] = jnp.zeros(l_ref.shape, F32)
        acc_ref[...] = jnp.zeros(acc_ref.shape, F32)

    q_all = q_ref[0]

    @pl.when(c == 0)
    def _():
        q_rep = jnp.concatenate([q_all] * n_maps, axis=1)
        row_map = lax.broadcasted_iota(I32, q_rep.shape, 0) // qr
        col_map = lax.broadcasted_iota(I32, q_rep.shape, 1) // d_qk
        qbd_ref[...] = jnp.where(row_map == col_map, q_rep, jnp.zeros_like(q_rep))

    m_p, l_p, acc = m_ref[...], l_ref[...], acc_ref[...]
    k_cat = jnp.concatenate([k_refs[i][0].astype(BF16) for i in range(n_chunk)], axis=1)
    s = jnp.dot(qbd_ref[...], k_cat, preferred_element_type=F32)
    m_n = jnp.maximum(m_p, jnp.max(s, axis=1, keepdims=True))
    p = jnp.exp(s - m_n)
    al = jnp.exp(m_p - m_n)
    l_p = al * l_p + jnp.sum(p, axis=1, keepdims=True)
    pb = p.astype(BF16)
    pv = []
    for h in range(n_heads):
        vh = jnp.concatenate(
            [v_refs[i][0, pl.ds(h, page, stride=n_heads), :].astype(BF16) for i in range(n_chunk)],
            axis=0)
        pv.append(jnp.dot(pb[h * hr:(h + 1) * hr], vh, preferred_element_type=F32))
    acc = al * acc + jnp.concatenate(pv, axis=0)
    m_p = m_n
    m_ref[...] = m_p
    l_ref[...] = l_p
    acc_ref[...] = acc

    @pl.when(c == pl.num_programs(1) - 1)
    def _():
        qf = q_all.astype(F32)
        kn, vn = kn_ref[0], vn_ref[0]
        q_tok = lax.broadcasted_iota(I32, (qf.shape[0], 1), 0) % qr
        s_new = []
        for j in range(n_new):
            parts = []
            for m in range(n_maps):
                krow = kn[j * n_maps + m:j * n_maps + m + 1, :]
                parts.append(jnp.sum(qf[m * qr:(m + 1) * qr] * krow, axis=1, keepdims=True))
            s_new.append(jnp.where(q_tok >= j, jnp.concatenate(parts, axis=0), -jnp.inf))
        m_n = m_p
        for sj in s_new:
            m_n = jnp.maximum(m_n, sj)
        al = jnp.exp(m_p - m_n)
        l_n = al * l_p
        a_n = al * acc
        for j in range(n_new):
            pj = jnp.exp(s_new[j] - m_n)
            l_n = l_n + pj
            a_n = a_n + jnp.concatenate(
                [pj[h * hr:(h + 1) * hr] * vn[j * n_heads + h:j * n_heads + h + 1, :]
                 for h in range(n_heads)], axis=0)
        a_n = a_n / l_n
        lam = _lambda(lq1_ref[...], lk1_ref[...], lq2_ref[...], lk2_ref[...], lam_init)
        d_v = a_n.shape[1]
        for h in range(n_heads):
            o = a_n[h * hr:h * hr + n_new] - lam * a_n[h * hr + qr:h * hr + qr + n_new]
            o_ref[0, :, h * d_v:(h + 1) * d_v] = (
                _rms(o, sw_ref[...]) * (1.0 - lam_init)).astype(o_ref.dtype)


def _attn_sample(pages, q_rows, k_new, v_new, ck, cv, lams, subln_w, *, n_new, n_maps, n_heads, qr,
                 lam_init, n_chunk):
    dec, n_pages = pages.shape
    d_qk = ck.shape[1] // n_maps
    d_v = cv.shape[2]
    rows = n_maps * qr
    pt_flat = pages.reshape(-1)

    def page_map(i, b, c, pt):
        return (pt[b * n_pages + c * n_chunk + i], 0, 0)

    vec = pl.BlockSpec((1, d_qk), lambda b, c, pt: (0, 0))
    k_specs = [pl.BlockSpec((1,) + ck.shape[1:], functools.partial(page_map, i)) for i in range(n_chunk)]
    v_specs = [pl.BlockSpec((1,) + cv.shape[1:], functools.partial(page_map, i)) for i in range(n_chunk)]
    grid_spec = pltpu.PrefetchScalarGridSpec(
        num_scalar_prefetch=1,
        grid=(dec, n_pages // n_chunk),
        in_specs=[
            pl.BlockSpec((1, rows, d_qk), lambda b, c, pt: (b, 0, 0)),
            pl.BlockSpec((1,) + k_new.shape[1:], lambda b, c, pt: (b, 0, 0)),
            pl.BlockSpec((1,) + v_new.shape[1:], lambda b, c, pt: (b, 0, 0)),
            vec, vec, vec, vec,
            pl.BlockSpec((1, d_v), lambda b, c, pt: (0, 0)),
        ] + k_specs + v_specs,
        out_specs=pl.BlockSpec((1, n_new, n_heads * d_v), lambda b, c, pt: (b, 0, 0)),
        scratch_shapes=[pltpu.VMEM((rows, 1), F32), pltpu.VMEM((rows, 1), F32),
                        pltpu.VMEM((rows, d_v), F32), pltpu.VMEM((rows, n_maps * d_qk), BF16)],
    )
    return pl.pallas_call(
        functools.partial(_attn_sample_body, n_chunk=n_chunk, n_new=n_new, n_maps=n_maps,
                          n_heads=n_heads, qr=qr, lam_init=lam_init),
        grid_spec=grid_spec,
        out_shape=jax.ShapeDtypeStruct((dec, n_new, n_heads * d_v), F32),
        compiler_params=_params(("arbitrary", "arbitrary")),
        name="attn_sample",
    )(pt_flat, q_rows, k_new, v_new, *lams, subln_w, *([ck] * n_chunk), *([cv] * n_chunk))


def _ln_swish(y, lw, lb):
    mu = jnp.mean(y, axis=-1, keepdims=True)
    yc = y - mu
    var = jnp.mean(yc * yc, axis=-1, keepdims=True)
    z = yc * lax.rsqrt(var + EPS) * lw + lb
    return z * jax.nn.sigmoid(z)


def _conv_prompt_body(u_ref, up_ref, dw_ref, db_ref, lw_ref, lb_ref, c_ref, buf_ref, y_ref,
                      *, tt, width, halo, rb):
    t = pl.program_id(1)
    buf_ref[0:halo, :] = jnp.where(t == 0, 0.0, up_ref[...])
    buf_ref[halo:halo + tt, :] = u_ref[...]
    off = halo - (width - 1)

    def col_body(cc, carry):
        c0 = pl.multiple_of(cc * LANES, LANES)
        wts = dw_ref[:, pl.ds(c0, LANES)]
        bias = db_ref[:, pl.ds(c0, LANES)]
        for r in range(tt // rb):
            acc = jnp.zeros((rb, LANES), F32)
            for w in range(width):
                lo = r * rb + off + w
                acc = acc + buf_ref[lo:lo + rb, pl.ds(c0, LANES)] * wts[w:w + 1, :]
            y_ref[r * rb:(r + 1) * rb, pl.ds(c0, LANES)] = acc + bias
        return carry

    lax.fori_loop(0, u_ref.shape[1] // LANES, col_body, 0)
    c_ref[...] = _ln_swish(y_ref[...], lw_ref[...], lb_ref[...]).astype(c_ref.dtype)


def _conv_prompt(u, dw_w, dw_b, ln_w, ln_b, *, batch, seq, tt):
    n, ch = u.shape
    width = dw_w.shape[0]
    halo = 32
    nt = seq // tt
    per = tt // halo
    row = pl.BlockSpec((1, ch), lambda b, t: (0, 0))
    return pl.pallas_call(
        functools.partial(_conv_prompt_body, tt=tt, width=width, halo=halo, rb=64),
        grid=(batch, nt),
        in_specs=[
            pl.BlockSpec((tt, ch), lambda b, t: (b * nt + t, 0)),
            pl.BlockSpec((halo, ch), lambda b, t: (jnp.maximum((b * nt + t) * per - 1, 0), 0)),
            pl.BlockSpec((width, ch), lambda b, t: (0, 0)),
            row, row, row,
        ],
        out_specs=pl.BlockSpec((tt, ch), lambda b, t: (b * nt + t, 0)),
        out_shape=jax.ShapeDtypeStruct((n, ch), BF16),
        scratch_shapes=[pltpu.VMEM((halo + tt, ch), F32), pltpu.VMEM((tt, ch), F32)],
        compiler_params=_params(("arbitrary", "arbitrary")),
        name="conv_prompt",
    )(u, u, dw_w, dw_b, ln_w, ln_b)


def _conv_sample_body(up_ref, dw_ref, db_ref, lw_ref, lb_ref, c_ref, y_ref, *, width, n_new):
    length, tb, ch = up_ref.shape

    def col_body(cc, carry):
        c0 = pl.multiple_of(cc * LANES, LANES)
        wts = dw_ref[:, pl.ds(c0, LANES)]
        bias = db_ref[:, pl.ds(c0, LANES)]
        accs = [jnp.zeros((tb, LANES), F32) for _ in range(n_new)]
        for j in range(length):
            x = up_ref[j, :, pl.ds(c0, LANES)]
            for t in range(n_new):
                w = j - t
                if 0 <= w < width:
                    accs[t] = accs[t] + x * wts[w:w + 1, :]
        for t in range(n_new):
            y_ref[t, :, pl.ds(c0, LANES)] = accs[t] + bias
        return carry

    lax.fori_loop(0, ch // LANES, col_body, 0)
    c_ref[...] = _ln_swish(y_ref[...], lw_ref[...], lb_ref[...]).astype(c_ref.dtype)


def _conv_sample(u_pad_t, dw_w, dw_b, ln_w, ln_b, *, n_new, tb):
    length, dec, ch = u_pad_t.shape
    width = dw_w.shape[0]
    row = pl.BlockSpec((1, ch), lambda b: (0, 0))
    return pl.pallas_call(
        functools.partial(_conv_sample_body, width=width, n_new=n_new),
        grid=(dec // tb,),
        in_specs=[
            pl.BlockSpec((length, tb, ch), lambda b: (0, b, 0)),
            pl.BlockSpec((width, ch), lambda b: (0, 0)),
            row, row, row,
        ],
        out_specs=pl.BlockSpec((n_new, tb, ch), lambda b: (0, b, 0)),
        out_shape=jax.ShapeDtypeStruct((n_new, dec, ch), BF16),
        scratch_shapes=[pltpu.VMEM((n_new, tb, ch), F32)],
        compiler_params=_params(("arbitrary",)),
        name="conv_sample",
    )(u_pad_t, dw_w, dw_b, ln_w, ln_b)


def _wo_body(o_ref, c_ref, x_ref, wo_ref, nw_ref, wr_ref, br_ref, *rest, aw, top_k):
    x1_ref, xf_ref, idx_ref, gate_ref = rest[-4:]
    attn = (jnp.dot(o_ref[...], wo_ref[0:aw, :], preferred_element_type=F32)
            + jnp.dot(c_ref[...], wo_ref[aw:, :], preferred_element_type=F32))
    x1 = x_ref[...] + attn
    x1_ref[...] = x1
    xf = _rms(x1, nw_ref[...])
    xf_ref[...] = xf
    xh = xf.astype(BF16)
    xl = (xf - xh.astype(F32)).astype(BF16)
    wr = wr_ref[...]
    wh = wr.astype(BF16)
    wl = (wr - wh.astype(F32)).astype(BF16)
    logits = (jnp.dot(xh, wh, preferred_element_type=F32) + jnp.dot(xl, wh, preferred_element_type=F32)
              + jnp.dot(xh, wl, preferred_element_type=F32)) + br_ref[...]
    n_exp = logits.shape[1]
    lane_e = lax.broadcasted_iota(I32, logits.shape, 1)
    lane_o = lax.broadcasted_iota(I32, idx_ref.shape, 1)
    vals = logits
    tops, idxs = [], []
    for _ in range(top_k):
        mx = jnp.max(vals, axis=1, keepdims=True)
        am = jnp.min(jnp.where(vals == mx, lane_e, n_exp), axis=1, keepdims=True)
        tops.append(mx)
        idxs.append(am)
        vals = jnp.where(lane_e == am, -jnp.inf, vals)
    es = [jnp.exp(t - tops[0]) for t in tops]
    den = es[0]
    for e in es[1:]:
        den = den + e
    idx_out = jnp.zeros(idx_ref.shape, I32)
    gate_out = jnp.zeros(gate_ref.shape, F32)
    for k in range(top_k):
        idx_out = jnp.where(lane_o == k, idxs[k], idx_out)
        gate_out = jnp.where(lane_o == k, es[k] / den, gate_out)
    idx_ref[...] = idx_out
    gate_ref[...] = gate_out


def _wo_router(o, c, x2d, w_o_bf, norm_w, w_router, b_router, *, tt, xf_rows, xf_row0=0, xf_buf=None):
    n, d = x2d.shape
    aw = o.shape[1]
    n_exp = w_router.shape[1]
    blk0 = xf_row0 // tt
    row = pl.BlockSpec((tt, d), lambda i: (i, 0))
    half = pl.BlockSpec((tt, aw), lambda i: (i, 0))
    small = pl.BlockSpec((tt, LANES), lambda i: (i, 0))
    in_specs = [
        half, pl.BlockSpec((tt, d - aw), lambda i: (i, 0)), row,
        pl.BlockSpec((d, d), lambda i: (0, 0)),
        pl.BlockSpec((1, d), lambda i: (0, 0)),
        pl.BlockSpec((d, n_exp), lambda i: (0, 0)),
        pl.BlockSpec((1, n_exp), lambda i: (0, 0)),
    ]
    args = [o, c, x2d, w_o_bf, norm_w, w_router, b_router]
    aliases = {}
    if xf_buf is not None:
        in_specs.append(pl.BlockSpec(memory_space=pl.ANY))
        args.append(xf_buf)
        aliases = {len(args) - 1: 1}
    return pl.pallas_call(
        functools.partial(_wo_body, aw=aw, top_k=TOP_K),
        grid=(n // tt,),
        in_specs=in_specs,
        out_specs=[row, pl.BlockSpec((tt, d), lambda i: (blk0 + i, 0)), small, small],
        out_shape=[
            jax.ShapeDtypeStruct((n, d), F32),
            jax.ShapeDtypeStruct((xf_rows, d), F32),
            jax.ShapeDtypeStruct((n, LANES), I32),
            jax.ShapeDtypeStruct((n, LANES), F32),
        ],
        input_output_aliases=aliases,
        compiler_params=_params(("arbitrary",)),
        name="wo_router",
    )(*args)


def _route_plan(idx, n_exp, tm, ts):
    n_tok, top_k = idx.shape
    n_asg = n_tok * top_k
    nsub = tm // ts
    flat_e = idx.reshape(-1)
    onehot = (flat_e[:, None] == jnp.arange(n_exp, dtype=I32)[None, :]).astype(I32)
    csum = jnp.cumsum(onehot, axis=0)
    rank = jnp.sum((csum - onehot) * onehot, axis=1)
    counts = csum[-1]
    ntile = (counts + tm - 1) // tm
    first_rows = counts - jnp.maximum(ntile - 1, 0) * tm
    tile_end = jnp.cumsum(ntile)
    tile_beg = tile_end - ntile
    fr = first_rows[flat_e]
    dest = (tile_beg[flat_e] * tm + jnp.where(rank < fr, rank, rank - fr + tm)).astype(I32)

    n_tiles = -(-n_asg // tm) + n_exp
    t_ids = jnp.arange(n_tiles, dtype=I32)
    tile_exp = jnp.minimum(jnp.sum(t_ids[:, None] >= tile_end[None, :], axis=1), n_exp - 1).astype(I32)
    n_used = tile_end[-1]
    first_sub = (first_rows[tile_exp] + ts - 1) // ts
    tile_nsub = jnp.where(t_ids < n_used, jnp.where(t_ids == tile_beg[tile_exp], first_sub, nsub), 0)
    tile_nsub = tile_nsub.astype(I32)

    n_cs = -(-n_asg // ts) + n_exp
    sub_end = jnp.cumsum(tile_nsub)
    n_csub = sub_end[-1]
    i_ids = jnp.minimum(jnp.arange(n_cs, dtype=I32), n_csub - 1)
    cs_tile = jnp.minimum(jnp.sum(i_ids[:, None] >= sub_end[None, :], axis=1), n_tiles - 1)
    cs_blk = (cs_tile * nsub + i_ids - (sub_end - tile_nsub)[cs_tile]).astype(I32)
    flat_tok = jnp.arange(n_asg, dtype=I32) // top_k
    buf_tok = jnp.zeros((n_tiles * tm,), I32).at[dest].set(flat_tok, unique_indices=True)
    tok_c = buf_tok.reshape(n_tiles * nsub, ts)[cs_blk].reshape(-1)
    return (dest, tile_exp, tile_nsub, n_used.astype(I32).reshape(1), tok_c, cs_blk,
            n_csub.astype(I32).reshape(1))


def _gather_body(tok_ref, blk_ref, ncs_ref, x_hbm, xs_ref, land_ref, sem, *, ts):
    del blk_ref
    i = pl.program_id(0)
    n = ncs_ref[0]

    def row_copy(step, slot, r):
        tok = tok_ref[step * ts + r]
        return pltpu.make_async_copy(x_hbm.at[pl.ds(tok, 1)], land_ref.at[slot, pl.ds(r, 1)],
                                     sem.at[slot])

    def start_all(step, slot):
        def body(g, carry):
            for u in range(GATHER_UNROLL):
                row_copy(step, slot, g * GATHER_UNROLL + u).start()
            return carry
        lax.fori_loop(0, ts // GATHER_UNROLL, body, 0)

    def wait_all(step, slot):
        def body(g, carry):
            for u in range(GATHER_UNROLL):
                row_copy(step, slot, g * GATHER_UNROLL + u).wait()
            return carry
        lax.fori_loop(0, ts // GATHER_UNROLL, body, 0)

    ahead = GATHER_SLOTS - 1

    @pl.when(i == 0)
    def _():
        for s in range(ahead):
            @pl.when(s < n)
            def _():
                start_all(s, s)

    @pl.when(i + ahead < n)
    def _():
        start_all(i + ahead, (i + ahead) % GATHER_SLOTS)

    @pl.when(i < n)
    def _():
        wait_all(i, i % GATHER_SLOTS)
        xs_ref[...] = land_ref[i % GATHER_SLOTS].astype(xs_ref.dtype)


def _gather_rows(xf, tok_c, cs_blk, n_csub, *, n_rows, ts):
    d = xf.shape[1]
    n_cs = cs_blk.shape[0]
    grid_spec = pltpu.PrefetchScalarGridSpec(
        num_scalar_prefetch=3,
        grid=(n_cs,),
        in_specs=[pl.BlockSpec(memory_space=pl.ANY)],
        out_specs=pl.BlockSpec((ts, d), lambda i, tok, blk, ncs: (blk[i], 0)),
        scratch_shapes=[pltpu.VMEM((GATHER_SLOTS, ts, d), xf.dtype),
                        pltpu.SemaphoreType.DMA((GATHER_SLOTS,))],
    )
    return pl.pallas_call(
        functools.partial(_gather_body, ts=ts),
        grid_spec=grid_spec,
        out_shape=jax.ShapeDtypeStruct((n_rows, d), BF16),
        compiler_params=_params(("arbitrary",)),
        name="moe_gather",
    )(tok_c, cs_blk, n_csub, xf)


def _expert_changed(texp_ref, t):
    return (t == 0) | (texp_ref[t] != texp_ref[jnp.maximum(t - 1, 0)])


def _moe_up_body(texp_ref, tns_ref, nused_ref, x_ref, wg_ref, wu_ref, bg_ref, bu_ref, a_ref, wbf_ref,
                 *, ts):
    t = pl.program_id(1)

    @pl.when(t < nused_ref[0])
    def _():
        @pl.when(_expert_changed(texp_ref, t))
        def _():
            wbf_ref[0] = wg_ref[0].astype(BF16)
            wbf_ref[1] = wu_ref[0].astype(BF16)

        def sub(j, carry):
            r0 = pl.multiple_of(j * ts, ts)
            x = x_ref[pl.ds(r0, ts), :]
            g = jnp.dot(x, wbf_ref[0], preferred_element_type=F32) + bg_ref[0]
            lin = jnp.dot(x, wbf_ref[1], preferred_element_type=F32) + bu_ref[0]
            g = jnp.minimum(g, SWIGLU_LIMIT)
            lin = jnp.clip(lin, -SWIGLU_LIMIT, SWIGLU_LIMIT)
            a = g * jax.nn.sigmoid(SWIGLU_ALPHA * g) * (lin + 1.0)
            a_ref[pl.ds(r0, ts), :] = a.astype(a_ref.dtype)
            return carry

        lax.fori_loop(0, tns_ref[t], sub, 0)


def _moe_down_body(texp_ref, tns_ref, nused_ref, a_ref, wd_ref, bd_ref, y_ref, wbf_ref, *, ts):
    t = pl.program_id(1)

    @pl.when(t < nused_ref[0])
    def _():
        @pl.when(_expert_changed(texp_ref, t))
        def _():
            wbf_ref[...] = wd_ref[0].astype(BF16)

        def sub(j, carry):
            r0 = pl.multiple_of(j * ts, ts)
            y_ref[pl.ds(r0, ts), :] = jnp.dot(a_ref[pl.ds(r0, ts), :], wbf_ref[...],
                                              preferred_element_type=F32) + bd_ref[0]
            return carry

        lax.fori_loop(0, tns_ref[t], sub, 0)


def _moe_experts(xs, tile_exp, tile_nsub, n_used, w_gate_up, b_gate_up, w_down, b_down, *, tm, ts, tf, tn):
    n_rows, d = xs.shape
    n_exp, _, two_ff = w_gate_up.shape
    d_ff = two_ff // 2
    n_f = d_ff // tf
    n_t = n_rows // tm
    bgu = b_gate_up.reshape(n_exp, 1, two_ff)
    bdn = b_down.reshape(n_exp, 1, d)

    def tile(t, nu):
        return jnp.minimum(t, nu[0] - 1)

    up_spec = pltpu.PrefetchScalarGridSpec(
        num_scalar_prefetch=3,
        grid=(n_f, n_t),
        in_specs=[
            pl.BlockSpec((tm, d), lambda f, t, te, ns, nu: (tile(t, nu), 0)),
            pl.BlockSpec((1, d, tf), lambda f, t, te, ns, nu: (te[tile(t, nu)], 0, f)),
            pl.BlockSpec((1, d, tf), lambda f, t, te, ns, nu: (te[tile(t, nu)], 0, n_f + f)),
            pl.BlockSpec((1, 1, tf), lambda f, t, te, ns, nu: (te[tile(t, nu)], 0, f)),
            pl.BlockSpec((1, 1, tf), lambda f, t, te, ns, nu: (te[tile(t, nu)], 0, n_f + f)),
        ],
        out_specs=pl.BlockSpec((tm, tf), lambda f, t, te, ns, nu: (tile(t, nu), f)),
        scratch_shapes=[pltpu.VMEM((2, d, tf), BF16)],
    )
    act = pl.pallas_call(
        functools.partial(_moe_up_body, ts=ts),
        grid_spec=up_spec,
        out_shape=jax.ShapeDtypeStruct((n_rows, d_ff), BF16),
        compiler_params=_params(("arbitrary", "arbitrary")),
        name="moe_up",
    )(tile_exp, tile_nsub, n_used, xs, w_gate_up, w_gate_up, bgu, bgu)

    n_n = d // tn
    down_spec = pltpu.PrefetchScalarGridSpec(
        num_scalar_prefetch=3,
        grid=(n_n, n_t),
        in_specs=[
            pl.BlockSpec((tm, d_ff), lambda n, t, te, ns, nu: (tile(t, nu), 0)),
            pl.BlockSpec((1, d_ff, tn), lambda n, t, te, ns, nu: (te[tile(t, nu)], 0, n)),
            pl.BlockSpec((1, 1, tn), lambda n, t, te, ns, nu: (te[tile(t, nu)], 0, n)),
        ],
        out_specs=pl.BlockSpec((tm, tn), lambda n, t, te, ns, nu: (tile(t, nu), n)),
        scratch_shapes=[pltpu.VMEM((d_ff, tn), BF16)],
    )
    return pl.pallas_call(
        functools.partial(_moe_down_body, ts=ts),
        grid_spec=down_spec,
        out_shape=jax.ShapeDtypeStruct((n_rows, d), F32),
        compiler_params=_params(("arbitrary", "arbitrary")),
        name="moe_down",
    )(tile_exp, tile_nsub, n_used, act, w_down, bdn)


def _combine_body(dest_ref, x1_ref, gate_ref, nw_ref, yb_hbm, out_ref, rows_ref, sem, *, tt, top_k):
    i = pl.program_id(0)
    rows_per_iter = GATHER_UNROLL // top_k

    def row_copy(step, slot, r, k):
        d = dest_ref[(step * tt + r) * top_k + k]
        return pltpu.make_async_copy(yb_hbm.at[pl.ds(d, 1)], rows_ref.at[slot, k, pl.ds(r, 1)],
                                     sem.at[slot])

    def start_all(step, slot):
        def body(g, carry):
            for u in range(rows_per_iter):
                for k in range(top_k):
                    row_copy(step, slot, g * rows_per_iter + u, k).start()
            return carry
        lax.fori_loop(0, tt // rows_per_iter, body, 0)

    def wait_all(step, slot):
        def body(g, carry):
            for u in range(rows_per_iter):
                for k in range(top_k):
                    row_copy(step, slot, g * rows_per_iter + u, k).wait()
            return carry
        lax.fori_loop(0, tt // rows_per_iter, body, 0)

    @pl.when(i == 0)
    def _():
        start_all(0, 0)

    @pl.when(i + 1 < pl.num_programs(0))
    def _():
        start_all(i + 1, (i + 1) % 2)

    slot = i % 2
    wait_all(i, slot)
    gate = gate_ref[...]
    moe = rows_ref[slot, 0] * gate[:, 0:1]
    for k in range(1, top_k):
        moe = moe + rows_ref[slot, k] * gate[:, k:k + 1]
    out_ref[...] = _rms(x1_ref[...] + moe, nw_ref[...])


def _combine(dest, x1, gate, norm_w, yb, *, tt):
    n, d = x1.shape
    grid_spec = pltpu.PrefetchScalarGridSpec(
        num_scalar_prefetch=1,
        grid=(n // tt,),
        in_specs=[
            pl.BlockSpec((tt, d), lambda i, ds: (i, 0)),
            pl.BlockSpec((tt, LANES), lambda i, ds: (i, 0)),
            pl.BlockSpec((1, d), lambda i, ds: (0, 0)),
            pl.BlockSpec(memory_space=pl.ANY),
        ],
        out_specs=pl.BlockSpec((tt, d), lambda i, ds: (i, 0)),
        scratch_shapes=[pltpu.VMEM((2, TOP_K, tt, d), F32), pltpu.SemaphoreType.DMA((2,))],
    )
    return pl.pallas_call(
        functools.partial(_combine_body, tt=tt, top_k=TOP_K),
        grid_spec=grid_spec,
        out_shape=jax.ShapeDtypeStruct((n, d), F32),
        compiler_params=_params(("arbitrary",)),
        name="moe_combine",
    )(dest, x1, gate, norm_w, yb)


def _tile(n, pref):
    t = min(n, pref)
    while n % t:
        t //= 2
    return t


def kernel(x_prompt, x_sample, cache_k, cache_v, state_conv, page_table, norm_mix_w, w_in,
           lambda_q1, lambda_k1, lambda_q2, lambda_k2, subln_w, conv_dw_w, conv_dw_b,
           conv_norm_w, conv_norm_b, w_o, norm_ffn_w, w_router, b_router, w_gate_up, b_gate_up,
           w_down, b_down, norm_final_w):
    batch, seq, d = x_prompt.shape
    dec, n_new, _ = x_sample.shape
    depth, n_pool, page, n_maps, d_qk = cache_k.shape
    n_heads, d_v = cache_v.shape[3:]
    aw = n_heads * d_v
    ch = d - aw
    n_pages = page_table.shape[1]
    past = n_pages * page
    rope_dim = d_qk // 4
    half = rope_dim // 2
    scale = d_qk ** -0.5
    n_exp = w_router.shape[2]
    n_p, n_s = batch * seq, dec * n_new

    h_p = x_prompt.reshape(n_p, d)
    h_s = x_sample.reshape(n_s, d)
    tab_p = _rope_tables(jnp.arange(seq, dtype=I32), d_qk, rope_dim)
    tab_s = _rope_tables(past + jnp.arange(n_s, dtype=I32) % n_new, d_qk, rope_dim)
    outs = {k: [] for k in ("kp", "vp", "cp", "ks", "vs", "cs")}

    for l in range(depth):
        lam_init = 0.8 - 0.6 * math.exp(-0.3 * l)
        lams = tuple(v[l].reshape(1, d_qk) for v in (lambda_q1, lambda_k1, lambda_q2, lambda_k2))
        sw = subln_w[l].reshape(1, d_v)
        w_in_bf = w_in[l].astype(BF16)
        w_o_bf = w_o[l].astype(BF16)
        nmw = norm_mix_w[l].reshape(1, d)
        dw_b = conv_dw_b[l].reshape(1, ch)
        ln_w = conv_norm_w[l].reshape(1, ch)
        ln_b = conv_norm_b[l].reshape(1, ch)

        tt_p = _tile(seq, 512)
        q_p, k_p, v_p, u_p = _proj(h_p, nmw, w_in_bf, tab_p, scale=scale, half=half, tt=tt_p)
        tt_s = _tile(n_s, 512)
        q_s, k_s, v_s, u_s = _proj(h_s, nmw, w_in_bf, tab_s, scale=scale, half=half, tt=tt_s)

        o_p = _attn_prompt(q_p, k_p, v_p, lams, sw, batch=batch, seq=seq, n_heads=n_heads,
                           d_qk=d_qk, lam_init=lam_init, tq=_tile(seq, 512))
        q_rows = q_s.reshape(dec, n_new, n_maps, d_qk).transpose(0, 2, 1, 3)
        q_rows = jnp.pad(q_rows, ((0, 0), (0, 0), (0, QR - n_new), (0, 0))).reshape(dec, n_maps * QR, d_qk)
        ck = cache_k.transpose(0, 1, 3, 4, 2).reshape(depth * n_pool, n_maps * d_qk, page)
        cv = cache_v.reshape(depth * n_pool, page * n_heads, d_v)
        o_s = _attn_sample(page_table + l * n_pool, q_rows, k_s.reshape(dec, n_new * n_maps, d_qk),
                           v_s.reshape(dec, n_new * n_heads, d_v), ck, cv, lams, sw, n_new=n_new,
                           n_maps=n_maps, n_heads=n_heads, qr=QR, lam_init=lam_init,
                           n_chunk=_tile(n_pages, 16))
        o_s = o_s.reshape(n_s, aw).astype(BF16)

        c_p = _conv_prompt(u_p, conv_dw_w[l], dw_b, ln_w, ln_b, batch=batch, seq=seq,
                           tt=_tile(seq, 256))
        u_s3 = u_s.reshape(dec, n_new, ch)
        u_pad_s = jnp.concatenate([state_conv[l], u_s3], axis=1)
        c_s = _conv_sample(u_pad_s.transpose(1, 0, 2), conv_dw_w[l], dw_b, ln_w, ln_b,
                           n_new=n_new, tb=_tile(dec, 16))
        c_s = c_s.transpose(1, 0, 2).reshape(n_s, ch)

        nfw = norm_ffn_w[l].reshape(1, d)
        br = b_router[l].reshape(1, n_exp)
        assert n_p % tt_s == 0
        x1_p, xf_all, idx_p, gate_p = _wo_router(o_p, c_p, h_p, w_o_bf, nfw, w_router[l], br, tt=tt_p,
                                                 xf_rows=n_p + n_s)
        x1_s, xf_all, idx_s, gate_s = _wo_router(o_s, c_s, h_s, w_o_bf, nfw, w_router[l], br, tt=tt_s,
                                                 xf_rows=n_p + n_s, xf_row0=n_p, xf_buf=xf_all)

        idx_all = jnp.concatenate([idx_p[:, :TOP_K], idx_s[:, :TOP_K]], axis=0)
        dest, tile_exp, tile_nsub, n_used, tok_c, cs_blk, n_csub = _route_plan(
            idx_all, n_exp, MOE_TILE, MOE_SUBTILE)
        xs = _gather_rows(xf_all, tok_c, cs_blk, n_csub, n_rows=tile_exp.shape[0] * MOE_TILE,
                          ts=MOE_SUBTILE)
        yb = _moe_experts(xs, tile_exp, tile_nsub, n_used, w_gate_up[l], b_gate_up[l], w_down[l],
                          b_down[l], tm=MOE_TILE, ts=MOE_SUBTILE, tf=_tile(w_down.shape[2], 1024),
                          tn=_tile(d, 1024))

        last = l == depth - 1
        nw_out = norm_final_w.reshape(1, d) if last else jnp.ones((1, d), F32)
        assert last, "the combine kernel fuses the final norm; deeper stacks need an un-normed variant"
        h_p = _combine(dest[:n_p * TOP_K], x1_p, gate_p, nw_out, yb, tt=_tile(n_p, 256))
        h_s = _combine(dest[n_p * TOP_K:], x1_s, gate_s, nw_out, yb, tt=_tile(n_s, 256))

        outs["kp"].append(k_p.reshape(batch, seq, n_maps, d_qk))
        outs["vp"].append(v_p.reshape(batch, seq, n_heads, d_v))
        outs["cp"].append(u_p.reshape(batch, seq, ch)[:, seq - (conv_dw_w.shape[1] - 1):])
        outs["ks"].append(k_s.reshape(dec, n_new, n_maps, d_qk))
        outs["vs"].append(v_s.reshape(dec, n_new, n_heads, d_v))
        outs["cs"].append(u_pad_s[:, n_new:])

    return (h_p.reshape(batch, seq, d), h_s.reshape(dec, n_new, d),
            jnp.stack(outs["kp"]), jnp.stack(outs["vp"]), jnp.stack(outs["cp"]),
            jnp.stack(outs["ks"]), jnp.stack(outs["vs"]), jnp.stack(outs["cs"]))
```

```python
import functools
import math

import jax
import jax.numpy as jnp
from jax import lax
from jax.experimental import pallas as pl
from jax.experimental.pallas import tpu as pltpu

F32 = jnp.float32
BF16 = jnp.bfloat16
I32 = jnp.int32

EPS = 1e-5
ROPE_THETA = 500000.0
TOP_K = 4
SWIGLU_LIMIT = 7.0
SWIGLU_ALPHA = 1.702
LANES = 128
VMEM_LIMIT = 56 * 1024 * 1024
QR = 8
MOE_TILE = 768
MOE_SUBTILE = 256
GATHER_UNROLL = 8
GATHER_SLOTS = 3


def _params(sem, vmem=VMEM_LIMIT):
    return pltpu.CompilerParams(dimension_semantics=sem, vmem_limit_bytes=vmem)


def _nt_dot(a, b):
    return lax.dot_general(a, b, (((1,), (1,)), ((), ())), preferred_element_type=F32)


def _rms(x, w):
    ms = jnp.mean(x * x, axis=-1, keepdims=True)
    return x * lax.rsqrt(ms + EPS) * w


def _lambda(lq1, lk1, lq2, lk2, lam_init):
    a = jnp.sum(lq1 * lk1, axis=-1, keepdims=True)
    b = jnp.sum(lq2 * lk2, axis=-1, keepdims=True)
    return jnp.exp(a) - jnp.exp(b) + lam_init


def _proj_body(x_ref, nw_ref, wa_ref, wb_ref, cs_ref, s1_ref, s2_ref,
               q_ref, k_ref, v_ref, u_ref, xn_ref, *, scale, half):
    j = pl.program_id(1)

    @pl.when(j == 0)
    def _():
        xn_ref[...] = _rms(x_ref[...], nw_ref[...]).astype(BF16)

    def rope_store(dst_ref, p, mul):
        cs, s1, s2 = cs_ref[...], s1_ref[...], s2_ref[...]
        for c in range(p.shape[1] // LANES):
            seg = p[:, c * LANES:(c + 1) * LANES]
            r = (seg * cs + pltpu.roll(seg, LANES - half, 1) * s1
                 + pltpu.roll(seg, half, 1) * s2)
            if mul is not None:
                r = r * mul
            dst_ref[:, c * LANES:(c + 1) * LANES] = r.astype(dst_ref.dtype)

    @pl.when(j == 0)
    def _():
        p = jnp.dot(xn_ref[...], wa_ref[...], preferred_element_type=F32)
        rope_store(q_ref, p, scale)

    @pl.when(j == 1)
    def _():
        p = jnp.dot(xn_ref[...], wa_ref[...], preferred_element_type=F32)
        rope_store(k_ref, p, None)

    @pl.when(j == 2)
    def _():
        v_ref[...] = jnp.dot(xn_ref[...], wa_ref[...], preferred_element_type=F32)

    @pl.when(j == 3)
    def _():
        xn = xn_ref[...]
        val = jnp.dot(xn, wa_ref[...], preferred_element_type=F32)
        gate = jnp.dot(xn, wb_ref[...], preferred_element_type=F32)
        u_ref[...] = val * jax.nn.sigmoid(gate)


def _proj(x2d, norm_w, w_in_bf, tables, *, scale, half, tt):
    n, d = x2d.shape
    w = w_in_bf.shape[1] // 5
    cs, s1, s2 = tables
    n_pos_blocks = cs.shape[0] // tt
    tab_spec = pl.BlockSpec((tt, LANES), lambda i, j: (i % n_pos_blocks, 0))
    out_spec = pl.BlockSpec((tt, w), lambda i, j: (i, 0))
    return pl.pallas_call(
        functools.partial(_proj_body, scale=scale, half=half),
        grid=(n // tt, 4),
        in_specs=[
            pl.BlockSpec((tt, d), lambda i, j: (i, 0)),
            pl.BlockSpec((1, d), lambda i, j: (0, 0)),
            pl.BlockSpec((d, w), lambda i, j: (0, j)),
            pl.BlockSpec((d, w), lambda i, j: (0, 4)),
            tab_spec, tab_spec, tab_spec,
        ],
        out_specs=[out_spec, out_spec, out_spec, out_spec],
        out_shape=[
            jax.ShapeDtypeStruct((n, w), BF16),
            jax.ShapeDtypeStruct((n, w), F32),
            jax.ShapeDtypeStruct((n, w), F32),
            jax.ShapeDtypeStruct((n, w), F32),
        ],
        scratch_shapes=[pltpu.VMEM((tt, d), BF16)],
        compiler_params=_params(("arbitrary", "arbitrary")),
        name="proj",
    )(x2d, norm_w, w_in_bf, w_in_bf, cs, s1, s2)


def _rope_tables(pos, d_qk, rope_dim):
    half = rope_dim // 2
    inv = ROPE_THETA ** (-jnp.arange(0, rope_dim, 2, dtype=F32) / rope_dim)
    ang = pos.astype(F32)[:, None] * inv[None, :]
    cos, sin = jnp.cos(ang), jnp.sin(ang)
    p = pos.shape[0]
    zh = jnp.zeros((p, half), F32)
    zr = jnp.zeros((p, d_qk - rope_dim), F32)
    cs = jnp.concatenate([cos, cos, jnp.ones((p, d_qk - rope_dim), F32)], axis=-1)
    s1 = jnp.concatenate([-sin, zh, zr], axis=-1)
    s2 = jnp.concatenate([zh, sin, zr], axis=-1)
    rep = LANES // d_qk
    return tuple(jnp.tile(t, (1, rep)) for t in (cs, s1, s2))


def _attn_prompt_body(q_ref, k_ref, v_ref, lq1_ref, lk1_ref, lq2_ref, lk2_ref, sw_ref,
                      o_ref, kb_ref, vb_ref, m_ref, l_ref, acc_ref, *, tq, tk, d_qk, lam_init):
    qi = pl.program_id(2)

    @pl.when(qi == 0)
    def _():
        kb_ref[...] = k_ref[...].astype(BF16)
        vb_ref[...] = v_ref[...].astype(BF16)

    q = q_ref[...]
    lane = lax.broadcasted_iota(I32, q.shape, 1)
    zero = jnp.zeros_like(q)
    qm = (jnp.where(lane < d_qk, q, zero), jnp.where(lane >= d_qk, q, zero))
    m_ref[...] = jnp.full(m_ref.shape, -jnp.inf, F32)
    l_ref[...] = jnp.zeros(l_ref.shape, F32)
    acc_ref[...] = jnp.zeros(acc_ref.shape, F32)

    def step(c, masked):
        k0 = pl.multiple_of(c * tk, tk)
        kc = kb_ref[pl.ds(k0, tk), :]
        vc = vb_ref[pl.ds(k0, tk), :]
        for mi in range(2):
            s = _nt_dot(qm[mi], kc)
            if masked:
                row = lax.broadcasted_iota(I32, (tq, tk), 0)
                col = lax.broadcasted_iota(I32, (tq, tk), 1)
                s = jnp.where(col + c * tk <= row + qi * tq, s, -jnp.inf)
            m_p = m_ref[mi]
            m_n = jnp.maximum(m_p, jnp.max(s, axis=1, keepdims=True))
            p = jnp.exp(s - jnp.concatenate([m_n] * (tk // LANES), axis=1))
            al = jnp.exp(m_p - m_n)
            l_ref[mi] = al * l_ref[mi] + jnp.sum(p, axis=1, keepdims=True)
            acc_ref[mi] = al * acc_ref[mi] + jnp.dot(p.astype(BF16), vc, preferred_element_type=F32)
            m_ref[mi] = m_n

    def full_step(c, carry):
        step(c, False)
        return carry

    n_full = qi * (tq // tk)
    lax.fori_loop(0, n_full, full_step, 0)
    for dgl in range(tq // tk):
        step(n_full + dgl, True)

    lam = _lambda(lq1_ref[...], lk1_ref[...], lq2_ref[...], lk2_ref[...], lam_init)
    o = acc_ref[0] / l_ref[0] - lam * (acc_ref[1] / l_ref[1])
    o_ref[...] = (_rms(o, sw_ref[...]) * (1.0 - lam_init)).astype(o_ref.dtype)


def _attn_prompt(q, k, v, lams, subln_w, *, batch, seq, n_heads, d_qk, lam_init, tq):
    n, w = q.shape
    d_v = w // n_heads
    nq = seq // tq
    vec = pl.BlockSpec((1, d_qk), lambda b, h, i: (0, 0))
    return pl.pallas_call(
        functools.partial(_attn_prompt_body, tq=tq, tk=tq, d_qk=d_qk, lam_init=lam_init),
        grid=(batch, n_heads, nq),
        in_specs=[
            pl.BlockSpec((tq, d_v), lambda b, h, i: (b * nq + i, h)),
            pl.BlockSpec((seq, d_v), lambda b, h, i: (b, h)),
            pl.BlockSpec((seq, d_v), lambda b, h, i: (b, h)),
            vec, vec, vec, vec,
            pl.BlockSpec((1, d_v), lambda b, h, i: (0, 0)),
        ],
        out_specs=pl.BlockSpec((tq, d_v), lambda b, h, i: (b * nq + i, h)),
        out_shape=jax.ShapeDtypeStruct((n, w), BF16),
        scratch_shapes=[pltpu.VMEM((seq, d_v), BF16), pltpu.VMEM((seq, d_v), BF16),
                        pltpu.VMEM((2, tq, LANES), F32), pltpu.VMEM((2, tq, LANES), F32),
                        pltpu.VMEM((2, tq, d_v), F32)],
        compiler_params=_params(("arbitrary", "arbitrary", "arbitrary")),
        name="attn_prompt",
    )(q, k, v, *lams, subln_w)


def _attn_sample_body(pt_ref, q_ref, kn_ref, vn_ref, lq1_ref, lk1_ref, lq2_ref, lk2_ref, sw_ref,
                      *rest, n_chunk, n_new, n_maps, n_heads, qr, lam_init):
    del pt_ref
    k_refs = rest[:n_chunk]
    v_refs = rest[n_chunk:2 * n_chunk]
    o_ref = rest[2 * n_chunk]
    m_ref, l_ref, acc_ref, qbd_ref = rest[2 * n_chunk + 1:]
    c = pl.program_id(1)
    page = k_refs[0].shape[2]
    d_qk = k_refs[0].shape[1] // n_maps
    hr = 2 * qr

    @pl.when(c == 0)
    def _():
        m_ref[...] = jnp.full(m_ref.shape, -jnp.inf, F32)
        l_ref[...] = jnp.zeros(l_ref.shape, F32)
        acc_ref[...] = jnp.zeros(acc_ref.shape, F32)

    q_all = q_ref[0]

    @pl.when(c == 0)
    def _():
        q_rep = jnp.concatenate([q_all] * n_maps, axis=1)
        row_map = lax.broadcasted_iota(I32, q_rep.shape, 0) // qr
        col_map = lax.broadcasted_iota(I32, q_rep.shape, 1) // d_qk
        qbd_ref[...] = jnp.where(row_map == col_map, q_rep, jnp.zeros_like(q_rep))

    m_p, l_p, acc = m_ref[...], l_ref[...], acc_ref[...]
    k_cat = jnp.concatenate([k_refs[i][0].astype(BF16) for i in range(n_chunk)], axis=1)
    s = jnp.dot(qbd_ref[...], k_cat, preferred_element_type=F32)
    m_n = jnp.maximum(m_p, jnp.max(s, axis=1, keepdims=True))
    p = jnp.exp(s - m_n)
    al = jnp.exp(m_p - m_n)
    l_p = al * l_p + jnp.sum(p, axis=1, keepdims=True)
    pb = p.astype(BF16)
    pv = []
    for h in range(n_heads):
        vh = jnp.concatenate(
            [v_refs[i][0, pl.ds(h, page, stride=n_heads), :].astype(BF16) for i in range(n_chunk)],
            axis=0)
        pv.append(jnp.dot(pb[h * hr:(h + 1) * hr], vh, preferred_element_type=F32))
    acc = al * acc + jnp.concatenate(pv, axis=0)
    m_p = m_n
    m_ref[...] = m_p
    l_ref[...] = l_p
    acc_ref[...] = acc

    @pl.when(c == pl.num_programs(1) - 1)
    def _():
        qf = q_all.astype(F32)
        kn, vn = kn_ref[0], vn_ref[0]
        q_tok = lax.broadcasted_iota(I32, (qf.shape[0], 1), 0) % qr
        s_new = []
        for j in range(n_new):
            parts = []
            for m in range(n_maps):
                krow = kn[j * n_maps + m:j * n_maps + m + 1, :]
                parts.append(jnp.sum(qf[m * qr:(m + 1) * qr] * krow, axis=1, keepdims=True))
            s_new.append(jnp.where(q_tok >= j, jnp.concatenate(parts, axis=0), -jnp.inf))
        m_n = m_p
        for sj in s_new:
            m_n = jnp.maximum(m_n, sj)
        al = jnp.exp(m_p - m_n)
        l_n = al * l_p
        a_n = al * acc
        for j in range(n_new):
            pj = jnp.exp(s_new[j] - m_n)
            l_n = l_n + pj
            a_n = a_n + jnp.concatenate(
                [pj[h * hr:(h + 1) * hr] * vn[j * n_heads + h:j * n_heads + h + 1, :]
                 for h in range(n_heads)], axis=0)
        a_n = a_n / l_n
        lam = _lambda(lq1_ref[...], lk1_ref[...], lq2_ref[...], lk2_ref[...], lam_init)
        d_v = a_n.shape[1]
        for h in range(n_heads):
            o = a_n[h * hr:h * hr + n_new] - lam * a_n[h * hr + qr:h * hr + qr + n_new]
            o_ref[0, :, h * d_v:(h + 1) * d_v] = (
                _rms(o, sw_ref[...]) * (1.0 - lam_init)).astype(o_ref.dtype)


def _attn_sample(pages, q_rows, k_new, v_new, ck, cv, lams, subln_w, *, n_new, n_maps, n_heads, qr,
                 lam_init, n_chunk):
    dec, n_pages = pages.shape
    d_qk = ck.shape[1] // n_maps
    d_v = cv.shape[2]
    rows = n_maps * qr
    pt_flat = pages.reshape(-1)

    def page_map(i, b, c, pt):
        return (pt[b * n_pages + c * n_chunk + i], 0, 0)

    vec = pl.BlockSpec((1, d_qk), lambda b, c, pt: (0, 0))
    k_specs = [pl.BlockSpec((1,) + ck.shape[1:], functools.partial(page_map, i)) for i in range(n_chunk)]
    v_specs = [pl.BlockSpec((1,) + cv.shape[1:], functools.partial(page_map, i)) for i in range(n_chunk)]
    grid_spec = pltpu.PrefetchScalarGridSpec(
        num_scalar_prefetch=1,
        grid=(dec, n_pages // n_chunk),
        in_specs=[
            pl.BlockSpec((1, rows, d_qk), lambda b, c, pt: (b, 0, 0)),
            pl.BlockSpec((1,) + k_new.shape[1:], lambda b, c, pt: (b, 0, 0)),
            pl.BlockSpec((1,) + v_new.shape[1:], lambda b, c, pt: (b, 0, 0)),
            vec, vec, vec, vec,
            pl.BlockSpec((1, d_v), lambda b, c, pt: (0, 0)),
        ] + k_specs + v_specs,
        out_specs=pl.BlockSpec((1, n_new, n_heads * d_v), lambda b, c, pt: (b, 0, 0)),
        scratch_shapes=[pltpu.VMEM((rows, 1), F32), pltpu.VMEM((rows, 1), F32),
                        pltpu.VMEM((rows, d_v), F32), pltpu.VMEM((rows, n_maps * d_qk), BF16)],
    )
    return pl.pallas_call(
        functools.partial(_attn_sample_body, n_chunk=n_chunk, n_new=n_new, n_maps=n_maps,
                          n_heads=n_heads, qr=qr, lam_init=lam_init),
        grid_spec=grid_spec,
        out_shape=jax.ShapeDtypeStruct((dec, n_new, n_heads * d_v), F32),
        compiler_params=_params(("arbitrary", "arbitrary")),
        name="attn_sample",
    )(pt_flat, q_rows, k_new, v_new, *lams, subln_w, *([ck] * n_chunk), *([cv] * n_chunk))


def _ln_swish(y, lw, lb):
    mu = jnp.mean(y, axis=-1, keepdims=True)
    yc = y - mu
    var = jnp.mean(yc * yc, axis=-1, keepdims=True)
    z = yc * lax.rsqrt(var + EPS) * lw + lb
    return z * jax.nn.sigmoid(z)


def _conv_prompt_body(u_ref, up_ref, dw_ref, db_ref, lw_ref, lb_ref, c_ref, buf_ref, y_ref,
                      *, tt, width, halo, rb):
    t = pl.program_id(1)
    buf_ref[0:halo, :] = jnp.where(t == 0, 0.0, up_ref[...])
    buf_ref[halo:halo + tt, :] = u_ref[...]
    off = halo - (width - 1)

    def col_body(cc, carry):
        c0 = pl.multiple_of(cc * LANES, LANES)
        wts = dw_ref[:, pl.ds(c0, LANES)]
        bias = db_ref[:, pl.ds(c0, LANES)]
        for r in range(tt // rb):
            acc = jnp.zeros((rb, LANES), F32)
            for w in range(width):
                lo = r * rb + off + w
                acc = acc + buf_ref[lo:lo + rb, pl.ds(c0, LANES)] * wts[w:w + 1, :]
            y_ref[r * rb:(r + 1) * rb, pl.ds(c0, LANES)] = acc + bias
        return carry

    lax.fori_loop(0, u_ref.shape[1] // LANES, col_body, 0)
    c_ref[...] = _ln_swish(y_ref[...], lw_ref[...], lb_ref[...]).astype(c_ref.dtype)


def _conv_prompt(u, dw_w, dw_b, ln_w, ln_b, *, batch, seq, tt):
    n, ch = u.shape
    width = dw_w.shape[0]
    halo = 32
    nt = seq // tt
    per = tt // halo
    row = pl.BlockSpec((1, ch), lambda b, t: (0, 0))
    return pl.pallas_call(
        functools.partial(_conv_prompt_body, tt=tt, width=width, halo=halo, rb=64),
        grid=(batch, nt),
        in_specs=[
            pl.BlockSpec((tt, ch), lambda b, t: (b * nt + t, 0)),
            pl.BlockSpec((halo, ch), lambda b, t: (jnp.maximum((b * nt + t) * per - 1, 0), 0)),
            pl.BlockSpec((width, ch), lambda b, t: (0, 0)),
            row, row, row,
        ],
        out_specs=pl.BlockSpec((tt, ch), lambda b, t: (b * nt + t, 0)),
        out_shape=jax.ShapeDtypeStruct((n, ch), BF16),
        scratch_shapes=[pltpu.VMEM((halo + tt, ch), F32), pltpu.VMEM((tt, ch), F32)],
        compiler_params=_params(("arbitrary", "arbitrary")),
        name="conv_prompt",
    )(u, u, dw_w, dw_b, ln_w, ln_b)


def _conv_sample_body(up_ref, dw_ref, db_ref, lw_ref, lb_ref, c_ref, y_ref, *, width, n_new):
    length, tb, ch = up_ref.shape

    def col_body(cc, carry):
        c0 = pl.multiple_of(cc * LANES, LANES)
        wts = dw_ref[:, pl.ds(c0, LANES)]
        bias = db_ref[:, pl.ds(c0, LANES)]
        accs = [jnp.zeros((tb, LANES), F32) for _ in range(n_new)]
        for j in range(length):
            x = up_ref[j, :, pl.ds(c0, LANES)]
            for t in range(n_new):
                w = j - t
                if 0 <= w < width:
                    accs[t] = accs[t] + x * wts[w:w + 1, :]
        for t in range(n_new):
            y_ref[t, :, pl.ds(c0, LANES)] = accs[t] + bias
        return carry

    lax.fori_loop(0, ch // LANES, col_body, 0)
    c_ref[...] = _ln_swish(y_ref[...], lw_ref[...], lb_ref[...]).astype(c_ref.dtype)


def _conv_sample(u_pad_t, dw_w, dw_b, ln_w, ln_b, *, n_new, tb):
    length, dec, ch = u_pad_t.shape
    width = dw_w.shape[0]
    row = pl.BlockSpec((1, ch), lambda b: (0, 0))
    return pl.pallas_call(
        functools.partial(_conv_sample_body, width=width, n_new=n_new),
        grid=(dec // tb,),
        in_specs=[
            pl.BlockSpec((length, tb, ch), lambda b: (0, b, 0)),
            pl.BlockSpec((width, ch), lambda b: (0, 0)),
            row, row, row,
        ],
        out_specs=pl.BlockSpec((n_new, tb, ch), lambda b: (0, b, 0)),
        out_shape=jax.ShapeDtypeStruct((n_new, dec, ch), BF16),
        scratch_shapes=[pltpu.VMEM((n_new, tb, ch), F32)],
        compiler_params=_params(("arbitrary",)),
        name="conv_sample",
    )(u_pad_t, dw_w, dw_b, ln_w, ln_b)


def _wo_body(o_ref, c_ref, x_ref, wo_ref, nw_ref, wr_ref, br_ref, *rest, aw, top_k):
    x1_ref, xf_ref, idx_ref, gate_ref = rest[-4:]
    attn = (jnp.dot(o_ref[...], wo_ref[0:aw, :], preferred_element_type=F32)
            + jnp.dot(c_ref[...], wo_ref[aw:, :], preferred_element_type=F32))
    x1 = x_ref[...] + attn
    x1_ref[...] = x1
    xf = _rms(x1, nw_ref[...])
    xf_ref[...] = xf
    xh = xf.astype(BF16)
    xl = (xf - xh.astype(F32)).astype(BF16)
    wr = wr_ref[...]
    wh = wr.astype(BF16)
    wl = (wr - wh.astype(F32)).astype(BF16)
    logits = (jnp.dot(xh, wh, preferred_element_type=F32) + jnp.dot(xl, wh, preferred_element_type=F32)
              + jnp.dot(xh, wl, preferred_element_type=F32)) + br_ref[...]
    n_exp = logits.shape[1]
    lane_e = lax.broadcasted_iota(I32, logits.shape, 1)
    lane_o = lax.broadcasted_iota(I32, idx_ref.shape, 1)
    vals = logits
    tops, idxs = [], []
    for _ in range(top_k):
        mx = jnp.max(vals, axis=1, keepdims=True)
        am = jnp.min(jnp.where(vals == mx, lane_e, n_exp), axis=1, keepdims=True)
        tops.append(mx)
        idxs.append(am)
        vals = jnp.where(lane_e == am, -jnp.inf, vals)
    es = [jnp.exp(t - tops[0]) for t in tops]
    den = es[0]
    for e in es[1:]:
        den = den + e
    idx_out = jnp.zeros(idx_ref.shape, I32)
    gate_out = jnp.zeros(gate_ref.shape, F32)
    for k in range(top_k):
        idx_out = jnp.where(lane_o == k, idxs[k], idx_out)
        gate_out = jnp.where(lane_o == k, es[k] / den, gate_out)
    idx_ref[...] = idx_out
    gate_ref[...] = gate_out


def _wo_router(o, c, x2d, w_o_bf, norm_w, w_router, b_router, *, tt, xf_rows, xf_row0=0, xf_buf=None):
    n, d = x2d.shape
    aw = o.shape[1]
    n_exp = w_router.shape[1]
    blk0 = xf_row0 // tt
    row = pl.BlockSpec((tt, d), lambda i: (i, 0))
    half = pl.BlockSpec((tt, aw), lambda i: (i, 0))
    small = pl.BlockSpec((tt, LANES), lambda i: (i, 0))
    in_specs = [
        half, pl.BlockSpec((tt, d - aw), lambda i: (i, 0)), row,
        pl.BlockSpec((d, d), lambda i: (0, 0)),
        pl.BlockSpec((1, d), lambda i: (0, 0)),
        pl.BlockSpec((d, n_exp), lambda i: (0, 0)),
        pl.BlockSpec((1, n_exp), lambda i: (0, 0)),
    ]
    args = [o, c, x2d, w_o_bf, norm_w, w_router, b_router]
    aliases = {}
    if xf_buf is not None:
        in_specs.append(pl.BlockSpec(memory_space=pl.ANY))
        args.append(xf_buf)
        aliases = {len(args) - 1: 1}
    return pl.pallas_call(
        functools.partial(_wo_body, aw=aw, top_k=TOP_K),
        grid=(n // tt,),
        in_specs=in_specs,
        out_specs=[row, pl.BlockSpec((tt, d), lambda i: (blk0 + i, 0)), small, small],
        out_shape=[
            jax.ShapeDtypeStruct((n, d), F32),
            jax.ShapeDtypeStruct((xf_rows, d), F32),
            jax.ShapeDtypeStruct((n, LANES), I32),
            jax.ShapeDtypeStruct((n, LANES), F32),
        ],
        input_output_aliases=aliases,
        compiler_params=_params(("arbitrary",)),
        name="wo_router",
    )(*args)


def _route_plan(idx, n_exp, tm, ts):
    n_tok, top_k = idx.shape
    n_asg = n_tok * top_k
    nsub = tm // ts
    flat_e = idx.reshape(-1)
    onehot = (flat_e[:, None] == jnp.arange(n_exp, dtype=I32)[None, :]).astype(I32)
    csum = jnp.cumsum(onehot, axis=0)
    rank = jnp.sum((csum - onehot) * onehot, axis=1)
    counts = csum[-1]
    ntile = (counts + tm - 1) // tm
    first_rows = counts - jnp.maximum(ntile - 1, 0) * tm
    tile_end = jnp.cumsum(ntile)
    tile_beg = tile_end - ntile
    fr = first_rows[flat_e]
    dest = (tile_beg[flat_e] * tm + jnp.where(rank < fr, rank, rank - fr + tm)).astype(I32)

    n_tiles = -(-n_asg // tm) + n_exp
    t_ids = jnp.arange(n_tiles, dtype=I32)
    tile_exp = jnp.minimum(jnp.sum(t_ids[:, None] >= tile_end[None, :], axis=1), n_exp - 1).astype(I32)
    n_used = tile_end[-1]
    first_sub = (first_rows[tile_exp] + ts - 1) // ts
    tile_nsub = jnp.where(t_ids < n_used, jnp.where(t_ids == tile_beg[tile_exp], first_sub, nsub), 0)
    tile_nsub = tile_nsub.astype(I32)

    n_cs = -(-n_asg // ts) + n_exp
    sub_end = jnp.cumsum(tile_nsub)
    n_csub = sub_end[-1]
    i_ids = jnp.minimum(jnp.arange(n_cs, dtype=I32), n_csub - 1)
    cs_tile = jnp.minimum(jnp.sum(i_ids[:, None] >= sub_end[None, :], axis=1), n_tiles - 1)
    cs_blk = (cs_tile * nsub + i_ids - (sub_end - tile_nsub)[cs_tile]).astype(I32)
    flat_tok = jnp.arange(n_asg, dtype=I32) // top_k
    buf_tok = jnp.zeros((n_tiles * tm,), I32).at[dest].set(flat_tok, unique_indices=True)
    tok_c = buf_tok.reshape(n_tiles * nsub, ts)[cs_blk].reshape(-1)
    return (dest, tile_exp, tile_nsub, n_used.astype(I32).reshape(1), tok_c, cs_blk,
            n_csub.astype(I32).reshape(1))


def _gather_body(tok_ref, blk_ref, ncs_ref, x_hbm, xs_ref, land_ref, sem, *, ts):
    del blk_ref
    i = pl.program_id(0)
    n = ncs_ref[0]

    def row_copy(step, slot, r):
        tok = tok_ref[step * ts + r]
        return pltpu.make_async_copy(x_hbm.at[pl.ds(tok, 1)], land_ref.at[slot, pl.ds(r, 1)],
                                     sem.at[slot])

    def start_all(step, slot):
        def body(g, carry):
            for u in range(GATHER_UNROLL):
                row_copy(step, slot, u * (ts // GATHER_UNROLL) + g).start()
            return carry
        lax.fori_loop(0, ts // GATHER_UNROLL, body, 0)

    def wait_all(step, slot):
        def body(g, carry):
            for u in range(GATHER_UNROLL):
                row_copy(step, slot, g * GATHER_UNROLL + u).wait()
            return carry
        lax.fori_loop(0, ts // GATHER_UNROLL, body, 0)

    ahead = GATHER_SLOTS - 1

    @pl.when(i == 0)
    def _():
        for s in range(ahead):
            @pl.when(s < n)
            def _():
                start_all(s, s)

    @pl.when(i + ahead < n)
    def _():
        start_all(i + ahead, (i + ahead) % GATHER_SLOTS)

    @pl.when(i < n)
    def _():
        wait_all(i, i % GATHER_SLOTS)
        xs_ref[...] = land_ref[i % GATHER_SLOTS].astype(xs_ref.dtype)


def _gather_rows(xf, tok_c, cs_blk, n_csub, *, n_rows, ts):
    d = xf.shape[1]
    n_cs = cs_blk.shape[0]
    grid_spec = pltpu.PrefetchScalarGridSpec(
        num_scalar_prefetch=3,
        grid=(n_cs,),
        in_specs=[pl.BlockSpec(memory_space=pl.ANY)],
        out_specs=pl.BlockSpec((ts, d), lambda i, tok, blk, ncs: (blk[i], 0)),
        scratch_shapes=[pltpu.VMEM((GATHER_SLOTS, ts, d), xf.dtype),
                        pltpu.SemaphoreType.DMA((GATHER_SLOTS,))],
    )
    return pl.pallas_call(
        functools.partial(_gather_body, ts=ts),
        grid_spec=grid_spec,
        out_shape=jax.ShapeDtypeStruct((n_rows, d), BF16),
        compiler_params=_params(("arbitrary",)),
        name="moe_gather",
    )(tok_c, cs_blk, n_csub, xf)


def _expert_changed(texp_ref, t):
    return (t == 0) | (texp_ref[t] != texp_ref[jnp.maximum(t - 1, 0)])


def _moe_up_body(texp_ref, tns_ref, nused_ref, x_ref, wg_ref, wu_ref, bg_ref, bu_ref, a_ref, wbf_ref,
                 *, ts):
    t = pl.program_id(1)

    @pl.when(t < nused_ref[0])
    def _():
        @pl.when(_expert_changed(texp_ref, t))
        def _():
            wbf_ref[0] = wg_ref[0].astype(BF16)
            wbf_ref[1] = wu_ref[0].astype(BF16)

        def sub(j, carry):
            r0 = pl.multiple_of(j * ts, ts)
            x = x_ref[pl.ds(r0, ts), :]
            g = jnp.dot(x, wbf_ref[0], preferred_element_type=F32) + bg_ref[0]
            lin = jnp.dot(x, wbf_ref[1], preferred_element_type=F32) + bu_ref[0]
            g = jnp.minimum(g, SWIGLU_LIMIT)
            lin = jnp.clip(lin, -SWIGLU_LIMIT, SWIGLU_LIMIT)
            a = g * jax.nn.sigmoid(SWIGLU_ALPHA * g) * (lin + 1.0)
            a_ref[pl.ds(r0, ts), :] = a.astype(a_ref.dtype)
            return carry

        lax.fori_loop(0, tns_ref[t], sub, 0)


def _moe_down_body(texp_ref, tns_ref, nused_ref, a_ref, wd_ref, bd_ref, y_ref, wbf_ref, *, ts):
    t = pl.program_id(1)

    @pl.when(t < nused_ref[0])
    def _():
        @pl.when(_expert_changed(texp_ref, t))
        def _():
            wbf_ref[...] = wd_ref[0].astype(BF16)

        def sub(j, carry):
            r0 = pl.multiple_of(j * ts, ts)
            y_ref[pl.ds(r0, ts), :] = jnp.dot(a_ref[pl.ds(r0, ts), :], wbf_ref[...],
                                              preferred_element_type=F32) + bd_ref[0]
            return carry

        lax.fori_loop(0, tns_ref[t], sub, 0)


def _moe_experts(xs, tile_exp, tile_nsub, n_used, w_gate_up, b_gate_up, w_down, b_down, *, tm, ts, tf, tn):
    n_rows, d = xs.shape
    n_exp, _, two_ff = w_gate_up.shape
    d_ff = two_ff // 2
    n_f = d_ff // tf
    n_t = n_rows // tm
    bgu = b_gate_up.reshape(n_exp, 1, two_ff)
    bdn = b_down.reshape(n_exp, 1, d)

    def tile(t, nu):
        return jnp.minimum(t, nu[0] - 1)

    up_spec = pltpu.PrefetchScalarGridSpec(
        num_scalar_prefetch=3,
        grid=(n_f, n_t),
        in_specs=[
            pl.BlockSpec((tm, d), lambda f, t, te, ns, nu: (tile(t, nu), 0)),
            pl.BlockSpec((1, d, tf), lambda f, t, te, ns, nu: (te[tile(t, nu)], 0, f)),
            pl.BlockSpec((1, d, tf), lambda f, t, te, ns, nu: (te[tile(t, nu)], 0, n_f + f)),
            pl.BlockSpec((1, 1, tf), lambda f, t, te, ns, nu: (te[tile(t, nu)], 0, f)),
            pl.BlockSpec((1, 1, tf), lambda f, t, te, ns, nu: (te[tile(t, nu)], 0, n_f + f)),
        ],
        out_specs=pl.BlockSpec((tm, tf), lambda f, t, te, ns, nu: (tile(t, nu), f)),
        scratch_shapes=[pltpu.VMEM((2, d, tf), BF16)],
    )
    act = pl.pallas_call(
        functools.partial(_moe_up_body, ts=ts),
        grid_spec=up_spec,
        out_shape=jax.ShapeDtypeStruct((n_rows, d_ff), BF16),
        compiler_params=_params(("arbitrary", "arbitrary")),
        name="moe_up",
    )(tile_exp, tile_nsub, n_used, xs, w_gate_up, w_gate_up, bgu, bgu)

    n_n = d // tn
    down_spec = pltpu.PrefetchScalarGridSpec(
        num_scalar_prefetch=3,
        grid=(n_n, n_t),
        in_specs=[
            pl.BlockSpec((tm, d_ff), lambda n, t, te, ns, nu: (tile(t, nu), 0)),
            pl.BlockSpec((1, d_ff, tn), lambda n, t, te, ns, nu: (te[tile(t, nu)], 0, n)),
            pl.BlockSpec((1, 1, tn), lambda n, t, te, ns, nu: (te[tile(t, nu)], 0, n)),
        ],
        out_specs=pl.BlockSpec((tm, tn), lambda n, t, te, ns, nu: (tile(t, nu), n)),
        scratch_shapes=[pltpu.VMEM((d_ff, tn), BF16)],
    )
    return pl.pallas_call(
        functools.partial(_moe_down_body, ts=ts),
        grid_spec=down_spec,
        out_shape=jax.ShapeDtypeStruct((n_rows, d), F32),
        compiler_params=_params(("arbitrary", "arbitrary")),
        name="moe_down",
    )(tile_exp, tile_nsub, n_used, act, w_down, bdn)


def _combine_body(dest_ref, x1_ref, gate_ref, nw_ref, yb_hbm, out_ref, rows_ref, sem, *, tt, top_k):
    i = pl.program_id(0)

    def row_copy(r, k):
        d = dest_ref[(i * tt + r) * top_k + k]
        return pltpu.make_async_copy(yb_hbm.at[pl.ds(d, 1)], rows_ref.at[k, pl.ds(r, 1)], sem)

    rows_per_iter = GATHER_UNROLL // top_k

    def issue(g, carry):
        for u in range(rows_per_iter):
            for k in range(top_k):
                row_copy(g * rows_per_iter + u, k).start()
        return carry

    def drain(g, carry):
        for u in range(rows_per_iter):
            for k in range(top_k):
                row_copy(g * rows_per_iter + u, k).wait()
        return carry

    lax.fori_loop(0, tt // rows_per_iter, issue, 0)
    lax.fori_loop(0, tt // rows_per_iter, drain, 0)
    gate = gate_ref[...]
    moe = rows_ref[0] * gate[:, 0:1]
    for k in range(1, top_k):
        moe = moe + rows_ref[k] * gate[:, k:k + 1]
    out_ref[...] = _rms(x1_ref[...] + moe, nw_ref[...])


def _combine(dest, x1, gate, norm_w, yb, *, tt):
    n, d = x1.shape
    grid_spec = pltpu.PrefetchScalarGridSpec(
        num_scalar_prefetch=1,
        grid=(n // tt,),
        in_specs=[
            pl.BlockSpec((tt, d), lambda i, ds: (i, 0)),
            pl.BlockSpec((tt, LANES), lambda i, ds: (i, 0)),
            pl.BlockSpec((1, d), lambda i, ds: (0, 0)),
            pl.BlockSpec(memory_space=pl.ANY),
        ],
        out_specs=pl.BlockSpec((tt, d), lambda i, ds: (i, 0)),
        scratch_shapes=[pltpu.VMEM((TOP_K, tt, d), F32), pltpu.SemaphoreType.DMA],
    )
    return pl.pallas_call(
        functools.partial(_combine_body, tt=tt, top_k=TOP_K),
        grid_spec=grid_spec,
        out_shape=jax.ShapeDtypeStruct((n, d), F32),
        compiler_params=_params(("arbitrary",)),
        name="moe_combine",
    )(dest, x1, gate, norm_w, yb)


def _tile(n, pref):
    t = min(n, pref)
    while n % t:
        t //= 2
    return t


def kernel(x_prompt, x_sample, cache_k, cache_v, state_conv, page_table, norm_mix_w, w_in,
           lambda_q1, lambda_k1, lambda_q2, lambda_k2, subln_w, conv_dw_w, conv_dw_b,
           conv_norm_w, conv_norm_b, w_o, norm_ffn_w, w_router, b_router, w_gate_up, b_gate_up,
           w_down, b_down, norm_final_w):
    batch, seq, d = x_prompt.shape
    dec, n_new, _ = x_sample.shape
    depth, n_pool, page, n_maps, d_qk = cache_k.shape
    n_heads, d_v = cache_v.shape[3:]
    aw = n_heads * d_v
    ch = d - aw
    n_pages = page_table.shape[1]
    past = n_pages * page
    rope_dim = d_qk // 4
    half = rope_dim // 2
    scale = d_qk ** -0.5
    n_exp = w_router.shape[2]
    n_p, n_s = batch * seq, dec * n_new

    h_p = x_prompt.reshape(n_p, d)
    h_s = x_sample.reshape(n_s, d)
    tab_p = _rope_tables(jnp.arange(seq, dtype=I32), d_qk, rope_dim)
    tab_s = _rope_tables(past + jnp.arange(n_s, dtype=I32) % n_new, d_qk, rope_dim)
    outs = {k: [] for k in ("kp", "vp", "cp", "ks", "vs", "cs")}

    for l in range(depth):
        lam_init = 0.8 - 0.6 * math.exp(-0.3 * l)
        lams = tuple(v[l].reshape(1, d_qk) for v in (lambda_q1, lambda_k1, lambda_q2, lambda_k2))
        sw = subln_w[l].reshape(1, d_v)
        w_in_bf = w_in[l].astype(BF16)
        w_o_bf = w_o[l].astype(BF16)
        nmw = norm_mix_w[l].reshape(1, d)
        dw_b = conv_dw_b[l].reshape(1, ch)
        ln_w = conv_norm_w[l].reshape(1, ch)
        ln_b = conv_norm_b[l].reshape(1, ch)

        tt_p = _tile(seq, 512)
        q_p, k_p, v_p, u_p = _proj(h_p, nmw, w_in_bf, tab_p, scale=scale, half=half, tt=tt_p)
        tt_s = _tile(n_s, 512)
        q_s, k_s, v_s, u_s = _proj(h_s, nmw, w_in_bf, tab_s, scale=scale, half=half, tt=tt_s)

        o_p = _attn_prompt(q_p, k_p, v_p, lams, sw, batch=batch, seq=seq, n_heads=n_heads,
                           d_qk=d_qk, lam_init=lam_init, tq=_tile(seq, 512))
        q_rows = q_s.reshape(dec, n_new, n_maps, d_qk).transpose(0, 2, 1, 3)
        q_rows = jnp.pad(q_rows, ((0, 0), (0, 0), (0, QR - n_new), (0, 0))).reshape(dec, n_maps * QR, d_qk)
        ck = cache_k.transpose(0, 1, 3, 4, 2).reshape(depth * n_pool, n_maps * d_qk, page)
        cv = cache_v.reshape(depth * n_pool, page * n_heads, d_v)
        o_s = _attn_sample(page_table + l * n_pool, q_rows, k_s.reshape(dec, n_new * n_maps, d_qk),
                           v_s.reshape(dec, n_new * n_heads, d_v), ck, cv, lams, sw, n_new=n_new,
                           n_maps=n_maps, n_heads=n_heads, qr=QR, lam_init=lam_init,
                           n_chunk=_tile(n_pages, 16))
        o_s = o_s.reshape(n_s, aw).astype(BF16)

        c_p = _conv_prompt(u_p, conv_dw_w[l], dw_b, ln_w, ln_b, batch=batch, seq=seq,
                           tt=_tile(seq, 256))
        u_s3 = u_s.reshape(dec, n_new, ch)
        u_pad_s = jnp.concatenate([state_conv[l], u_s3], axis=1)
        c_s = _conv_sample(u_pad_s.transpose(1, 0, 2), conv_dw_w[l], dw_b, ln_w, ln_b,
                           n_new=n_new, tb=_tile(dec, 16))
        c_s = c_s.transpose(1, 0, 2).reshape(n_s, ch)

        nfw = norm_ffn_w[l].reshape(1, d)
        br = b_router[l].reshape(1, n_exp)
        assert n_p % tt_s == 0
        x1_p, xf_all, idx_p, gate_p = _wo_router(o_p, c_p, h_p, w_o_bf, nfw, w_router[l], br, tt=tt_p,
                                                 xf_rows=n_p + n_s)
        x1_s, xf_all, idx_s, gate_s = _wo_router(o_s, c_s, h_s, w_o_bf, nfw, w_router[l], br, tt=tt_s,
                                                 xf_rows=n_p + n_s, xf_row0=n_p, xf_buf=xf_all)

        idx_all = jnp.concatenate([idx_p[:, :TOP_K], idx_s[:, :TOP_K]], axis=0)
        dest, tile_exp, tile_nsub, n_used, tok_c, cs_blk, n_csub = _route_plan(
            idx_all, n_exp, MOE_TILE, MOE_SUBTILE)
        xs = _gather_rows(xf_all, tok_c, cs_blk, n_csub, n_rows=tile_exp.shape[0] * MOE_TILE,
                          ts=MOE_SUBTILE)
        yb = _moe_experts(xs, tile_exp, tile_nsub, n_used, w_gate_up[l], b_gate_up[l], w_down[l],
                          b_down[l], tm=MOE_TILE, ts=MOE_SUBTILE, tf=_tile(w_down.shape[2], 1024),
                          tn=_tile(d, 1024))

        last = l == depth - 1
        nw_out = norm_final_w.reshape(1, d) if last else jnp.ones((1, d), F32)
        assert last, "the combine kernel fuses the final norm; deeper stacks need an un-normed variant"
        h_p = _combine(dest[:n_p * TOP_K], x1_p, gate_p, nw_out, yb, tt=_tile(n_p, 256))
        h_s = _combine(dest[n_p * TOP_K:], x1_s, gate_s, nw_out, yb, tt=_tile(n_s, 256))

        outs["kp"].append(k_p.reshape(batch, seq, n_maps, d_qk))
        outs["vp"].append(v_p.reshape(batch, seq, n_heads, d_v))
        outs["cp"].append(u_p.reshape(batch, seq, ch)[:, seq - (conv_dw_w.shape[1] - 1):])
        outs["ks"].append(k_s.reshape(dec, n_new, n_maps, d_qk))
        outs["vs"].append(v_s.reshape(dec, n_new, n_heads, d_v))
        outs["cs"].append(u_pad_s[:, n_new:])

    return (h_p.reshape(batch, seq, d), h_s.reshape(dec, n_new, d),
            jnp.stack(outs["kp"]), jnp.stack(outs["vp"]), jnp.stack(outs["cp"]),
            jnp.stack(outs["ks"]), jnp.stack(outs["vs"]), jnp.stack(outs["cs"]))
```

```python
import functools
import math

import jax
import jax.numpy as jnp
from jax import lax
from jax.experimental import pallas as pl
from jax.experimental.pallas import tpu as pltpu

F32 = jnp.float32
BF16 = jnp.bfloat16
I32 = jnp.int32

EPS = 1e-5
ROPE_THETA = 500000.0
TOP_K = 4
SWIGLU_LIMIT = 7.0
SWIGLU_ALPHA = 1.702
LANES = 128
VMEM_LIMIT = 56 * 1024 * 1024
QR = 8
MOE_TILE = 768
MOE_SUBTILE = 256
GATHER_UNROLL = 8
GATHER_SLOTS = 3


def _params(sem, vmem=VMEM_LIMIT):
    return pltpu.CompilerParams(dimension_semantics=sem, vmem_limit_bytes=vmem)


def _nt_dot(a, b):
    return lax.dot_general(a, b, (((1,), (1,)), ((), ())), preferred_element_type=F32)


def _rms(x, w):
    ms = jnp.mean(x * x, axis=-1, keepdims=True)
    return x * lax.rsqrt(ms + EPS) * w


def _lambda(lq1, lk1, lq2, lk2, lam_init):
    a = jnp.sum(lq1 * lk1, axis=-1, keepdims=True)
    b = jnp.sum(lq2 * lk2, axis=-1, keepdims=True)
    return jnp.exp(a) - jnp.exp(b) + lam_init


def _proj_body(x_ref, nw_ref, wa_ref, wb_ref, cs_ref, s1_ref, s2_ref,
               q_ref, k_ref, v_ref, u_ref, xn_ref, *, scale, half):
    j = pl.program_id(1)

    @pl.when(j == 0)
    def _():
        xn_ref[...] = _rms(x_ref[...], nw_ref[...]).astype(BF16)

    def rope_store(dst_ref, p, mul):
        cs, s1, s2 = cs_ref[...], s1_ref[...], s2_ref[...]
        for c in range(p.shape[1] // LANES):
            seg = p[:, c * LANES:(c + 1) * LANES]
            r = (seg * cs + pltpu.roll(seg, LANES - half, 1) * s1
                 + pltpu.roll(seg, half, 1) * s2)
            if mul is not None:
                r = r * mul
            dst_ref[:, c * LANES:(c + 1) * LANES] = r.astype(dst_ref.dtype)

    @pl.when(j == 0)
    def _():
        p = jnp.dot(xn_ref[...], wa_ref[...], preferred_element_type=F32)
        rope_store(q_ref, p, scale)

    @pl.when(j == 1)
    def _():
        p = jnp.dot(xn_ref[...], wa_ref[...], preferred_element_type=F32)
        rope_store(k_ref, p, None)

    @pl.when(j == 2)
    def _():
        v_ref[...] = jnp.dot(xn_ref[...], wa_ref[...], preferred_element_type=F32)

    @pl.when(j == 3)
    def _():
        xn = xn_ref[...]
        val = jnp.dot(xn, wa_ref[...], preferred_element_type=F32)
        gate = jnp.dot(xn, wb_ref[...], preferred_element_type=F32)
        u_ref[...] = val * jax.nn.sigmoid(gate)


def _proj(x2d, norm_w, w_in_bf, tables, *, scale, half, tt):
    n, d = x2d.shape
    w = w_in_bf.shape[1] // 5
    cs, s1, s2 = tables
    n_pos_blocks = cs.shape[0] // tt
    tab_spec = pl.BlockSpec((tt, LANES), lambda i, j: (i % n_pos_blocks, 0))
    out_spec = pl.BlockSpec((tt, w), lambda i, j: (i, 0))
    return pl.pallas_call(
        functools.partial(_proj_body, scale=scale, half=half),
        grid=(n // tt, 4),
        in_specs=[
            pl.BlockSpec((tt, d), lambda i, j: (i, 0)),
            pl.BlockSpec((1, d), lambda i, j: (0, 0)),
            pl.BlockSpec((d, w), lambda i, j: (0, j)),
            pl.BlockSpec((d, w), lambda i, j: (0, 4)),
            tab_spec, tab_spec, tab_spec,
        ],
        out_specs=[out_spec, out_spec, out_spec, out_spec],
        out_shape=[
            jax.ShapeDtypeStruct((n, w), BF16),
            jax.ShapeDtypeStruct((n, w), F32),
            jax.ShapeDtypeStruct((n, w), F32),
            jax.ShapeDtypeStruct((n, w), F32),
        ],
        scratch_shapes=[pltpu.VMEM((tt, d), BF16)],
        compiler_params=_params(("arbitrary", "arbitrary")),
        name="proj",
    )(x2d, norm_w, w_in_bf, w_in_bf, cs, s1, s2)


def _rope_tables(pos, d_qk, rope_dim):
    half = rope_dim // 2
    inv = ROPE_THETA ** (-jnp.arange(0, rope_dim, 2, dtype=F32) / rope_dim)
    ang = pos.astype(F32)[:, None] * inv[None, :]
    cos, sin = jnp.cos(ang), jnp.sin(ang)
    p = pos.shape[0]
    zh = jnp.zeros((p, half), F32)
    zr = jnp.zeros((p, d_qk - rope_dim), F32)
    cs = jnp.concatenate([cos, cos, jnp.ones((p, d_qk - rope_dim), F32)], axis=-1)
    s1 = jnp.concatenate([-sin, zh, zr], axis=-1)
    s2 = jnp.concatenate([zh, sin, zr], axis=-1)
    rep = LANES // d_qk
    return tuple(jnp.tile(t, (1, rep)) for t in (cs, s1, s2))


def _attn_prompt_body(q_ref, k_ref, v_ref, lq1_ref, lk1_ref, lq2_ref, lk2_ref, sw_ref,
                      o_ref, kb_ref, vb_ref, m_ref, l_ref, acc_ref, *, tq, tk, d_qk, lam_init):
    qi = pl.program_id(2)

    @pl.when(qi == 0)
    def _():
        kb_ref[...] = k_ref[...].astype(BF16)
        vb_ref[...] = v_ref[...].astype(BF16)

    q = q_ref[...]
    lane = lax.broadcasted_iota(I32, q.shape, 1)
    zero = jnp.zeros_like(q)
    qm = (jnp.where(lane < d_qk, q, zero), jnp.where(lane >= d_qk, q, zero))
    m_ref[...] = jnp.full(m_ref.shape, -jnp.inf, F32)
    l_ref[...] = jnp.zeros(l_ref.shape, F32)
    acc_ref[...] = jnp.zeros(acc_ref.shape, F32)

    def step(c, masked):
        k0 = pl.multiple_of(c * tk, tk)
        kc = kb_ref[pl.ds(k0, tk), :]
        vc = vb_ref[pl.ds(k0, tk), :]
        for mi in range(2):
            s = _nt_dot(qm[mi], kc)
            if masked:
                row = lax.broadcasted_iota(I32, (tq, tk), 0)
                col = lax.broadcasted_iota(I32, (tq, tk), 1)
                s = jnp.where(col + c * tk <= row + qi * tq, s, -jnp.inf)
            m_p = m_ref[mi]
            m_n = jnp.maximum(m_p, jnp.max(s, axis=1, keepdims=True))
            p = jnp.exp(s - jnp.concatenate([m_n] * (tk // LANES), axis=1))
            al = jnp.exp(m_p - m_n)
            l_ref[mi] = al * l_ref[mi] + jnp.sum(p, axis=1, keepdims=True)
            acc_ref[mi] = al * acc_ref[mi] + jnp.dot(p.astype(BF16), vc, preferred_element_type=F32)
            m_ref[mi] = m_n

    def full_step(c, carry):
        step(c, False)
        return carry

    n_full = qi * (tq // tk)
    lax.fori_loop(0, n_full, full_step, 0)
    for dgl in range(tq // tk):
        step(n_full + dgl, True)

    lam = _lambda(lq1_ref[...], lk1_ref[...], lq2_ref[...], lk2_ref[...], lam_init)
    o = acc_ref[0] / l_ref[0] - lam * (acc_ref[1] / l_ref[1])
    o_ref[...] = (_rms(o, sw_ref[...]) * (1.0 - lam_init)).astype(o_ref.dtype)


def _attn_prompt(q, k, v, lams, subln_w, *, batch, seq, n_heads, d_qk, lam_init, tq):
    n, w = q.shape
    d_v = w // n_heads
    nq = seq // tq
    vec = pl.BlockSpec((1, d_qk), lambda b, h, i: (0, 0))
    return pl.pallas_call(
        functools.partial(_attn_prompt_body, tq=tq, tk=tq, d_qk=d_qk, lam_init=lam_init),
        grid=(batch, n_heads, nq),
        in_specs=[
            pl.BlockSpec((tq, d_v), lambda b, h, i: (b * nq + i, h)),
            pl.BlockSpec((seq, d_v), lambda b, h, i: (b, h)),
            pl.BlockSpec((seq, d_v), lambda b, h, i: (b, h)),
            vec, vec, vec, vec,
            pl.BlockSpec((1, d_v), lambda b, h, i: (0, 0)),
        ],
        out_specs=pl.BlockSpec((tq, d_v), lambda b, h, i: (b * nq + i, h)),
        out_shape=jax.ShapeDtypeStruct((n, w), BF16),
        scratch_shapes=[pltpu.VMEM((seq, d_v), BF16), pltpu.VMEM((seq, d_v), BF16),
                        pltpu.VMEM((2, tq, LANES), F32), pltpu.VMEM((2, tq, LANES), F32),
                        pltpu.VMEM((2, tq, d_v), F32)],
        compiler_params=_params(("arbitrary", "arbitrary", "arbitrary")),
        name="attn_prompt",
    )(q, k, v, *lams, subln_w)


def _attn_sample_body(pt_ref, q_ref, kn_ref, vn_ref, lq1_ref, lk1_ref, lq2_ref, lk2_ref, sw_ref,
                      *rest, n_chunk, n_new, n_maps, n_heads, qr, lam_init):
    del pt_ref
    k_refs = rest[:n_chunk]
    v_refs = rest[n_chunk:2 * n_chunk]
    o_ref = rest[2 * n_chunk]
    m_ref, l_ref, acc_ref, qbd_ref = rest[2 * n_chunk + 1:]
    c = pl.program_id(1)
    page = k_refs[0].shape[2]
    d_qk = k_refs[0].shape[1] // n_maps
    hr = 2 * qr

    @pl.when(c == 0)
    def _():
        m_ref[...] = jnp.full(m_ref.shape, -jnp.inf, F32)
        l_ref[...] = jnp.zeros(l_ref.shape, F32)
        acc_ref[...] = jnp.zeros(acc_ref.shape, F32)

    q_all = q_ref[0]

    @pl.when(c == 0)
    def _():
        q_rep = jnp.concatenate([q_all] * n_maps, axis=1)
        row_map = lax.broadcasted_iota(I32, q_rep.shape, 0) // qr
        col_map = lax.broadcasted_iota(I32, q_rep.shape, 1) // d_qk
        qbd_ref[...] = jnp.where(row_map == col_map, q_rep, jnp.zeros_like(q_rep))

    m_p, l_p, acc = m_ref[...], l_ref[...], acc_ref[...]
    k_cat = jnp.concatenate([k_refs[i][0].astype(BF16) for i in range(n_chunk)], axis=1)
    s = jnp.dot(qbd_ref[...], k_cat, preferred_element_type=F32)
    m_n = jnp.maximum(m_p, jnp.max(s, axis=1, keepdims=True))
    p = jnp.exp(s - m_n)
    al = jnp.exp(m_p - m_n)
    l_p = al * l_p + jnp.sum(p, axis=1, keepdims=True)
    pb = p.astype(BF16)
    pv = []
    for h in range(n_heads):
        vh = jnp.concatenate(
            [v_refs[i][0, pl.ds(h, page, stride=n_heads), :].astype(BF16) for i in range(n_chunk)],
            axis=0)
        pv.append(jnp.dot(pb[h * hr:(h + 1) * hr], vh, preferred_element_type=F32))
    acc = al * acc + jnp.concatenate(pv, axis=0)
    m_p = m_n
    m_ref[...] = m_p
    l_ref[...] = l_p
    acc_ref[...] = acc

    @pl.when(c == pl.num_programs(1) - 1)
    def _():
        qf = q_all.astype(F32)
        kn, vn = kn_ref[0], vn_ref[0]
        q_tok = lax.broadcasted_iota(I32, (qf.shape[0], 1), 0) % qr
        s_new = []
        for j in range(n_new):
            parts = []
            for m in range(n_maps):
                krow = kn[j * n_maps + m:j * n_maps + m + 1, :]
                parts.append(jnp.sum(qf[m * qr:(m + 1) * qr] * krow, axis=1, keepdims=True))
            s_new.append(jnp.where(q_tok >= j, jnp.concatenate(parts, axis=0), -jnp.inf))
        m_n = m_p
        for sj in s_new:
            m_n = jnp.maximum(m_n, sj)
        al = jnp.exp(m_p - m_n)
        l_n = al * l_p
        a_n = al * acc
        for j in range(n_new):
            pj = jnp.exp(s_new[j] - m_n)
            l_n = l_n + pj
            a_n = a_n + jnp.concatenate(
                [pj[h * hr:(h + 1) * hr] * vn[j * n_heads + h:j * n_heads + h + 1, :]
                 for h in range(n_heads)], axis=0)
        a_n = a_n / l_n
        lam = _lambda(lq1_ref[...], lk1_ref[...], lq2_ref[...], lk2_ref[...], lam_init)
        d_v = a_n.shape[1]
        for h in range(n_heads):
            o = a_n[h * hr:h * hr + n_new] - lam * a_n[h * hr + qr:h * hr + qr + n_new]
            o_ref[0, :, h * d_v:(h + 1) * d_v] = (
                _rms(o, sw_ref[...]) * (1.0 - lam_init)).astype(o_ref.dtype)


def _attn_sample(pages, q_rows, k_new, v_new, ck, cv, lams, subln_w, *, n_new, n_maps, n_heads, qr,
                 lam_init, n_chunk):
    dec, n_pages = pages.shape
    d_qk = ck.shape[1] // n_maps
    d_v = cv.shape[2]
    rows = n_maps * qr
    pt_flat = pages.reshape(-1)

    def page_map(i, b, c, pt):
        return (pt[b * n_pages + c * n_chunk + i], 0, 0)

    vec = pl.BlockSpec((1, d_qk), lambda b, c, pt: (0, 0))
    k_specs = [pl.BlockSpec((1,) + ck.shape[1:], functools.partial(page_map, i)) for i in range(n_chunk)]
    v_specs = [pl.BlockSpec((1,) + cv.shape[1:], functools.partial(page_map, i)) for i in range(n_chunk)]
    grid_spec = pltpu.PrefetchScalarGridSpec(
        num_scalar_prefetch=1,
        grid=(dec, n_pages // n_chunk),
        in_specs=[
            pl.BlockSpec((1, rows, d_qk), lambda b, c, pt: (b, 0, 0)),
            pl.BlockSpec((1,) + k_new.shape[1:], lambda b, c, pt: (b, 0, 0)),
            pl.BlockSpec((1,) + v_new.shape[1:], lambda b, c, pt: (b, 0, 0)),
            vec, vec, vec, vec,
            pl.BlockSpec((1, d_v), lambda b, c, pt: (0, 0)),
        ] + k_specs + v_specs,
        out_specs=pl.BlockSpec((1, n_new, n_heads * d_v), lambda b, c, pt: (b, 0, 0)),
        scratch_shapes=[pltpu.VMEM((rows, 1), F32), pltpu.VMEM((rows, 1), F32),
                        pltpu.VMEM((rows, d_v), F32), pltpu.VMEM((rows, n_maps * d_qk), BF16)],
    )
    return pl.pallas_call(
        functools.partial(_attn_sample_body, n_chunk=n_chunk, n_new=n_new, n_maps=n_maps,
                          n_heads=n_heads, qr=qr, lam_init=lam_init),
        grid_spec=grid_spec,
        out_shape=jax.ShapeDtypeStruct((dec, n_new, n_heads * d_v), F32),
        compiler_params=_params(("arbitrary", "arbitrary")),
        name="attn_sample",
    )(pt_flat, q_rows, k_new, v_new, *lams, subln_w, *([ck] * n_chunk), *([cv] * n_chunk))


def _ln_swish(y, lw, lb):
    mu = jnp.mean(y, axis=-1, keepdims=True)
    yc = y - mu
    var = jnp.mean(yc * yc, axis=-1, keepdims=True)
    z = yc * lax.rsqrt(var + EPS) * lw + lb
    return z * jax.nn.sigmoid(z)


def _conv_prompt_body(u_ref, up_ref, dw_ref, db_ref, lw_ref, lb_ref, c_ref, buf_ref, y_ref,
                      *, tt, width, halo, rb):
    t = pl.program_id(1)
    buf_ref[0:halo, :] = jnp.where(t == 0, 0.0, up_ref[...])
    buf_ref[halo:halo + tt, :] = u_ref[...]
    off = halo - (width - 1)

    def col_body(cc, carry):
        c0 = pl.multiple_of(cc * LANES, LANES)
        wts = dw_ref[:, pl.ds(c0, LANES)]
        bias = db_ref[:, pl.ds(c0, LANES)]
        for r in range(tt // rb):
            acc = jnp.zeros((rb, LANES), F32)
            for w in range(width):
                lo = r * rb + off + w
                acc = acc + buf_ref[lo:lo + rb, pl.ds(c0, LANES)] * wts[w:w + 1, :]
            y_ref[r * rb:(r + 1) * rb, pl.ds(c0, LANES)] = acc + bias
        return carry

    lax.fori_loop(0, u_ref.shape[1] // LANES, col_body, 0)
    c_ref[...] = _ln_swish(y_ref[...], lw_ref[...], lb_ref[...]).astype(c_ref.dtype)


def _conv_prompt(u, dw_w, dw_b, ln_w, ln_b, *, batch, seq, tt):
    n, ch = u.shape
    width = dw_w.shape[0]
    halo = 32
    nt = seq // tt
    per = tt // halo
    row = pl.BlockSpec((1, ch), lambda b, t: (0, 0))
    return pl.pallas_call(
        functools.partial(_conv_prompt_body, tt=tt, width=width, halo=halo, rb=64),
        grid=(batch, nt),
        in_specs=[
            pl.BlockSpec((tt, ch), lambda b, t: (b * nt + t, 0)),
            pl.BlockSpec((halo, ch), lambda b, t: (jnp.maximum((b * nt + t) * per - 1, 0), 0)),
            pl.BlockSpec((width, ch), lambda b, t: (0, 0)),
            row, row, row,
        ],
        out_specs=pl.BlockSpec((tt, ch), lambda b, t: (b * nt + t, 0)),
        out_shape=jax.ShapeDtypeStruct((n, ch), BF16),
        scratch_shapes=[pltpu.VMEM((halo + tt, ch), F32), pltpu.VMEM((tt, ch), F32)],
        compiler_params=_params(("arbitrary", "arbitrary")),
        name="conv_prompt",
    )(u, u, dw_w, dw_b, ln_w, ln_b)


def _conv_sample_body(up_ref, dw_ref, db_ref, lw_ref, lb_ref, c_ref, y_ref, *, width, n_new):
    length, tb, ch = up_ref.shape

    def col_body(cc, carry):
        c0 = pl.multiple_of(cc * LANES, LANES)
        wts = dw_ref[:, pl.ds(c0, LANES)]
        bias = db_ref[:, pl.ds(c0, LANES)]
        accs = [jnp.zeros((tb, LANES), F32) for _ in range(n_new)]
        for j in range(length):
            x = up_ref[j, :, pl.ds(c0, LANES)]
            for t in range(n_new):
                w = j - t
                if 0 <= w < width:
                    accs[t] = accs[t] + x * wts[w:w + 1, :]
        for t in range(n_new):
            y_ref[t, :, pl.ds(c0, LANES)] = accs[t] + bias
        return carry

    lax.fori_loop(0, ch // LANES, col_body, 0)
    c_ref[...] = _ln_swish(y_ref[...], lw_ref[...], lb_ref[...]).astype(c_ref.dtype)


def _conv_sample(u_pad_t, dw_w, dw_b, ln_w, ln_b, *, n_new, tb):
    length, dec, ch = u_pad_t.shape
    width = dw_w.shape[0]
    row = pl.BlockSpec((1, ch), lambda b: (0, 0))
    return pl.pallas_call(
        functools.partial(_conv_sample_body, width=width, n_new=n_new),
        grid=(dec // tb,),
        in_specs=[
            pl.BlockSpec((length, tb, ch), lambda b: (0, b, 0)),
            pl.BlockSpec((width, ch), lambda b: (0, 0)),
            row, row, row,
        ],
        out_specs=pl.BlockSpec((n_new, tb, ch), lambda b: (0, b, 0)),
        out_shape=jax.ShapeDtypeStruct((n_new, dec, ch), BF16),
        scratch_shapes=[pltpu.VMEM((n_new, tb, ch), F32)],
        compiler_params=_params(("arbitrary",)),
        name="conv_sample",
    )(u_pad_t, dw_w, dw_b, ln_w, ln_b)


def _wo_body(o_ref, c_ref, x_ref, wo_ref, nw_ref, wr_ref, br_ref, *rest, aw, top_k):
    x1_ref, xf_ref, idx_ref, gate_ref = rest[-4:]
    attn = (jnp.dot(o_ref[...], wo_ref[0:aw, :], preferred_element_type=F32)
            + jnp.dot(c_ref[...], wo_ref[aw:, :], preferred_element_type=F32))
    x1 = x_ref[...] + attn
    x1_ref[...] = x1
    xf = _rms(x1, nw_ref[...])
    xf_ref[...] = xf
    xh = xf.astype(BF16)
    xl = (xf - xh.astype(F32)).astype(BF16)
    wr = wr_ref[...]
    wh = wr.astype(BF16)
    wl = (wr - wh.astype(F32)).astype(BF16)
    logits = (jnp.dot(xh, wh, preferred_element_type=F32) + jnp.dot(xl, wh, preferred_element_type=F32)
              + jnp.dot(xh, wl, preferred_element_type=F32)) + br_ref[...]
    n_exp = logits.shape[1]
    lane_e = lax.broadcasted_iota(I32, logits.shape, 1)
    lane_o = lax.broadcasted_iota(I32, idx_ref.shape, 1)
    vals = logits
    tops, idxs = [], []
    for _ in range(top_k):
        mx = jnp.max(vals, axis=1, keepdims=True)
        am = jnp.min(jnp.where(vals == mx, lane_e, n_exp), axis=1, keepdims=True)
        tops.append(mx)
        idxs.append(am)
        vals = jnp.where(lane_e == am, -jnp.inf, vals)
    es = [jnp.exp(t - tops[0]) for t in tops]
    den = es[0]
    for e in es[1:]:
        den = den + e
    idx_out = jnp.zeros(idx_ref.shape, I32)
    gate_out = jnp.zeros(gate_ref.shape, F32)
    for k in range(top_k):
        idx_out = jnp.where(lane_o == k, idxs[k], idx_out)
        gate_out = jnp.where(lane_o == k, es[k] / den, gate_out)
    idx_ref[...] = idx_out
    gate_ref[...] = gate_out


def _wo_router(o, c, x2d, w_o_bf, norm_w, w_router, b_router, *, tt, xf_rows, xf_row0=0, xf_buf=None):
    n, d = x2d.shape
    aw = o.shape[1]
    n_exp = w_router.shape[1]
    blk0 = xf_row0 // tt
    row = pl.BlockSpec((tt, d), lambda i: (i, 0))
    half = pl.BlockSpec((tt, aw), lambda i: (i, 0))
    small = pl.BlockSpec((tt, LANES), lambda i: (i, 0))
    in_specs = [
        half, pl.BlockSpec((tt, d - aw), lambda i: (i, 0)), row,
        pl.BlockSpec((d, d), lambda i: (0, 0)),
        pl.BlockSpec((1, d), lambda i: (0, 0)),
        pl.BlockSpec((d, n_exp), lambda i: (0, 0)),
        pl.BlockSpec((1, n_exp), lambda i: (0, 0)),
    ]
    args = [o, c, x2d, w_o_bf, norm_w, w_router, b_router]
    aliases = {}
    if xf_buf is not None:
        in_specs.append(pl.BlockSpec(memory_space=pl.ANY))
        args.append(xf_buf)
        aliases = {len(args) - 1: 1}
    return pl.pallas_call(
        functools.partial(_wo_body, aw=aw, top_k=TOP_K),
        grid=(n // tt,),
        in_specs=in_specs,
        out_specs=[row, pl.BlockSpec((tt, d), lambda i: (blk0 + i, 0)), small, small],
        out_shape=[
            jax.ShapeDtypeStruct((n, d), F32),
            jax.ShapeDtypeStruct((xf_rows, d), F32),
            jax.ShapeDtypeStruct((n, LANES), I32),
            jax.ShapeDtypeStruct((n, LANES), F32),
        ],
        input_output_aliases=aliases,
        compiler_params=_params(("arbitrary",)),
        name="wo_router",
    )(*args)


def _route_plan(idx, n_exp, tm, ts):
    n_tok, top_k = idx.shape
    n_asg = n_tok * top_k
    nsub = tm // ts
    flat_e = idx.reshape(-1)
    onehot = (flat_e[:, None] == jnp.arange(n_exp, dtype=I32)[None, :]).astype(I32)
    csum = jnp.cumsum(onehot, axis=0)
    rank = jnp.sum((csum - onehot) * onehot, axis=1)
    counts = csum[-1]
    ntile = (counts + tm - 1) // tm
    first_rows = counts - jnp.maximum(ntile - 1, 0) * tm
    tile_end = jnp.cumsum(ntile)
    tile_beg = tile_end - ntile
    fr = first_rows[flat_e]
    dest = (tile_beg[flat_e] * tm + jnp.where(rank < fr, rank, rank - fr + tm)).astype(I32)

    n_tiles = -(-n_asg // tm) + n_exp
    t_ids = jnp.arange(n_tiles, dtype=I32)
    tile_exp = jnp.minimum(jnp.sum(t_ids[:, None] >= tile_end[None, :], axis=1), n_exp - 1).astype(I32)
    n_used = tile_end[-1]
    first_sub = (first_rows[tile_exp] + ts - 1) // ts
    tile_nsub = jnp.where(t_ids < n_used, jnp.where(t_ids == tile_beg[tile_exp], first_sub, nsub), 0)
    tile_nsub = tile_nsub.astype(I32)

    n_cs = -(-n_asg // ts) + n_exp
    sub_end = jnp.cumsum(tile_nsub)
    n_csub = sub_end[-1]
    i_ids = jnp.minimum(jnp.arange(n_cs, dtype=I32), n_csub - 1)
    cs_tile = jnp.minimum(jnp.sum(i_ids[:, None] >= sub_end[None, :], axis=1), n_tiles - 1)
    cs_blk = (cs_tile * nsub + i_ids - (sub_end - tile_nsub)[cs_tile]).astype(I32)
    flat_tok = jnp.arange(n_asg, dtype=I32) // top_k
    buf_tok = jnp.zeros((n_tiles * tm,), I32).at[dest].set(flat_tok, unique_indices=True)
    tok_c = buf_tok.reshape(n_tiles * nsub, ts)[cs_blk].reshape(-1)
    return (dest, tile_exp, tile_nsub, n_used.astype(I32).reshape(1), tok_c, cs_blk,
            n_csub.astype(I32).reshape(1))


def _gather_body(tok_ref, blk_ref, ncs_ref, x_hbm, xs_ref, land_ref, sem, *, ts):
    del blk_ref
    i = pl.program_id(0)
    n = ncs_ref[0]

    def row_copy(step, slot, r):
        tok = tok_ref[step * ts + r]
        return pltpu.make_async_copy(x_hbm.at[pl.ds(tok, 1)], land_ref.at[slot, pl.ds(r, 1)],
                                     sem.at[slot])

    def start_all(step, slot):
        def body(g, carry):
            for u in range(GATHER_UNROLL):
                row_copy(step, slot, g * GATHER_UNROLL + u).start(priority=1)
            return carry
        lax.fori_loop(0, ts // GATHER_UNROLL, body, 0)

    def wait_all(step, slot):
        def body(g, carry):
            for u in range(GATHER_UNROLL):
                row_copy(step, slot, g * GATHER_UNROLL + u).wait()
            return carry
        lax.fori_loop(0, ts // GATHER_UNROLL, body, 0)

    ahead = GATHER_SLOTS - 1

    @pl.when(i == 0)
    def _():
        for s in range(ahead):
            @pl.when(s < n)
            def _():
                start_all(s, s)

    @pl.when(i + ahead < n)
    def _():
        start_all(i + ahead, (i + ahead) % GATHER_SLOTS)

    @pl.when(i < n)
    def _():
        wait_all(i, i % GATHER_SLOTS)
        xs_ref[...] = land_ref[i % GATHER_SLOTS].astype(xs_ref.dtype)


def _gather_rows(xf, tok_c, cs_blk, n_csub, *, n_rows, ts):
    d = xf.shape[1]
    n_cs = cs_blk.shape[0]
    grid_spec = pltpu.PrefetchScalarGridSpec(
        num_scalar_prefetch=3,
        grid=(n_cs,),
        in_specs=[pl.BlockSpec(memory_space=pl.ANY)],
        out_specs=pl.BlockSpec((ts, d), lambda i, tok, blk, ncs: (blk[i], 0)),
        scratch_shapes=[pltpu.VMEM((GATHER_SLOTS, ts, d), xf.dtype),
                        pltpu.SemaphoreType.DMA((GATHER_SLOTS,))],
    )
    return pl.pallas_call(
        functools.partial(_gather_body, ts=ts),
        grid_spec=grid_spec,
        out_shape=jax.ShapeDtypeStruct((n_rows, d), BF16),
        compiler_params=_params(("arbitrary",)),
        name="moe_gather",
    )(tok_c, cs_blk, n_csub, xf)


def _expert_changed(texp_ref, t):
    return (t == 0) | (texp_ref[t] != texp_ref[jnp.maximum(t - 1, 0)])


def _moe_up_body(texp_ref, tns_ref, nused_ref, x_ref, wg_ref, wu_ref, bg_ref, bu_ref, a_ref, wbf_ref,
                 *, ts):
    t = pl.program_id(1)

    @pl.when(t < nused_ref[0])
    def _():
        @pl.when(_expert_changed(texp_ref, t))
        def _():
            wbf_ref[0] = wg_ref[0].astype(BF16)
            wbf_ref[1] = wu_ref[0].astype(BF16)

        def sub(j, carry):
            r0 = pl.multiple_of(j * ts, ts)
            x = x_ref[pl.ds(r0, ts), :]
            g = jnp.dot(x, wbf_ref[0], preferred_element_type=F32) + bg_ref[0]
            lin = jnp.dot(x, wbf_ref[1], preferred_element_type=F32) + bu_ref[0]
            g = jnp.minimum(g, SWIGLU_LIMIT)
            lin = jnp.clip(lin, -SWIGLU_LIMIT, SWIGLU_LIMIT)
            a = g * jax.nn.sigmoid(SWIGLU_ALPHA * g) * (lin + 1.0)
            a_ref[pl.ds(r0, ts), :] = a.astype(a_ref.dtype)
            return carry

        lax.fori_loop(0, tns_ref[t], sub, 0)


def _moe_down_body(texp_ref, tns_ref, nused_ref, a_ref, wd_ref, bd_ref, y_ref, wbf_ref, *, ts):
    t = pl.program_id(1)

    @pl.when(t < nused_ref[0])
    def _():
        @pl.when(_expert_changed(texp_ref, t))
        def _():
            wbf_ref[...] = wd_ref[0].astype(BF16)

        def sub(j, carry):
            r0 = pl.multiple_of(j * ts, ts)
            y_ref[pl.ds(r0, ts), :] = jnp.dot(a_ref[pl.ds(r0, ts), :], wbf_ref[...],
                                              preferred_element_type=F32) + bd_ref[0]
            return carry

        lax.fori_loop(0, tns_ref[t], sub, 0)


def _moe_experts(xs, tile_exp, tile_nsub, n_used, w_gate_up, b_gate_up, w_down, b_down, *, tm, ts, tf, tn):
    n_rows, d = xs.shape
    n_exp, _, two_ff = w_gate_up.shape
    d_ff = two_ff // 2
    n_f = d_ff // tf
    n_t = n_rows // tm
    bgu = b_gate_up.reshape(n_exp, 1, two_ff)
    bdn = b_down.reshape(n_exp, 1, d)

    def tile(t, nu):
        return jnp.minimum(t, nu[0] - 1)

    up_spec = pltpu.PrefetchScalarGridSpec(
        num_scalar_prefetch=3,
        grid=(n_f, n_t),
        in_specs=[
            pl.BlockSpec((tm, d), lambda f, t, te, ns, nu: (tile(t, nu), 0)),
            pl.BlockSpec((1, d, tf), lambda f, t, te, ns, nu: (te[tile(t, nu)], 0, f)),
            pl.BlockSpec((1, d, tf), lambda f, t, te, ns, nu: (te[tile(t, nu)], 0, n_f + f)),
            pl.BlockSpec((1, 1, tf), lambda f, t, te, ns, nu: (te[tile(t, nu)], 0, f)),
            pl.BlockSpec((1, 1, tf), lambda f, t, te, ns, nu: (te[tile(t, nu)], 0, n_f + f)),
        ],
        out_specs=pl.BlockSpec((tm, tf), lambda f, t, te, ns, nu: (tile(t, nu), f)),
        scratch_shapes=[pltpu.VMEM((2, d, tf), BF16)],
    )
    act = pl.pallas_call(
        functools.partial(_moe_up_body, ts=ts),
        grid_spec=up_spec,
        out_shape=jax.ShapeDtypeStruct((n_rows, d_ff), BF16),
        compiler_params=_params(("arbitrary", "arbitrary")),
        name="moe_up",
    )(tile_exp, tile_nsub, n_used, xs, w_gate_up, w_gate_up, bgu, bgu)

    n_n = d // tn
    down_spec = pltpu.PrefetchScalarGridSpec(
        num_scalar_prefetch=3,
        grid=(n_n, n_t),
        in_specs=[
            pl.BlockSpec((tm, d_ff), lambda n, t, te, ns, nu: (tile(t, nu), 0)),
            pl.BlockSpec((1, d_ff, tn), lambda n, t, te, ns, nu: (te[tile(t, nu)], 0, n)),
            pl.BlockSpec((1, 1, tn), lambda n, t, te, ns, nu: (te[tile(t, nu)], 0, n)),
        ],
        out_specs=pl.BlockSpec((tm, tn), lambda n, t, te, ns, nu: (tile(t, nu), n)),
        scratch_shapes=[pltpu.VMEM((d_ff, tn), BF16)],
    )
    return pl.pallas_call(
        functools.partial(_moe_down_body, ts=ts),
        grid_spec=down_spec,
        out_shape=jax.ShapeDtypeStruct((n_rows, d), F32),
        compiler_params=_params(("arbitrary", "arbitrary")),
        name="moe_down",
    )(tile_exp, tile_nsub, n_used, act, w_down, bdn)


def _combine_body(dest_ref, x1_ref, gate_ref, nw_ref, yb_hbm, out_ref, rows_ref, sem, *, tt, top_k):
    i = pl.program_id(0)

    def row_copy(r, k):
        d = dest_ref[(i * tt + r) * top_k + k]
        return pltpu.make_async_copy(yb_hbm.at[pl.ds(d, 1)], rows_ref.at[k, pl.ds(r, 1)], sem)

    rows_per_iter = GATHER_UNROLL // top_k

    def issue(g, carry):
        for u in range(rows_per_iter):
            for k in range(top_k):
                row_copy(g * rows_per_iter + u, k).start(priority=1)
        return carry

    def drain(g, carry):
        for u in range(rows_per_iter):
            for k in range(top_k):
                row_copy(g * rows_per_iter + u, k).wait()
        return carry

    lax.fori_loop(0, tt // rows_per_iter, issue, 0)
    lax.fori_loop(0, tt // rows_per_iter, drain, 0)
    gate = gate_ref[...]
    moe = rows_ref[0] * gate[:, 0:1]
    for k in range(1, top_k):
        moe = moe + rows_ref[k] * gate[:, k:k + 1]
    out_ref[...] = _rms(x1_ref[...] + moe, nw_ref[...])


def _combine(dest, x1, gate, norm_w, yb, *, tt):
    n, d = x1.shape
    grid_spec = pltpu.PrefetchScalarGridSpec(
        num_scalar_prefetch=1,
        grid=(n // tt,),
        in_specs=[
            pl.BlockSpec((tt, d), lambda i, ds: (i, 0)),
            pl.BlockSpec((tt, LANES), lambda i, ds: (i, 0)),
            pl.BlockSpec((1, d), lambda i, ds: (0, 0)),
            pl.BlockSpec(memory_space=pl.ANY),
        ],
        out_specs=pl.BlockSpec((tt, d), lambda i, ds: (i, 0)),
        scratch_shapes=[pltpu.VMEM((TOP_K, tt, d), F32), pltpu.SemaphoreType.DMA],
    )
    return pl.pallas_call(
        functools.partial(_combine_body, tt=tt, top_k=TOP_K),
        grid_spec=grid_spec,
        out_shape=jax.ShapeDtypeStruct((n, d), F32),
        compiler_params=_params(("arbitrary",)),
        name="moe_combine",
    )(dest, x1, gate, norm_w, yb)


def _tile(n, pref):
    t = min(n, pref)
    while n % t:
        t //= 2
    return t


def kernel(x_prompt, x_sample, cache_k, cache_v, state_conv, page_table, norm_mix_w, w_in,
           lambda_q1, lambda_k1, lambda_q2, lambda_k2, subln_w, conv_dw_w, conv_dw_b,
           conv_norm_w, conv_norm_b, w_o, norm_ffn_w, w_router, b_router, w_gate_up, b_gate_up,
           w_down, b_down, norm_final_w):
    batch, seq, d = x_prompt.shape
    dec, n_new, _ = x_sample.shape
    depth, n_pool, page, n_maps, d_qk = cache_k.shape
    n_heads, d_v = cache_v.shape[3:]
    aw = n_heads * d_v
    ch = d - aw
    n_pages = page_table.shape[1]
    past = n_pages * page
    rope_dim = d_qk // 4
    half = rope_dim // 2
    scale = d_qk ** -0.5
    n_exp = w_router.shape[2]
    n_p, n_s = batch * seq, dec * n_new

    h_p = x_prompt.reshape(n_p, d)
    h_s = x_sample.reshape(n_s, d)
    tab_p = _rope_tables(jnp.arange(seq, dtype=I32), d_qk, rope_dim)
    tab_s = _rope_tables(past + jnp.arange(n_s, dtype=I32) % n_new, d_qk, rope_dim)
    outs = {k: [] for k in ("kp", "vp", "cp", "ks", "vs", "cs")}

    for l in range(depth):
        lam_init = 0.8 - 0.6 * math.exp(-0.3 * l)
        lams = tuple(v[l].reshape(1, d_qk) for v in (lambda_q1, lambda_k1, lambda_q2, lambda_k2))
        sw = subln_w[l].reshape(1, d_v)
        w_in_bf = w_in[l].astype(BF16)
        w_o_bf = w_o[l].astype(BF16)
        nmw = norm_mix_w[l].reshape(1, d)
        dw_b = conv_dw_b[l].reshape(1, ch)
        ln_w = conv_norm_w[l].reshape(1, ch)
        ln_b = conv_norm_b[l].reshape(1, ch)

        tt_p = _tile(seq, 512)
        q_p, k_p, v_p, u_p = _proj(h_p, nmw, w_in_bf, tab_p, scale=scale, half=half, tt=tt_p)
        tt_s = _tile(n_s, 512)
        q_s, k_s, v_s, u_s = _proj(h_s, nmw, w_in_bf, tab_s, scale=scale, half=half, tt=tt_s)

        o_p = _attn_prompt(q_p, k_p, v_p, lams, sw, batch=batch, seq=seq, n_heads=n_heads,
                           d_qk=d_qk, lam_init=lam_init, tq=_tile(seq, 512))
        q_rows = q_s.reshape(dec, n_new, n_maps, d_qk).transpose(0, 2, 1, 3)
        q_rows = jnp.pad(q_rows, ((0, 0), (0, 0), (0, QR - n_new), (0, 0))).reshape(dec, n_maps * QR, d_qk)
        ck = cache_k.transpose(0, 1, 3, 4, 2).reshape(depth * n_pool, n_maps * d_qk, page)
        cv = cache_v.reshape(depth * n_pool, page * n_heads, d_v)
        o_s = _attn_sample(page_table + l * n_pool, q_rows, k_s.reshape(dec, n_new * n_maps, d_qk),
                           v_s.reshape(dec, n_new * n_heads, d_v), ck, cv, lams, sw, n_new=n_new,
                           n_maps=n_maps, n_heads=n_heads, qr=QR, lam_init=lam_init,
                           n_chunk=_tile(n_pages, 16))
        o_s = o_s.reshape(n_s, aw).astype(BF16)

        c_p = _conv_prompt(u_p, conv_dw_w[l], dw_b, ln_w, ln_b, batch=batch, seq=seq,
                           tt=_tile(seq, 256))
        u_s3 = u_s.reshape(dec, n_new, ch)
        u_pad_s = jnp.concatenate([state_conv[l], u_s3], axis=1)
        c_s = _conv_sample(u_pad_s.transpose(1, 0, 2), conv_dw_w[l], dw_b, ln_w, ln_b,
                           n_new=n_new, tb=_tile(dec, 16))
        c_s = c_s.transpose(1, 0, 2).reshape(n_s, ch)

        nfw = norm_ffn_w[l].reshape(1, d)
        br = b_router[l].reshape(1, n_exp)
        assert n_p % tt_s == 0
        x1_p, xf_all, idx_p, gate_p = _wo_router(o_p, c_p, h_p, w_o_bf, nfw, w_router[l], br, tt=tt_p,
                                                 xf_rows=n_p + n_s)
        x1_s, xf_all, idx_s, gate_s = _wo_router(o_s, c_s, h_s, w_o_bf, nfw, w_router[l], br, tt=tt_s,
                                                 xf_rows=n_p + n_s, xf_row0=n_p, xf_buf=xf_all)

        idx_all = jnp.concatenate([idx_p[:, :TOP_K], idx_s[:, :TOP_K]], axis=0)
        dest, tile_exp, tile_nsub, n_used, tok_c, cs_blk, n_csub = _route_plan(
            idx_all, n_exp, MOE_TILE, MOE_SUBTILE)
        xs = _gather_rows(xf_all, tok_c, cs_blk, n_csub, n_rows=tile_exp.shape[0] * MOE_TILE,
                          ts=MOE_SUBTILE)
        yb = _moe_experts(xs, tile_exp, tile_nsub, n_used, w_gate_up[l], b_gate_up[l], w_down[l],
                          b_down[l], tm=MOE_TILE, ts=MOE_SUBTILE, tf=_tile(w_down.shape[2], 1024),
                          tn=_tile(d, 1024))

        last = l == depth - 1
        nw_out = norm_final_w.reshape(1, d) if last else jnp.ones((1, d), F32)
        assert last, "the combine kernel fuses the final norm; deeper stacks need an un-normed variant"
        h_p = _combine(dest[:n_p * TOP_K], x1_p, gate_p, nw_out, yb, tt=_tile(n_p, 256))
        h_s = _combine(dest[n_p * TOP_K:], x1_s, gate_s, nw_out, yb, tt=_tile(n_s, 256))

        outs["kp"].append(k_p.reshape(batch, seq, n_maps, d_qk))
        outs["vp"].append(v_p.reshape(batch, seq, n_heads, d_v))
        outs["cp"].append(u_p.reshape(batch, seq, ch)[:, seq - (conv_dw_w.shape[1] - 1):])
        outs["ks"].append(k_s.reshape(dec, n_new, n_maps, d_qk))
        outs["vs"].append(v_s.reshape(dec, n_new, n_heads, d_v))
        outs["cs"].append(u_pad_s[:, n_new:])

    return (h_p.reshape(batch, seq, d), h_s.reshape(dec, n_new, d),
            jnp.stack(outs["kp"]), jnp.stack(outs["vp"]), jnp.stack(outs["cp"]),
            jnp.stack(outs["ks"]), jnp.stack(outs["vs"]), jnp.stack(outs["cs"]))
```

```python
import functools
import math

import jax
import jax.numpy as jnp
from jax import lax
from jax.experimental import pallas as pl
from jax.experimental.pallas import tpu as pltpu

F32 = jnp.float32
BF16 = jnp.bfloat16
I32 = jnp.int32

EPS = 1e-5
ROPE_THETA = 500000.0
TOP_K = 4
SWIGLU_LIMIT = 7.0
SWIGLU_ALPHA = 1.702
LANES = 128
VMEM_LIMIT = 56 * 1024 * 1024
QR = 8
MOE_TILE = 768
MOE_SUBTILE = 256
GATHER_UNROLL = 8


def _params(sem, vmem=VMEM_LIMIT):
    return pltpu.CompilerParams(dimension_semantics=sem, vmem_limit_bytes=vmem)


def _nt_dot(a, b):
    return lax.dot_general(a, b, (((1,), (1,)), ((), ())), preferred_element_type=F32)


def _rms(x, w):
    ms = jnp.mean(x * x, axis=-1, keepdims=True)
    return x * lax.rsqrt(ms + EPS) * w


def _lambda(lq1, lk1, lq2, lk2, lam_init):
    a = jnp.sum(lq1 * lk1, axis=-1, keepdims=True)
    b = jnp.sum(lq2 * lk2, axis=-1, keepdims=True)
    return jnp.exp(a) - jnp.exp(b) + lam_init


def _proj_body(x_ref, nw_ref, wa_ref, wb_ref, cs_ref, s1_ref, s2_ref,
               q_ref, k_ref, v_ref, u_ref, xn_ref, *, scale, half):
    j = pl.program_id(1)

    @pl.when(j == 0)
    def _():
        xn_ref[...] = _rms(x_ref[...], nw_ref[...]).astype(BF16)

    def rope_store(dst_ref, p, mul):
        cs, s1, s2 = cs_ref[...], s1_ref[...], s2_ref[...]
        for c in range(p.shape[1] // LANES):
            seg = p[:, c * LANES:(c + 1) * LANES]
            r = (seg * cs + pltpu.roll(seg, LANES - half, 1) * s1
                 + pltpu.roll(seg, half, 1) * s2)
            if mul is not None:
                r = r * mul
            dst_ref[:, c * LANES:(c + 1) * LANES] = r.astype(dst_ref.dtype)

    @pl.when(j == 0)
    def _():
        p = jnp.dot(xn_ref[...], wa_ref[...], preferred_element_type=F32)
        rope_store(q_ref, p, scale)

    @pl.when(j == 1)
    def _():
        p = jnp.dot(xn_ref[...], wa_ref[...], preferred_element_type=F32)
        rope_store(k_ref, p, None)

    @pl.when(j == 2)
    def _():
        v_ref[...] = jnp.dot(xn_ref[...], wa_ref[...], preferred_element_type=F32)

    @pl.when(j == 3)
    def _():
        xn = xn_ref[...]
        val = jnp.dot(xn, wa_ref[...], preferred_element_type=F32)
        gate = jnp.dot(xn, wb_ref[...], preferred_element_type=F32)
        u_ref[...] = val * jax.nn.sigmoid(gate)


def _proj(x2d, norm_w, w_in_bf, tables, *, scale, half, tt):
    n, d = x2d.shape
    w = w_in_bf.shape[1] // 5
    cs, s1, s2 = tables
    n_pos_blocks = cs.shape[0] // tt
    tab_spec = pl.BlockSpec((tt, LANES), lambda i, j: (i % n_pos_blocks, 0))
    out_spec = pl.BlockSpec((tt, w), lambda i, j: (i, 0))
    return pl.pallas_call(
        functools.partial(_proj_body, scale=scale, half=half),
        grid=(n // tt, 4),
        in_specs=[
            pl.BlockSpec((tt, d), lambda i, j: (i, 0)),
            pl.BlockSpec((1, d), lambda i, j: (0, 0)),
            pl.BlockSpec((d, w), lambda i, j: (0, j)),
            pl.BlockSpec((d, w), lambda i, j: (0, 4)),
            tab_spec, tab_spec, tab_spec,
        ],
        out_specs=[out_spec, out_spec, out_spec, out_spec],
        out_shape=[
            jax.ShapeDtypeStruct((n, w), BF16),
            jax.ShapeDtypeStruct((n, w), F32),
            jax.ShapeDtypeStruct((n, w), F32),
            jax.ShapeDtypeStruct((n, w), F32),
        ],
        scratch_shapes=[pltpu.VMEM((tt, d), BF16)],
        compiler_params=_params(("arbitrary", "arbitrary")),
        name="proj",
    )(x2d, norm_w, w_in_bf, w_in_bf, cs, s1, s2)


def _rope_tables(pos, d_qk, rope_dim):
    half = rope_dim // 2
    inv = ROPE_THETA ** (-jnp.arange(0, rope_dim, 2, dtype=F32) / rope_dim)
    ang = pos.astype(F32)[:, None] * inv[None, :]
    cos, sin = jnp.cos(ang), jnp.sin(ang)
    p = pos.shape[0]
    zh = jnp.zeros((p, half), F32)
    zr = jnp.zeros((p, d_qk - rope_dim), F32)
    cs = jnp.concatenate([cos, cos, jnp.ones((p, d_qk - rope_dim), F32)], axis=-1)
    s1 = jnp.concatenate([-sin, zh, zr], axis=-1)
    s2 = jnp.concatenate([zh, sin, zr], axis=-1)
    rep = LANES // d_qk
    return tuple(jnp.tile(t, (1, rep)) for t in (cs, s1, s2))


def _attn_prompt_body(q_ref, k_ref, v_ref, lq1_ref, lk1_ref, lq2_ref, lk2_ref, sw_ref,
                      o_ref, kb_ref, vb_ref, m_ref, l_ref, acc_ref, *, tq, tk, d_qk, lam_init):
    qi = pl.program_id(2)

    @pl.when(qi == 0)
    def _():
        kb_ref[...] = k_ref[...].astype(BF16)
        vb_ref[...] = v_ref[...].astype(BF16)

    q = q_ref[...]
    lane = lax.broadcasted_iota(I32, q.shape, 1)
    zero = jnp.zeros_like(q)
    qm = (jnp.where(lane < d_qk, q, zero), jnp.where(lane >= d_qk, q, zero))
    m_ref[...] = jnp.full(m_ref.shape, -jnp.inf, F32)
    l_ref[...] = jnp.zeros(l_ref.shape, F32)
    acc_ref[...] = jnp.zeros(acc_ref.shape, F32)

    def step(c, masked):
        k0 = pl.multiple_of(c * tk, tk)
        kc = kb_ref[pl.ds(k0, tk), :]
        vc = vb_ref[pl.ds(k0, tk), :]
        for mi in range(2):
            s = _nt_dot(qm[mi], kc)
            if masked:
                row = lax.broadcasted_iota(I32, (tq, tk), 0)
                col = lax.broadcasted_iota(I32, (tq, tk), 1)
                s = jnp.where(col + c * tk <= row + qi * tq, s, -jnp.inf)
            m_p = m_ref[mi]
            m_n = jnp.maximum(m_p, jnp.max(s, axis=1, keepdims=True))
            p = jnp.exp(s - jnp.concatenate([m_n] * (tk // LANES), axis=1))
            al = jnp.exp(m_p - m_n)
            l_ref[mi] = al * l_ref[mi] + jnp.sum(p, axis=1, keepdims=True)
            acc_ref[mi] = al * acc_ref[mi] + jnp.dot(p.astype(BF16), vc, preferred_element_type=F32)
            m_ref[mi] = m_n

    def full_step(c, carry):
        step(c, False)
        return carry

    n_full = qi * (tq // tk)
    lax.fori_loop(0, n_full, full_step, 0)
    for dgl in range(tq // tk):
        step(n_full + dgl, True)

    lam = _lambda(lq1_ref[...], lk1_ref[...], lq2_ref[...], lk2_ref[...], lam_init)
    o = acc_ref[0] / l_ref[0] - lam * (acc_ref[1] / l_ref[1])
    o_ref[...] = (_rms(o, sw_ref[...]) * (1.0 - lam_init)).astype(o_ref.dtype)


def _attn_prompt(q, k, v, lams, subln_w, *, batch, seq, n_heads, d_qk, lam_init, tq):
    n, w = q.shape
    d_v = w // n_heads
    nq = seq // tq
    vec = pl.BlockSpec((1, d_qk), lambda b, h, i: (0, 0))
    return pl.pallas_call(
        functools.partial(_attn_prompt_body, tq=tq, tk=tq, d_qk=d_qk, lam_init=lam_init),
        grid=(batch, n_heads, nq),
        in_specs=[
            pl.BlockSpec((tq, d_v), lambda b, h, i: (b * nq + i, h)),
            pl.BlockSpec((seq, d_v), lambda b, h, i: (b, h)),
            pl.BlockSpec((seq, d_v), lambda b, h, i: (b, h)),
            vec, vec, vec, vec,
            pl.BlockSpec((1, d_v), lambda b, h, i: (0, 0)),
        ],
        out_specs=pl.BlockSpec((tq, d_v), lambda b, h, i: (b * nq + i, h)),
        out_shape=jax.ShapeDtypeStruct((n, w), BF16),
        scratch_shapes=[pltpu.VMEM((seq, d_v), BF16), pltpu.VMEM((seq, d_v), BF16),
                        pltpu.VMEM((2, tq, LANES), F32), pltpu.VMEM((2, tq, LANES), F32),
                        pltpu.VMEM((2, tq, d_v), F32)],
        compiler_params=_params(("arbitrary", "arbitrary", "arbitrary")),
        name="attn_prompt",
    )(q, k, v, *lams, subln_w)


def _attn_sample_body(pt_ref, q_ref, kn_ref, vn_ref, lq1_ref, lk1_ref, lq2_ref, lk2_ref, sw_ref,
                      *rest, n_chunk, n_new, n_maps, n_heads, qr, lam_init):
    del pt_ref
    k_refs = rest[:n_chunk]
    v_refs = rest[n_chunk:2 * n_chunk]
    o_ref = rest[2 * n_chunk]
    m_ref, l_ref, acc_ref, qbd_ref = rest[2 * n_chunk + 1:]
    c = pl.program_id(1)
    page = k_refs[0].shape[2]
    d_qk = k_refs[0].shape[1] // n_maps
    hr = 2 * qr

    @pl.when(c == 0)
    def _():
        m_ref[...] = jnp.full(m_ref.shape, -jnp.inf, F32)
        l_ref[...] = jnp.zeros(l_ref.shape, F32)
        acc_ref[...] = jnp.zeros(acc_ref.shape, F32)

    q_all = q_ref[0]

    @pl.when(c == 0)
    def _():
        q_rep = jnp.concatenate([q_all] * n_maps, axis=1)
        row_map = lax.broadcasted_iota(I32, q_rep.shape, 0) // qr
        col_map = lax.broadcasted_iota(I32, q_rep.shape, 1) // d_qk
        qbd_ref[...] = jnp.where(row_map == col_map, q_rep, jnp.zeros_like(q_rep))

    m_p, l_p, acc = m_ref[...], l_ref[...], acc_ref[...]
    k_cat = jnp.concatenate([k_refs[i][0].astype(BF16) for i in range(n_chunk)], axis=1)
    s = jnp.dot(qbd_ref[...], k_cat, preferred_element_type=F32)
    m_n = jnp.maximum(m_p, jnp.max(s, axis=1, keepdims=True))
    p = jnp.exp(s - m_n)
    al = jnp.exp(m_p - m_n)
    l_p = al * l_p + jnp.sum(p, axis=1, keepdims=True)
    pb = p.astype(BF16)
    pv = []
    for h in range(n_heads):
        vh = jnp.concatenate(
            [v_refs[i][0, pl.ds(h, page, stride=n_heads), :].astype(BF16) for i in range(n_chunk)],
            axis=0)
        pv.append(jnp.dot(pb[h * hr:(h + 1) * hr], vh, preferred_element_type=F32))
    acc = al * acc + jnp.concatenate(pv, axis=0)
    m_p = m_n
    m_ref[...] = m_p
    l_ref[...] = l_p
    acc_ref[...] = acc

    @pl.when(c == pl.num_programs(1) - 1)
    def _():
        qf = q_all.astype(F32)
        kn, vn = kn_ref[0], vn_ref[0]
        q_tok = lax.broadcasted_iota(I32, (qf.shape[0], 1), 0) % qr
        s_new = []
        for j in range(n_new):
            parts = []
            for m in range(n_maps):
                krow = kn[j * n_maps + m:j * n_maps + m + 1, :]
                parts.append(jnp.sum(qf[m * qr:(m + 1) * qr] * krow, axis=1, keepdims=True))
            s_new.append(jnp.where(q_tok >= j, jnp.concatenate(parts, axis=0), -jnp.inf))
        m_n = m_p
        for sj in s_new:
            m_n = jnp.maximum(m_n, sj)
        al = jnp.exp(m_p - m_n)
        l_n = al * l_p
        a_n = al * acc
        for j in range(n_new):
            pj = jnp.exp(s_new[j] - m_n)
            l_n = l_n + pj
            a_n = a_n + jnp.concatenate(
                [pj[h * hr:(h + 1) * hr] * vn[j * n_heads + h:j * n_heads + h + 1, :]
                 for h in range(n_heads)], axis=0)
        a_n = a_n / l_n
        lam = _lambda(lq1_ref[...], lk1_ref[...], lq2_ref[...], lk2_ref[...], lam_init)
        d_v = a_n.shape[1]
        for h in range(n_heads):
            o = a_n[h * hr:h * hr + n_new] - lam * a_n[h * hr + qr:h * hr + qr + n_new]
            o_ref[0, :, h * d_v:(h + 1) * d_v] = (
                _rms(o, sw_ref[...]) * (1.0 - lam_init)).astype(o_ref.dtype)


def _attn_sample(pages, q_rows, k_new, v_new, ck, cv, lams, subln_w, *, n_new, n_maps, n_heads, qr,
                 lam_init, n_chunk):
    dec, n_pages = pages.shape
    d_qk = ck.shape[1] // n_maps
    d_v = cv.shape[2]
    rows = n_maps * qr
    pt_flat = pages.reshape(-1)

    def page_map(i, b, c, pt):
        return (pt[b * n_pages + c * n_chunk + i], 0, 0)

    vec = pl.BlockSpec((1, d_qk), lambda b, c, pt: (0, 0))
    k_specs = [pl.BlockSpec((1,) + ck.shape[1:], functools.partial(page_map, i)) for i in range(n_chunk)]
    v_specs = [pl.BlockSpec((1,) + cv.shape[1:], functools.partial(page_map, i)) for i in range(n_chunk)]
    grid_spec = pltpu.PrefetchScalarGridSpec(
        num_scalar_prefetch=1,
        grid=(dec, n_pages // n_chunk),
        in_specs=[
            pl.BlockSpec((1, rows, d_qk), lambda b, c, pt: (b, 0, 0)),
            pl.BlockSpec((1,) + k_new.shape[1:], lambda b, c, pt: (b, 0, 0)),
            pl.BlockSpec((1,) + v_new.shape[1:], lambda b, c, pt: (b, 0, 0)),
            vec, vec, vec, vec,
            pl.BlockSpec((1, d_v), lambda b, c, pt: (0, 0)),
        ] + k_specs + v_specs,
        out_specs=pl.BlockSpec((1, n_new, n_heads * d_v), lambda b, c, pt: (b, 0, 0)),
        scratch_shapes=[pltpu.VMEM((rows, 1), F32), pltpu.VMEM((rows, 1), F32),
                        pltpu.VMEM((rows, d_v), F32), pltpu.VMEM((rows, n_maps * d_qk), BF16)],
    )
    return pl.pallas_call(
        functools.partial(_attn_sample_body, n_chunk=n_chunk, n_new=n_new, n_maps=n_maps,
                          n_heads=n_heads, qr=qr, lam_init=lam_init),
        grid_spec=grid_spec,
        out_shape=jax.ShapeDtypeStruct((dec, n_new, n_heads * d_v), F32),
        compiler_params=_params(("arbitrary", "arbitrary")),
        name="attn_sample",
    )(pt_flat, q_rows, k_new, v_new, *lams, subln_w, *([ck] * n_chunk), *([cv] * n_chunk))


def _ln_swish(y, lw, lb):
    mu = jnp.mean(y, axis=-1, keepdims=True)
    yc = y - mu
    var = jnp.mean(yc * yc, axis=-1, keepdims=True)
    z = yc * lax.rsqrt(var + EPS) * lw + lb
    return z * jax.nn.sigmoid(z)


def _conv_prompt_body(u_ref, up_ref, dw_ref, db_ref, lw_ref, lb_ref, c_ref, buf_ref, y_ref,
                      *, tt, width, halo, rb):
    t = pl.program_id(1)
    buf_ref[0:halo, :] = jnp.where(t == 0, 0.0, up_ref[...])
    buf_ref[halo:halo + tt, :] = u_ref[...]
    off = halo - (width - 1)

    def col_body(cc, carry):
        c0 = pl.multiple_of(cc * LANES, LANES)
        wts = dw_ref[:, pl.ds(c0, LANES)]
        bias = db_ref[:, pl.ds(c0, LANES)]
        for r in range(tt // rb):
            acc = jnp.zeros((rb, LANES), F32)
            for w in range(width):
                lo = r * rb + off + w
                acc = acc + buf_ref[lo:lo + rb, pl.ds(c0, LANES)] * wts[w:w + 1, :]
            y_ref[r * rb:(r + 1) * rb, pl.ds(c0, LANES)] = acc + bias
        return carry

    lax.fori_loop(0, u_ref.shape[1] // LANES, col_body, 0)
    c_ref[...] = _ln_swish(y_ref[...], lw_ref[...], lb_ref[...]).astype(c_ref.dtype)


def _conv_prompt(u, dw_w, dw_b, ln_w, ln_b, *, batch, seq, tt):
    n, ch = u.shape
    width = dw_w.shape[0]
    halo = 32
    nt = seq // tt
    per = tt // halo
    row = pl.BlockSpec((1, ch), lambda b, t: (0, 0))
    return pl.pallas_call(
        functools.partial(_conv_prompt_body, tt=tt, width=width, halo=halo, rb=64),
        grid=(batch, nt),
        in_specs=[
            pl.BlockSpec((tt, ch), lambda b, t: (b * nt + t, 0)),
            pl.BlockSpec((halo, ch), lambda b, t: (jnp.maximum((b * nt + t) * per - 1, 0), 0)),
            pl.BlockSpec((width, ch), lambda b, t: (0, 0)),
            row, row, row,
        ],
        out_specs=pl.BlockSpec((tt, ch), lambda b, t: (b * nt + t, 0)),
        out_shape=jax.ShapeDtypeStruct((n, ch), BF16),
        scratch_shapes=[pltpu.VMEM((halo + tt, ch), F32), pltpu.VMEM((tt, ch), F32)],
        compiler_params=_params(("arbitrary", "arbitrary")),
        name="conv_prompt",
    )(u, u, dw_w, dw_b, ln_w, ln_b)


def _conv_sample_body(up_ref, dw_ref, db_ref, lw_ref, lb_ref, c_ref, y_ref, *, width, n_new):
    length, tb, ch = up_ref.shape

    def col_body(cc, carry):
        c0 = pl.multiple_of(cc * LANES, LANES)
        wts = dw_ref[:, pl.ds(c0, LANES)]
        bias = db_ref[:, pl.ds(c0, LANES)]
        accs = [jnp.zeros((tb, LANES), F32) for _ in range(n_new)]
        for j in range(length):
            x = up_ref[j, :, pl.ds(c0, LANES)]
            for t in range(n_new):
                w = j - t
                if 0 <= w < width:
                    accs[t] = accs[t] + x * wts[w:w + 1, :]
        for t in range(n_new):
            y_ref[t, :, pl.ds(c0, LANES)] = accs[t] + bias
        return carry

    lax.fori_loop(0, ch // LANES, col_body, 0)
    c_ref[...] = _ln_swish(y_ref[...], lw_ref[...], lb_ref[...]).astype(c_ref.dtype)


def _conv_sample(u_pad_t, dw_w, dw_b, ln_w, ln_b, *, n_new, tb):
    length, dec, ch = u_pad_t.shape
    width = dw_w.shape[0]
    row = pl.BlockSpec((1, ch), lambda b: (0, 0))
    return pl.pallas_call(
        functools.partial(_conv_sample_body, width=width, n_new=n_new),
        grid=(dec // tb,),
        in_specs=[
            pl.BlockSpec((length, tb, ch), lambda b: (0, b, 0)),
            pl.BlockSpec((width, ch), lambda b: (0, 0)),
            row, row, row,
        ],
        out_specs=pl.BlockSpec((n_new, tb, ch), lambda b: (0, b, 0)),
        out_shape=jax.ShapeDtypeStruct((n_new, dec, ch), BF16),
        scratch_shapes=[pltpu.VMEM((n_new, tb, ch), F32)],
        compiler_params=_params(("arbitrary",)),
        name="conv_sample",
    )(u_pad_t, dw_w, dw_b, ln_w, ln_b)


def _wo_body(o_ref, c_ref, x_ref, wo_ref, nw_ref, wr_ref, br_ref, *rest, aw, top_k):
    x1_ref, xf_ref, idx_ref, gate_ref = rest[-4:]
    attn = (jnp.dot(o_ref[...], wo_ref[0:aw, :], preferred_element_type=F32)
            + jnp.dot(c_ref[...], wo_ref[aw:, :], preferred_element_type=F32))
    x1 = x_ref[...] + attn
    x1_ref[...] = x1
    xf = _rms(x1, nw_ref[...])
    xf_ref[...] = xf
    xh = xf.astype(BF16)
    xl = (xf - xh.astype(F32)).astype(BF16)
    wr = wr_ref[...]
    wh = wr.astype(BF16)
    wl = (wr - wh.astype(F32)).astype(BF16)
    logits = (jnp.dot(xh, wh, preferred_element_type=F32) + jnp.dot(xl, wh, preferred_element_type=F32)
              + jnp.dot(xh, wl, preferred_element_type=F32)) + br_ref[...]
    n_exp = logits.shape[1]
    lane_e = lax.broadcasted_iota(I32, logits.shape, 1)
    lane_o = lax.broadcasted_iota(I32, idx_ref.shape, 1)
    vals = logits
    tops, idxs = [], []
    for _ in range(top_k):
        mx = jnp.max(vals, axis=1, keepdims=True)
        am = jnp.min(jnp.where(vals == mx, lane_e, n_exp), axis=1, keepdims=True)
        tops.append(mx)
        idxs.append(am)
        vals = jnp.where(lane_e == am, -jnp.inf, vals)
    es = [jnp.exp(t - tops[0]) for t in tops]
    den = es[0]
    for e in es[1:]:
        den = den + e
    idx_out = jnp.zeros(idx_ref.shape, I32)
    gate_out = jnp.zeros(gate_ref.shape, F32)
    for k in range(top_k):
        idx_out = jnp.where(lane_o == k, idxs[k], idx_out)
        gate_out = jnp.where(lane_o == k, es[k] / den, gate_out)
    idx_ref[...] = idx_out
    gate_ref[...] = gate_out


def _wo_router(o, c, x2d, w_o_bf, norm_w, w_router, b_router, *, tt, xf_rows, xf_row0=0, xf_buf=None):
    n, d = x2d.shape
    aw = o.shape[1]
    n_exp = w_router.shape[1]
    blk0 = xf_row0 // tt
    row = pl.BlockSpec((tt, d), lambda i: (i, 0))
    half = pl.BlockSpec((tt, aw), lambda i: (i, 0))
    small = pl.BlockSpec((tt, LANES), lambda i: (i, 0))
    in_specs = [
        half, pl.BlockSpec((tt, d - aw), lambda i: (i, 0)), row,
        pl.BlockSpec((d, d), lambda i: (0, 0)),
        pl.BlockSpec((1, d), lambda i: (0, 0)),
        pl.BlockSpec((d, n_exp), lambda i: (0, 0)),
        pl.BlockSpec((1, n_exp), lambda i: (0, 0)),
    ]
    args = [o, c, x2d, w_o_bf, norm_w, w_router, b_router]
    aliases = {}
    if xf_buf is not None:
        in_specs.append(pl.BlockSpec(memory_space=pl.ANY))
        args.append(xf_buf)
        aliases = {len(args) - 1: 1}
    return pl.pallas_call(
        functools.partial(_wo_body, aw=aw, top_k=TOP_K),
        grid=(n // tt,),
        in_specs=in_specs,
        out_specs=[row, pl.BlockSpec((tt, d), lambda i: (blk0 + i, 0)), small, small],
        out_shape=[
            jax.ShapeDtypeStruct((n, d), F32),
            jax.ShapeDtypeStruct((xf_rows, d), F32),
            jax.ShapeDtypeStruct((n, LANES), I32),
            jax.ShapeDtypeStruct((n, LANES), F32),
        ],
        input_output_aliases=aliases,
        compiler_params=_params(("arbitrary",)),
        name="wo_router",
    )(*args)


def _route_plan(idx, n_exp, tm, ts):
    n_tok, top_k = idx.shape
    n_asg = n_tok * top_k
    nsub = tm // ts
    flat_e = idx.reshape(-1)
    onehot = (flat_e[:, None] == jnp.arange(n_exp, dtype=I32)[None, :]).astype(I32)
    csum = jnp.cumsum(onehot, axis=0)
    rank = jnp.sum((csum - onehot) * onehot, axis=1)
    counts = csum[-1]
    ntile = (counts + tm - 1) // tm
    first_rows = counts - jnp.maximum(ntile - 1, 0) * tm
    tile_end = jnp.cumsum(ntile)
    tile_beg = tile_end - ntile
    fr = first_rows[flat_e]
    dest = (tile_beg[flat_e] * tm + jnp.where(rank < fr, rank, rank - fr + tm)).astype(I32)

    n_tiles = -(-n_asg // tm) + n_exp
    t_ids = jnp.arange(n_tiles, dtype=I32)
    tile_exp = jnp.minimum(jnp.sum(t_ids[:, None] >= tile_end[None, :], axis=1), n_exp - 1).astype(I32)
    n_used = tile_end[-1]
    first_sub = (first_rows[tile_exp] + ts - 1) // ts
    tile_nsub = jnp.where(t_ids < n_used, jnp.where(t_ids == tile_beg[tile_exp], first_sub, nsub), 0)
    tile_nsub = tile_nsub.astype(I32)

    flat_tok = jnp.arange(n_asg, dtype=I32) // top_k
    buf_tok = jnp.zeros((n_tiles * tm,), I32).at[dest].set(flat_tok, unique_indices=True)
    return dest, tile_exp, tile_nsub, n_used.astype(I32).reshape(1), buf_tok


def _gather_body(tok_ref, tns_ref, nused_ref, x_hbm, xs_ref, land_ref, sem, *, tm, ts):
    t = pl.program_id(0)
    n = nused_ref[0]

    def row_copy(tile, slot, r):
        tok = tok_ref[tile * tm + r]
        return pltpu.make_async_copy(x_hbm.at[pl.ds(tok, 1)], land_ref.at[slot, pl.ds(r, 1)],
                                     sem.at[slot])

    def start_all(tile, slot):
        def body(g, carry):
            for u in range(GATHER_UNROLL):
                row_copy(tile, slot, g * GATHER_UNROLL + u).start()
            return carry
        lax.fori_loop(0, tns_ref[tile] * (ts // GATHER_UNROLL), body, 0)

    def wait_all(tile, slot):
        def body(g, carry):
            for u in range(GATHER_UNROLL):
                row_copy(tile, slot, g * GATHER_UNROLL + u).wait()
            return carry
        lax.fori_loop(0, tns_ref[tile] * (ts // GATHER_UNROLL), body, 0)

    @pl.when(t == 0)
    def _():
        start_all(0, 0)

    @pl.when(t + 1 < n)
    def _():
        start_all(t + 1, (t + 1) % 2)

    @pl.when(t < n)
    def _():
        slot = t % 2
        wait_all(t, slot)

        def convert(j, carry):
            r0 = pl.multiple_of(j * ts, ts)
            xs_ref[pl.ds(r0, ts), :] = land_ref[slot, pl.ds(r0, ts), :].astype(xs_ref.dtype)
            return carry

        lax.fori_loop(0, tns_ref[t], convert, 0)


def _gather_rows(xf, buf_tok, tile_nsub, n_used, *, tm, ts):
    d = xf.shape[1]
    n_rows = buf_tok.shape[0]
    grid_spec = pltpu.PrefetchScalarGridSpec(
        num_scalar_prefetch=3,
        grid=(n_rows // tm,),
        in_specs=[pl.BlockSpec(memory_space=pl.ANY)],
        out_specs=pl.BlockSpec((tm, d), lambda t, tok, tns, nu: (jnp.minimum(t, nu[0] - 1), 0)),
        scratch_shapes=[pltpu.VMEM((2, tm, d), xf.dtype), pltpu.SemaphoreType.DMA((2,))],
    )
    return pl.pallas_call(
        functools.partial(_gather_body, tm=tm, ts=ts),
        grid_spec=grid_spec,
        out_shape=jax.ShapeDtypeStruct((n_rows, d), BF16),
        compiler_params=_params(("arbitrary",)),
        name="moe_gather",
    )(buf_tok, tile_nsub, n_used, xf)


def _expert_changed(texp_ref, t):
    return (t == 0) | (texp_ref[t] != texp_ref[jnp.maximum(t - 1, 0)])


def _moe_up_body(texp_ref, tns_ref, nused_ref, x_ref, wg_ref, wu_ref, bg_ref, bu_ref, a_ref, wbf_ref,
                 *, ts):
    t = pl.program_id(1)

    @pl.when(t < nused_ref[0])
    def _():
        @pl.when(_expert_changed(texp_ref, t))
        def _():
            wbf_ref[0] = wg_ref[0].astype(BF16)
            wbf_ref[1] = wu_ref[0].astype(BF16)

        def sub(j, carry):
            r0 = pl.multiple_of(j * ts, ts)
            x = x_ref[pl.ds(r0, ts), :]
            g = jnp.dot(x, wbf_ref[0], preferred_element_type=F32) + bg_ref[0]
            lin = jnp.dot(x, wbf_ref[1], preferred_element_type=F32) + bu_ref[0]
            g = jnp.minimum(g, SWIGLU_LIMIT)
            lin = jnp.clip(lin, -SWIGLU_LIMIT, SWIGLU_LIMIT)
            a = g * jax.nn.sigmoid(SWIGLU_ALPHA * g) * (lin + 1.0)
            a_ref[pl.ds(r0, ts), :] = a.astype(a_ref.dtype)
            return carry

        lax.fori_loop(0, tns_ref[t], sub, 0)


def _moe_down_body(texp_ref, tns_ref, nused_ref, a_ref, wd_ref, bd_ref, y_ref, wbf_ref, *, ts):
    t = pl.program_id(1)

    @pl.when(t < nused_ref[0])
    def _():
        @pl.when(_expert_changed(texp_ref, t))
        def _():
            wbf_ref[...] = wd_ref[0].astype(BF16)

        def sub(j, carry):
            r0 = pl.multiple_of(j * ts, ts)
            y_ref[pl.ds(r0, ts), :] = jnp.dot(a_ref[pl.ds(r0, ts), :], wbf_ref[...],
                                              preferred_element_type=F32) + bd_ref[0]
            return carry

        lax.fori_loop(0, tns_ref[t], sub, 0)


def _moe_experts(xs, tile_exp, tile_nsub, n_used, w_gate_up, b_gate_up, w_down, b_down, *, tm, ts, tf, tn):
    n_rows, d = xs.shape
    n_exp, _, two_ff = w_gate_up.shape
    d_ff = two_ff // 2
    n_f = d_ff // tf
    n_t = n_rows // tm
    bgu = b_gate_up.reshape(n_exp, 1, two_ff)
    bdn = b_down.reshape(n_exp, 1, d)

    def tile(t, nu):
        return jnp.minimum(t, nu[0] - 1)

    up_spec = pltpu.PrefetchScalarGridSpec(
        num_scalar_prefetch=3,
        grid=(n_f, n_t),
        in_specs=[
            pl.BlockSpec((tm, d), lambda f, t, te, ns, nu: (tile(t, nu), 0)),
            pl.BlockSpec((1, d, tf), lambda f, t, te, ns, nu: (te[tile(t, nu)], 0, f)),
            pl.BlockSpec((1, d, tf), lambda f, t, te, ns, nu: (te[tile(t, nu)], 0, n_f + f)),
            pl.BlockSpec((1, 1, tf), lambda f, t, te, ns, nu: (te[tile(t, nu)], 0, f)),
            pl.BlockSpec((1, 1, tf), lambda f, t, te, ns, nu: (te[tile(t, nu)], 0, n_f + f)),
        ],
        out_specs=pl.BlockSpec((tm, tf), lambda f, t, te, ns, nu: (tile(t, nu), f)),
        scratch_shapes=[pltpu.VMEM((2, d, tf), BF16)],
    )
    act = pl.pallas_call(
        functools.partial(_moe_up_body, ts=ts),
        grid_spec=up_spec,
        out_shape=jax.ShapeDtypeStruct((n_rows, d_ff), BF16),
        compiler_params=_params(("arbitrary", "arbitrary")),
        name="moe_up",
    )(tile_exp, tile_nsub, n_used, xs, w_gate_up, w_gate_up, bgu, bgu)

    n_n = d // tn
    down_spec = pltpu.PrefetchScalarGridSpec(
        num_scalar_prefetch=3,
        grid=(n_n, n_t),
        in_specs=[
            pl.BlockSpec((tm, d_ff), lambda n, t, te, ns, nu: (tile(t, nu), 0)),
            pl.BlockSpec((1, d_ff, tn), lambda n, t, te, ns, nu: (te[tile(t, nu)], 0, n)),
            pl.BlockSpec((1, 1, tn), lambda n, t, te, ns, nu: (te[tile(t, nu)], 0, n)),
        ],
        out_specs=pl.BlockSpec((tm, tn), lambda n, t, te, ns, nu: (tile(t, nu), n)),
        scratch_shapes=[pltpu.VMEM((d_ff, tn), BF16)],
    )
    return pl.pallas_call(
        functools.partial(_moe_down_body, ts=ts),
        grid_spec=down_spec,
        out_shape=jax.ShapeDtypeStruct((n_rows, d), F32),
        compiler_params=_params(("arbitrary", "arbitrary")),
        name="moe_down",
    )(tile_exp, tile_nsub, n_used, act, w_down, bdn)


def _combine_body(dest_ref, x1_ref, gate_ref, nw_ref, yb_hbm, out_ref, rows_ref, sem, *, tt, top_k):
    i = pl.program_id(0)

    def row_copy(r, k):
        d = dest_ref[(i * tt + r) * top_k + k]
        return pltpu.make_async_copy(yb_hbm.at[pl.ds(d, 1)], rows_ref.at[k, pl.ds(r, 1)], sem)

    rows_per_iter = GATHER_UNROLL // top_k

    def issue(g, carry):
        for u in range(rows_per_iter):
            for k in range(top_k):
                row_copy(g * rows_per_iter + u, k).start()
        return carry

    def drain(g, carry):
        for u in range(rows_per_iter):
            for k in range(top_k):
                row_copy(g * rows_per_iter + u, k).wait()
        return carry

    lax.fori_loop(0, tt // rows_per_iter, issue, 0)
    lax.fori_loop(0, tt // rows_per_iter, drain, 0)
    gate = gate_ref[...]
    moe = rows_ref[0] * gate[:, 0:1]
    for k in range(1, top_k):
        moe = moe + rows_ref[k] * gate[:, k:k + 1]
    out_ref[...] = _rms(x1_ref[...] + moe, nw_ref[...])


def _combine(dest, x1, gate, norm_w, yb, *, tt):
    n, d = x1.shape
    grid_spec = pltpu.PrefetchScalarGridSpec(
        num_scalar_prefetch=1,
        grid=(n // tt,),
        in_specs=[
            pl.BlockSpec((tt, d), lambda i, ds: (i, 0)),
            pl.BlockSpec((tt, LANES), lambda i, ds: (i, 0)),
            pl.BlockSpec((1, d), lambda i, ds: (0, 0)),
            pl.BlockSpec(memory_space=pl.ANY),
        ],
        out_specs=pl.BlockSpec((tt, d), lambda i, ds: (i, 0)),
        scratch_shapes=[pltpu.VMEM((TOP_K, tt, d), F32), pltpu.SemaphoreType.DMA],
    )
    return pl.pallas_call(
        functools.partial(_combine_body, tt=tt, top_k=TOP_K),
        grid_spec=grid_spec,
        out_shape=jax.ShapeDtypeStruct((n, d), F32),
        compiler_params=_params(("arbitrary",)),
        name="moe_combine",
    )(dest, x1, gate, norm_w, yb)


def _tile(n, pref):
    t = min(n, pref)
    while n % t:
        t //= 2
    return t


def kernel(x_prompt, x_sample, cache_k, cache_v, state_conv, page_table, norm_mix_w, w_in,
           lambda_q1, lambda_k1, lambda_q2, lambda_k2, subln_w, conv_dw_w, conv_dw_b,
           conv_norm_w, conv_norm_b, w_o, norm_ffn_w, w_router, b_router, w_gate_up, b_gate_up,
           w_down, b_down, norm_final_w):
    batch, seq, d = x_prompt.shape
    dec, n_new, _ = x_sample.shape
    depth, n_pool, page, n_maps, d_qk = cache_k.shape
    n_heads, d_v = cache_v.shape[3:]
    aw = n_heads * d_v
    ch = d - aw
    n_pages = page_table.shape[1]
    past = n_pages * page
    rope_dim = d_qk // 4
    half = rope_dim // 2
    scale = d_qk ** -0.5
    n_exp = w_router.shape[2]
    n_p, n_s = batch * seq, dec * n_new

    h_p = x_prompt.reshape(n_p, d)
    h_s = x_sample.reshape(n_s, d)
    tab_p = _rope_tables(jnp.arange(seq, dtype=I32), d_qk, rope_dim)
    tab_s = _rope_tables(past + jnp.arange(n_s, dtype=I32) % n_new, d_qk, rope_dim)
    outs = {k: [] for k in ("kp", "vp", "cp", "ks", "vs", "cs")}

    for l in range(depth):
        lam_init = 0.8 - 0.6 * math.exp(-0.3 * l)
        lams = tuple(v[l].reshape(1, d_qk) for v in (lambda_q1, lambda_k1, lambda_q2, lambda_k2))
        sw = subln_w[l].reshape(1, d_v)
        w_in_bf = w_in[l].astype(BF16)
        w_o_bf = w_o[l].astype(BF16)
        nmw = norm_mix_w[l].reshape(1, d)
        dw_b = conv_dw_b[l].reshape(1, ch)
        ln_w = conv_norm_w[l].reshape(1, ch)
        ln_b = conv_norm_b[l].reshape(1, ch)

        tt_p = _tile(seq, 512)
        q_p, k_p, v_p, u_p = _proj(h_p, nmw, w_in_bf, tab_p, scale=scale, half=half, tt=tt_p)
        tt_s = _tile(n_s, 512)
        q_s, k_s, v_s, u_s = _proj(h_s, nmw, w_in_bf, tab_s, scale=scale, half=half, tt=tt_s)

        o_p = _attn_prompt(q_p, k_p, v_p, lams, sw, batch=batch, seq=seq, n_heads=n_heads,
                           d_qk=d_qk, lam_init=lam_init, tq=_tile(seq, 512))
        q_rows = q_s.reshape(dec, n_new, n_maps, d_qk).transpose(0, 2, 1, 3)
        q_rows = jnp.pad(q_rows, ((0, 0), (0, 0), (0, QR - n_new), (0, 0))).reshape(dec, n_maps * QR, d_qk)
        ck = cache_k.transpose(0, 1, 3, 4, 2).reshape(depth * n_pool, n_maps * d_qk, page)
        cv = cache_v.reshape(depth * n_pool, page * n_heads, d_v)
        o_s = _attn_sample(page_table + l * n_pool, q_rows, k_s.reshape(dec, n_new * n_maps, d_qk),
                           v_s.reshape(dec, n_new * n_heads, d_v), ck, cv, lams, sw, n_new=n_new,
                           n_maps=n_maps, n_heads=n_heads, qr=QR, lam_init=lam_init,
                           n_chunk=_tile(n_pages, 16))
        o_s = o_s.reshape(n_s, aw).astype(BF16)

        c_p = _conv_prompt(u_p, conv_dw_w[l], dw_b, ln_w, ln_b, batch=batch, seq=seq,
                           tt=_tile(seq, 256))
        u_s3 = u_s.reshape(dec, n_new, ch)
        u_pad_s = jnp.concatenate([state_conv[l], u_s3], axis=1)
        c_s = _conv_sample(u_pad_s.transpose(1, 0, 2), conv_dw_w[l], dw_b, ln_w, ln_b,
                           n_new=n_new, tb=_tile(dec, 16))
        c_s = c_s.transpose(1, 0, 2).reshape(n_s, ch)

        nfw = norm_ffn_w[l].reshape(1, d)
        br = b_router[l].reshape(1, n_exp)
        assert n_p % tt_s == 0
        x1_p, xf_all, idx_p, gate_p = _wo_router(o_p, c_p, h_p, w_o_bf, nfw, w_router[l], br, tt=tt_p,
                                                 xf_rows=n_p + n_s)
        x1_s, xf_all, idx_s, gate_s = _wo_router(o_s, c_s, h_s, w_o_bf, nfw, w_router[l], br, tt=tt_s,
                                                 xf_rows=n_p + n_s, xf_row0=n_p, xf_buf=xf_all)

        idx_all = jnp.concatenate([idx_p[:, :TOP_K], idx_s[:, :TOP_K]], axis=0)
        dest, tile_exp, tile_nsub, n_used, buf_tok = _route_plan(idx_all, n_exp, MOE_TILE, MOE_SUBTILE)
        xs = _gather_rows(xf_all, buf_tok, tile_nsub, n_used, tm=MOE_TILE, ts=MOE_SUBTILE)
        yb = _moe_experts(xs, tile_exp, tile_nsub, n_used, w_gate_up[l], b_gate_up[l], w_down[l],
                          b_down[l], tm=MOE_TILE, ts=MOE_SUBTILE, tf=_tile(w_down.shape[2], 1024),
                          tn=_tile(d, 1024))

        last = l == depth - 1
        nw_out = norm_final_w.reshape(1, d) if last else jnp.ones((1, d), F32)
        assert last, "the combine kernel fuses the final norm; deeper stacks need an un-normed variant"
        h_p = _combine(dest[:n_p * TOP_K], x1_p, gate_p, nw_out, yb, tt=_tile(n_p, 512))
        h_s = _combine(dest[n_p * TOP_K:], x1_s, gate_s, nw_out, yb, tt=_tile(n_s, 512))

        outs["kp"].append(k_p.reshape(batch, seq, n_maps, d_qk))
        outs["vp"].append(v_p.reshape(batch, seq, n_heads, d_v))
        outs["cp"].append(u_p.reshape(batch, seq, ch)[:, seq - (conv_dw_w.shape[1] - 1):])
        outs["ks"].append(k_s.reshape(dec, n_new, n_maps, d_qk))
        outs["vs"].append(v_s.reshape(dec, n_new, n_heads, d_v))
        outs["cs"].append(u_pad_s[:, n_new:])

    return (h_p.reshape(batch, seq, d), h_s.reshape(dec, n_new, d),
            jnp.stack(outs["kp"]), jnp.stack(outs["vp"]), jnp.stack(outs["cp"]),
            jnp.stack(outs["ks"]), jnp.stack(outs["vs"]), jnp.stack(outs["cs"]))
```

```python
import functools
import math

import jax
import jax.numpy as jnp
from jax import lax
from jax.experimental import pallas as pl
from jax.experimental.pallas import tpu as pltpu

F32 = jnp.float32
BF16 = jnp.bfloat16
I32 = jnp.int32

EPS = 1e-5
ROPE_THETA = 500000.0
TOP_K = 4
SWIGLU_LIMIT = 7.0
SWIGLU_ALPHA = 1.702
LANES = 128
VMEM_LIMIT = 56 * 1024 * 1024
QR = 8
MOE_TILE = 768
MOE_SUBTILE = 256
GATHER_UNROLL = 8


def _params(sem, vmem=VMEM_LIMIT):
    return pltpu.CompilerParams(dimension_semantics=sem, vmem_limit_bytes=vmem)


def _nt_dot(a, b):
    return lax.dot_general(a, b, (((1,), (1,)), ((), ())), preferred_element_type=F32)


def _rms(x, w):
    ms = jnp.mean(x * x, axis=-1, keepdims=True)
    return x * lax.rsqrt(ms + EPS) * w


def _lambda(lq1, lk1, lq2, lk2, lam_init):
    a = jnp.sum(lq1 * lk1, axis=-1, keepdims=True)
    b = jnp.sum(lq2 * lk2, axis=-1, keepdims=True)
    return jnp.exp(a) - jnp.exp(b) + lam_init


def _proj_body(x_ref, nw_ref, wa_ref, wb_ref, cs_ref, s1_ref, s2_ref,
               q_ref, k_ref, v_ref, u_ref, xn_ref, *, scale, half):
    j = pl.program_id(1)

    @pl.when(j == 0)
    def _():
        xn_ref[...] = _rms(x_ref[...], nw_ref[...]).astype(BF16)

    def rope_store(dst_ref, p, mul):
        cs, s1, s2 = cs_ref[...], s1_ref[...], s2_ref[...]
        for c in range(p.shape[1] // LANES):
            seg = p[:, c * LANES:(c + 1) * LANES]
            r = (seg * cs + pltpu.roll(seg, LANES - half, 1) * s1
                 + pltpu.roll(seg, half, 1) * s2)
            if mul is not None:
                r = r * mul
            dst_ref[:, c * LANES:(c + 1) * LANES] = r.astype(dst_ref.dtype)

    @pl.when(j == 0)
    def _():
        p = jnp.dot(xn_ref[...], wa_ref[...], preferred_element_type=F32)
        rope_store(q_ref, p, scale)

    @pl.when(j == 1)
    def _():
        p = jnp.dot(xn_ref[...], wa_ref[...], preferred_element_type=F32)
        rope_store(k_ref, p, None)

    @pl.when(j == 2)
    def _():
        v_ref[...] = jnp.dot(xn_ref[...], wa_ref[...], preferred_element_type=F32)

    @pl.when(j == 3)
    def _():
        xn = xn_ref[...]
        val = jnp.dot(xn, wa_ref[...], preferred_element_type=F32)
        gate = jnp.dot(xn, wb_ref[...], preferred_element_type=F32)
        u_ref[...] = val * jax.nn.sigmoid(gate)


def _proj(x2d, norm_w, w_in_bf, tables, *, scale, half, tt):
    n, d = x2d.shape
    w = w_in_bf.shape[1] // 5
    cs, s1, s2 = tables
    n_pos_blocks = cs.shape[0] // tt
    tab_spec = pl.BlockSpec((tt, LANES), lambda i, j: (i % n_pos_blocks, 0))
    out_spec = pl.BlockSpec((tt, w), lambda i, j: (i, 0))
    return pl.pallas_call(
        functools.partial(_proj_body, scale=scale, half=half),
        grid=(n // tt, 4),
        in_specs=[
            pl.BlockSpec((tt, d), lambda i, j: (i, 0)),
            pl.BlockSpec((1, d), lambda i, j: (0, 0)),
            pl.BlockSpec((d, w), lambda i, j: (0, j)),
            pl.BlockSpec((d, w), lambda i, j: (0, 4)),
            tab_spec, tab_spec, tab_spec,
        ],
        out_specs=[out_spec, out_spec, out_spec, out_spec],
        out_shape=[
            jax.ShapeDtypeStruct((n, w), BF16),
            jax.ShapeDtypeStruct((n, w), F32),
            jax.ShapeDtypeStruct((n, w), F32),
            jax.ShapeDtypeStruct((n, w), F32),
        ],
        scratch_shapes=[pltpu.VMEM((tt, d), BF16)],
        compiler_params=_params(("arbitrary", "arbitrary")),
        name="proj",
    )(x2d, norm_w, w_in_bf, w_in_bf, cs, s1, s2)


def _rope_tables(pos, d_qk, rope_dim):
    half = rope_dim // 2
    inv = ROPE_THETA ** (-jnp.arange(0, rope_dim, 2, dtype=F32) / rope_dim)
    ang = pos.astype(F32)[:, None] * inv[None, :]
    cos, sin = jnp.cos(ang), jnp.sin(ang)
    p = pos.shape[0]
    zh = jnp.zeros((p, half), F32)
    zr = jnp.zeros((p, d_qk - rope_dim), F32)
    cs = jnp.concatenate([cos, cos, jnp.ones((p, d_qk - rope_dim), F32)], axis=-1)
    s1 = jnp.concatenate([-sin, zh, zr], axis=-1)
    s2 = jnp.concatenate([zh, sin, zr], axis=-1)
    rep = LANES // d_qk
    return tuple(jnp.tile(t, (1, rep)) for t in (cs, s1, s2))


def _attn_prompt_body(q_ref, k_ref, v_ref, lq1_ref, lk1_ref, lq2_ref, lk2_ref, sw_ref,
                      o_ref, kb_ref, vb_ref, m_ref, l_ref, acc_ref, *, tq, tk, d_qk, lam_init):
    qi = pl.program_id(2)

    @pl.when(qi == 0)
    def _():
        kb_ref[...] = k_ref[...].astype(BF16)
        vb_ref[...] = v_ref[...].astype(BF16)

    q = q_ref[...]
    lane = lax.broadcasted_iota(I32, q.shape, 1)
    zero = jnp.zeros_like(q)
    qm = (jnp.where(lane < d_qk, q, zero), jnp.where(lane >= d_qk, q, zero))
    m_ref[...] = jnp.full(m_ref.shape, -jnp.inf, F32)
    l_ref[...] = jnp.zeros(l_ref.shape, F32)
    acc_ref[...] = jnp.zeros(acc_ref.shape, F32)

    def step(c, masked):
        k0 = pl.multiple_of(c * tk, tk)
        kc = kb_ref[pl.ds(k0, tk), :]
        vc = vb_ref[pl.ds(k0, tk), :]
        for mi in range(2):
            s = _nt_dot(qm[mi], kc)
            if masked:
                row = lax.broadcasted_iota(I32, (tq, tk), 0)
                col = lax.broadcasted_iota(I32, (tq, tk), 1)
                s = jnp.where(col + c * tk <= row + qi * tq, s, -jnp.inf)
            m_p = m_ref[mi]
            m_n = jnp.maximum(m_p, jnp.max(s, axis=1, keepdims=True))
            p = jnp.exp(s - jnp.concatenate([m_n] * (tk // LANES), axis=1))
            al = jnp.exp(m_p - m_n)
            l_ref[mi] = al * l_ref[mi] + jnp.sum(p, axis=1, keepdims=True)
            acc_ref[mi] = al * acc_ref[mi] + jnp.dot(p.astype(BF16), vc, preferred_element_type=F32)
            m_ref[mi] = m_n

    def full_step(c, carry):
        step(c, False)
        return carry

    n_full = qi * (tq // tk)
    lax.fori_loop(0, n_full, full_step, 0)
    for dgl in range(tq // tk):
        step(n_full + dgl, True)

    lam = _lambda(lq1_ref[...], lk1_ref[...], lq2_ref[...], lk2_ref[...], lam_init)
    o = acc_ref[0] / l_ref[0] - lam * (acc_ref[1] / l_ref[1])
    o_ref[...] = (_rms(o, sw_ref[...]) * (1.0 - lam_init)).astype(o_ref.dtype)


def _attn_prompt(q, k, v, lams, subln_w, *, batch, seq, n_heads, d_qk, lam_init, tq):
    n, w = q.shape
    d_v = w // n_heads
    nq = seq // tq
    vec = pl.BlockSpec((1, d_qk), lambda b, h, i: (0, 0))
    return pl.pallas_call(
        functools.partial(_attn_prompt_body, tq=tq, tk=tq, d_qk=d_qk, lam_init=lam_init),
        grid=(batch, n_heads, nq),
        in_specs=[
            pl.BlockSpec((tq, d_v), lambda b, h, i: (b * nq + i, h)),
            pl.BlockSpec((seq, d_v), lambda b, h, i: (b, h)),
            pl.BlockSpec((seq, d_v), lambda b, h, i: (b, h)),
            vec, vec, vec, vec,
            pl.BlockSpec((1, d_v), lambda b, h, i: (0, 0)),
        ],
        out_specs=pl.BlockSpec((tq, d_v), lambda b, h, i: (b * nq + i, h)),
        out_shape=jax.ShapeDtypeStruct((n, w), BF16),
        scratch_shapes=[pltpu.VMEM((seq, d_v), BF16), pltpu.VMEM((seq, d_v), BF16),
                        pltpu.VMEM((2, tq, LANES), F32), pltpu.VMEM((2, tq, LANES), F32),
                        pltpu.VMEM((2, tq, d_v), F32)],
        compiler_params=_params(("arbitrary", "arbitrary", "arbitrary")),
        name="attn_prompt",
    )(q, k, v, *lams, subln_w)


def _attn_sample_body(pt_ref, q_ref, kn_ref, vn_ref, lq1_ref, lk1_ref, lq2_ref, lk2_ref, sw_ref,
                      *rest, n_chunk, n_new, n_maps, n_heads, qr, lam_init):
    del pt_ref
    k_refs = rest[:n_chunk]
    v_refs = rest[n_chunk:2 * n_chunk]
    o_ref = rest[2 * n_chunk]
    m_ref, l_ref, acc_ref, qbd_ref = rest[2 * n_chunk + 1:]
    c = pl.program_id(1)
    page = k_refs[0].shape[2]
    d_qk = k_refs[0].shape[1] // n_maps
    hr = 2 * qr

    @pl.when(c == 0)
    def _():
        m_ref[...] = jnp.full(m_ref.shape, -jnp.inf, F32)
        l_ref[...] = jnp.zeros(l_ref.shape, F32)
        acc_ref[...] = jnp.zeros(acc_ref.shape, F32)

    q_all = q_ref[0]

    @pl.when(c == 0)
    def _():
        q_rep = jnp.concatenate([q_all] * n_maps, axis=1)
        row_map = lax.broadcasted_iota(I32, q_rep.shape, 0) // qr
        col_map = lax.broadcasted_iota(I32, q_rep.shape, 1) // d_qk
        qbd_ref[...] = jnp.where(row_map == col_map, q_rep, jnp.zeros_like(q_rep))

    m_p, l_p, acc = m_ref[...], l_ref[...], acc_ref[...]
    k_cat = jnp.concatenate([k_refs[i][0].astype(BF16) for i in range(n_chunk)], axis=1)
    s = jnp.dot(qbd_ref[...], k_cat, preferred_element_type=F32)
    m_n = jnp.maximum(m_p, jnp.max(s, axis=1, keepdims=True))
    p = jnp.exp(s - m_n)
    al = jnp.exp(m_p - m_n)
    l_p = al * l_p + jnp.sum(p, axis=1, keepdims=True)
    pb = p.astype(BF16)
    pv = []
    for h in range(n_heads):
        vh = jnp.concatenate(
            [v_refs[i][0, pl.ds(h, page, stride=n_heads), :].astype(BF16) for i in range(n_chunk)],
            axis=0)
        pv.append(jnp.dot(pb[h * hr:(h + 1) * hr], vh, preferred_element_type=F32))
    acc = al * acc + jnp.concatenate(pv, axis=0)
    m_p = m_n
    m_ref[...] = m_p
    l_ref[...] = l_p
    acc_ref[...] = acc

    @pl.when(c == pl.num_programs(1) - 1)
    def _():
        qf = q_all.astype(F32)
        kn, vn = kn_ref[0], vn_ref[0]
        q_tok = lax.broadcasted_iota(I32, (qf.shape[0], 1), 0) % qr
        s_new = []
        for j in range(n_new):
            parts = []
            for m in range(n_maps):
                krow = kn[j * n_maps + m:j * n_maps + m + 1, :]
                parts.append(jnp.sum(qf[m * qr:(m + 1) * qr] * krow, axis=1, keepdims=True))
            s_new.append(jnp.where(q_tok >= j, jnp.concatenate(parts, axis=0), -jnp.inf))
        m_n = m_p
        for sj in s_new:
            m_n = jnp.maximum(m_n, sj)
        al = jnp.exp(m_p - m_n)
        l_n = al * l_p
        a_n = al * acc
        for j in range(n_new):
            pj = jnp.exp(s_new[j] - m_n)
            l_n = l_n + pj
            a_n = a_n + jnp.concatenate(
                [pj[h * hr:(h + 1) * hr] * vn[j * n_heads + h:j * n_heads + h + 1, :]
                 for h in range(n_heads)], axis=0)
        a_n = a_n / l_n
        lam = _lambda(lq1_ref[...], lk1_ref[...], lq2_ref[...], lk2_ref[...], lam_init)
        d_v = a_n.shape[1]
        for h in range(n_heads):
            o = a_n[h * hr:h * hr + n_new] - lam * a_n[h * hr + qr:h * hr + qr + n_new]
            o_ref[0, :, h * d_v:(h + 1) * d_v] = (
                _rms(o, sw_ref[...]) * (1.0 - lam_init)).astype(o_ref.dtype)


def _attn_sample(pages, q_rows, k_new, v_new, ck, cv, lams, subln_w, *, n_new, n_maps, n_heads, qr,
                 lam_init, n_chunk):
    dec, n_pages = pages.shape
    d_qk = ck.shape[1] // n_maps
    d_v = cv.shape[2]
    rows = n_maps * qr
    pt_flat = pages.reshape(-1)

    def page_map(i, b, c, pt):
        return (pt[b * n_pages + c * n_chunk + i], 0, 0)

    vec = pl.BlockSpec((1, d_qk), lambda b, c, pt: (0, 0))
    k_specs = [pl.BlockSpec((1,) + ck.shape[1:], functools.partial(page_map, i)) for i in range(n_chunk)]
    v_specs = [pl.BlockSpec((1,) + cv.shape[1:], functools.partial(page_map, i)) for i in range(n_chunk)]
    grid_spec = pltpu.PrefetchScalarGridSpec(
        num_scalar_prefetch=1,
        grid=(dec, n_pages // n_chunk),
        in_specs=[
            pl.BlockSpec((1, rows, d_qk), lambda b, c, pt: (b, 0, 0)),
            pl.BlockSpec((1,) + k_new.shape[1:], lambda b, c, pt: (b, 0, 0)),
            pl.BlockSpec((1,) + v_new.shape[1:], lambda b, c, pt: (b, 0, 0)),
            vec, vec, vec, vec,
            pl.BlockSpec((1, d_v), lambda b, c, pt: (0, 0)),
        ] + k_specs + v_specs,
        out_specs=pl.BlockSpec((1, n_new, n_heads * d_v), lambda b, c, pt: (b, 0, 0)),
        scratch_shapes=[pltpu.VMEM((rows, 1), F32), pltpu.VMEM((rows, 1), F32),
                        pltpu.VMEM((rows, d_v), F32), pltpu.VMEM((rows, n_maps * d_qk), BF16)],
    )
    return pl.pallas_call(
        functools.partial(_attn_sample_body, n_chunk=n_chunk, n_new=n_new, n_maps=n_maps,
                          n_heads=n_heads, qr=qr, lam_init=lam_init),
        grid_spec=grid_spec,
        out_shape=jax.ShapeDtypeStruct((dec, n_new, n_heads * d_v), F32),
        compiler_params=_params(("arbitrary", "arbitrary")),
        name="attn_sample",
    )(pt_flat, q_rows, k_new, v_new, *lams, subln_w, *([ck] * n_chunk), *([cv] * n_chunk))


def _ln_swish(y, lw, lb):
    mu = jnp.mean(y, axis=-1, keepdims=True)
    yc = y - mu
    var = jnp.mean(yc * yc, axis=-1, keepdims=True)
    z = yc * lax.rsqrt(var + EPS) * lw + lb
    return z * jax.nn.sigmoid(z)


def _conv_prompt_body(u_ref, up_ref, dw_ref, db_ref, lw_ref, lb_ref, c_ref, buf_ref, y_ref,
                      *, tt, width, halo, rb):
    t = pl.program_id(1)
    buf_ref[0:halo, :] = jnp.where(t == 0, 0.0, up_ref[...])
    buf_ref[halo:halo + tt, :] = u_ref[...]
    off = halo - (width - 1)

    def col_body(cc, carry):
        c0 = pl.multiple_of(cc * LANES, LANES)
        wts = dw_ref[:, pl.ds(c0, LANES)]
        bias = db_ref[:, pl.ds(c0, LANES)]
        for r in range(tt // rb):
            acc = jnp.zeros((rb, LANES), F32)
            for w in range(width):
                lo = r * rb + off + w
                acc = acc + buf_ref[lo:lo + rb, pl.ds(c0, LANES)] * wts[w:w + 1, :]
            y_ref[r * rb:(r + 1) * rb, pl.ds(c0, LANES)] = acc + bias
        return carry

    lax.fori_loop(0, u_ref.shape[1] // LANES, col_body, 0)
    c_ref[...] = _ln_swish(y_ref[...], lw_ref[...], lb_ref[...]).astype(c_ref.dtype)


def _conv_prompt(u, dw_w, dw_b, ln_w, ln_b, *, batch, seq, tt):
    n, ch = u.shape
    width = dw_w.shape[0]
    halo = 32
    nt = seq // tt
    per = tt // halo
    row = pl.BlockSpec((1, ch), lambda b, t: (0, 0))
    return pl.pallas_call(
        functools.partial(_conv_prompt_body, tt=tt, width=width, halo=halo, rb=64),
        grid=(batch, nt),
        in_specs=[
            pl.BlockSpec((tt, ch), lambda b, t: (b * nt + t, 0)),
            pl.BlockSpec((halo, ch), lambda b, t: (jnp.maximum((b * nt + t) * per - 1, 0), 0)),
            pl.BlockSpec((width, ch), lambda b, t: (0, 0)),
            row, row, row,
        ],
        out_specs=pl.BlockSpec((tt, ch), lambda b, t: (b * nt + t, 0)),
        out_shape=jax.ShapeDtypeStruct((n, ch), BF16),
        scratch_shapes=[pltpu.VMEM((halo + tt, ch), F32), pltpu.VMEM((tt, ch), F32)],
        compiler_params=_params(("arbitrary", "arbitrary")),
        name="conv_prompt",
    )(u, u, dw_w, dw_b, ln_w, ln_b)


def _conv_sample_body(up_ref, dw_ref, db_ref, lw_ref, lb_ref, c_ref, y_ref, *, width, n_new):
    length, tb, ch = up_ref.shape

    def col_body(cc, carry):
        c0 = pl.multiple_of(cc * LANES, LANES)
        wts = dw_ref[:, pl.ds(c0, LANES)]
        bias = db_ref[:, pl.ds(c0, LANES)]
        accs = [jnp.zeros((tb, LANES), F32) for _ in range(n_new)]
        for j in range(length):
            x = up_ref[j, :, pl.ds(c0, LANES)]
            for t in range(n_new):
                w = j - t
                if 0 <= w < width:
                    accs[t] = accs[t] + x * wts[w:w + 1, :]
        for t in range(n_new):
            y_ref[t, :, pl.ds(c0, LANES)] = accs[t] + bias
        return carry

    lax.fori_loop(0, ch // LANES, col_body, 0)
    c_ref[...] = _ln_swish(y_ref[...], lw_ref[...], lb_ref[...]).astype(c_ref.dtype)


def _conv_sample(u_pad_t, dw_w, dw_b, ln_w, ln_b, *, n_new, tb):
    length, dec, ch = u_pad_t.shape
    width = dw_w.shape[0]
    row = pl.BlockSpec((1, ch), lambda b: (0, 0))
    return pl.pallas_call(
        functools.partial(_conv_sample_body, width=width, n_new=n_new),
        grid=(dec // tb,),
        in_specs=[
            pl.BlockSpec((length, tb, ch), lambda b: (0, b, 0)),
            pl.BlockSpec((width, ch), lambda b: (0, 0)),
            row, row, row,
        ],
        out_specs=pl.BlockSpec((n_new, tb, ch), lambda b: (0, b, 0)),
        out_shape=jax.ShapeDtypeStruct((n_new, dec, ch), BF16),
        scratch_shapes=[pltpu.VMEM((n_new, tb, ch), F32)],
        compiler_params=_params(("arbitrary",)),
        name="conv_sample",
    )(u_pad_t, dw_w, dw_b, ln_w, ln_b)


def _wo_body(o_ref, c_ref, x_ref, wo_ref, nw_ref, wr_ref, br_ref, *rest, aw, top_k):
    x1_ref, xf_ref, idx_ref, gate_ref = rest[-4:]
    attn = (jnp.dot(o_ref[...], wo_ref[0:aw, :], preferred_element_type=F32)
            + jnp.dot(c_ref[...], wo_ref[aw:, :], preferred_element_type=F32))
    x1 = x_ref[...] + attn
    x1_ref[...] = x1
    xf = _rms(x1, nw_ref[...])
    xf_ref[...] = xf
    xh = xf.astype(BF16)
    xl = (xf - xh.astype(F32)).astype(BF16)
    wr = wr_ref[...]
    wh = wr.astype(BF16)
    wl = (wr - wh.astype(F32)).astype(BF16)
    logits = (jnp.dot(xh, wh, preferred_element_type=F32) + jnp.dot(xl, wh, preferred_element_type=F32)
              + jnp.dot(xh, wl, preferred_element_type=F32)) + br_ref[...]
    n_exp = logits.shape[1]
    lane_e = lax.broadcasted_iota(I32, logits.shape, 1)
    lane_o = lax.broadcasted_iota(I32, idx_ref.shape, 1)
    vals = logits
    tops, idxs = [], []
    for _ in range(top_k):
        mx = jnp.max(vals, axis=1, keepdims=True)
        am = jnp.min(jnp.where(vals == mx, lane_e, n_exp), axis=1, keepdims=True)
        tops.append(mx)
        idxs.append(am)
        vals = jnp.where(lane_e == am, -jnp.inf, vals)
    es = [jnp.exp(t - tops[0]) for t in tops]
    den = es[0]
    for e in es[1:]:
        den = den + e
    idx_out = jnp.zeros(idx_ref.shape, I32)
    gate_out = jnp.zeros(gate_ref.shape, F32)
    for k in range(top_k):
        idx_out = jnp.where(lane_o == k, idxs[k], idx_out)
        gate_out = jnp.where(lane_o == k, es[k] / den, gate_out)
    idx_ref[...] = idx_out
    gate_ref[...] = gate_out


def _wo_router(o, c, x2d, w_o_bf, norm_w, w_router, b_router, *, tt, xf_rows, xf_row0=0, xf_buf=None):
    n, d = x2d.shape
    aw = o.shape[1]
    n_exp = w_router.shape[1]
    blk0 = xf_row0 // tt
    row = pl.BlockSpec((tt, d), lambda i: (i, 0))
    half = pl.BlockSpec((tt, aw), lambda i: (i, 0))
    small = pl.BlockSpec((tt, LANES), lambda i: (i, 0))
    in_specs = [
        half, pl.BlockSpec((tt, d - aw), lambda i: (i, 0)), row,
        pl.BlockSpec((d, d), lambda i: (0, 0)),
        pl.BlockSpec((1, d), lambda i: (0, 0)),
        pl.BlockSpec((d, n_exp), lambda i: (0, 0)),
        pl.BlockSpec((1, n_exp), lambda i: (0, 0)),
    ]
    args = [o, c, x2d, w_o_bf, norm_w, w_router, b_router]
    aliases = {}
    if xf_buf is not None:
        in_specs.append(pl.BlockSpec(memory_space=pl.ANY))
        args.append(xf_buf)
        aliases = {len(args) - 1: 1}
    return pl.pallas_call(
        functools.partial(_wo_body, aw=aw, top_k=TOP_K),
        grid=(n // tt,),
        in_specs=in_specs,
        out_specs=[row, pl.BlockSpec((tt, d), lambda i: (blk0 + i, 0)), small, small],
        out_shape=[
            jax.ShapeDtypeStruct((n, d), F32),
            jax.ShapeDtypeStruct((xf_rows, d), F32),
            jax.ShapeDtypeStruct((n, LANES), I32),
            jax.ShapeDtypeStruct((n, LANES), F32),
        ],
        input_output_aliases=aliases,
        compiler_params=_params(("arbitrary",)),
        name="wo_router",
    )(*args)


def _route_plan(idx, n_exp, tm, ts):
    n_tok, top_k = idx.shape
    n_asg = n_tok * top_k
    nsub = tm // ts
    flat_e = idx.reshape(-1)
    onehot = (flat_e[:, None] == jnp.arange(n_exp, dtype=I32)[None, :]).astype(I32)
    csum = jnp.cumsum(onehot, axis=0)
    rank = jnp.sum((csum - onehot) * onehot, axis=1)
    counts = csum[-1]
    ntile = (counts + tm - 1) // tm
    first_rows = counts - jnp.maximum(ntile - 1, 0) * tm
    tile_end = jnp.cumsum(ntile)
    tile_beg = tile_end - ntile
    fr = first_rows[flat_e]
    dest = (tile_beg[flat_e] * tm + jnp.where(rank < fr, rank, rank - fr + tm)).astype(I32)

    n_tiles = -(-n_asg // tm) + n_exp
    t_ids = jnp.arange(n_tiles, dtype=I32)
    tile_exp = jnp.minimum(jnp.sum(t_ids[:, None] >= tile_end[None, :], axis=1), n_exp - 1).astype(I32)
    n_used = tile_end[-1]
    first_sub = (first_rows[tile_exp] + ts - 1) // ts
    tile_nsub = jnp.where(t_ids < n_used, jnp.where(t_ids == tile_beg[tile_exp], first_sub, nsub), 0)
    tile_nsub = tile_nsub.astype(I32)
    tile_rows = jnp.where(t_ids < n_used, jnp.where(t_ids == tile_beg[tile_exp], first_rows[tile_exp], tm), 0)

    flat_tok = jnp.arange(n_asg, dtype=I32) // top_k
    buf_tok = jnp.zeros((n_tiles * tm,), I32).at[dest].set(flat_tok, unique_indices=True)
    return dest, tile_exp, tile_nsub, tile_rows.astype(I32), n_used.astype(I32).reshape(1), buf_tok


def _gather_body(tok_ref, tns_ref, trows_ref, nused_ref, x_hbm, xs_ref, land_ref, sem, *, tm, ts):
    t = pl.program_id(0)
    n = nused_ref[0]

    def row_copy(tile, slot, r):
        tok = tok_ref[tile * tm + r]
        return pltpu.make_async_copy(x_hbm.at[pl.ds(tok, 1)], land_ref.at[slot, pl.ds(r, 1)],
                                     sem.at[slot])

    def groups(tile):
        return lax.shift_right_logical(trows_ref[tile] + (GATHER_UNROLL - 1), GATHER_UNROLL.bit_length() - 1)

    def start_all(tile, slot):
        def body(g, carry):
            for u in range(GATHER_UNROLL):
                row_copy(tile, slot, g * GATHER_UNROLL + u).start()
            return carry
        lax.fori_loop(0, groups(tile), body, 0)

    def wait_all(tile, slot):
        def body(g, carry):
            for u in range(GATHER_UNROLL):
                row_copy(tile, slot, g * GATHER_UNROLL + u).wait()
            return carry
        lax.fori_loop(0, groups(tile), body, 0)

    @pl.when(t == 0)
    def _():
        land_ref[...] = jnp.zeros(land_ref.shape, land_ref.dtype)
        start_all(0, 0)

    @pl.when(t + 1 < n)
    def _():
        start_all(t + 1, (t + 1) % 2)

    @pl.when(t < n)
    def _():
        slot = t % 2
        wait_all(t, slot)

        def convert(j, carry):
            r0 = pl.multiple_of(j * ts, ts)
            xs_ref[pl.ds(r0, ts), :] = land_ref[slot, pl.ds(r0, ts), :].astype(xs_ref.dtype)
            return carry

        lax.fori_loop(0, tns_ref[t], convert, 0)


def _gather_rows(xf, buf_tok, tile_nsub, tile_rows, n_used, *, tm, ts):
    d = xf.shape[1]
    n_rows = buf_tok.shape[0]
    grid_spec = pltpu.PrefetchScalarGridSpec(
        num_scalar_prefetch=4,
        grid=(n_rows // tm,),
        in_specs=[pl.BlockSpec(memory_space=pl.ANY)],
        out_specs=pl.BlockSpec((tm, d), lambda t, tok, tns, trw, nu: (jnp.minimum(t, nu[0] - 1), 0)),
        scratch_shapes=[pltpu.VMEM((2, tm, d), xf.dtype), pltpu.SemaphoreType.DMA((2,))],
    )
    return pl.pallas_call(
        functools.partial(_gather_body, tm=tm, ts=ts),
        grid_spec=grid_spec,
        out_shape=jax.ShapeDtypeStruct((n_rows, d), BF16),
        compiler_params=_params(("arbitrary",)),
        name="moe_gather",
    )(buf_tok, tile_nsub, tile_rows, n_used, xf)


def _expert_changed(texp_ref, t):
    return (t == 0) | (texp_ref[t] != texp_ref[jnp.maximum(t - 1, 0)])


def _moe_up_body(texp_ref, tns_ref, nused_ref, x_ref, wg_ref, wu_ref, bg_ref, bu_ref, a_ref, wbf_ref,
                 *, ts):
    t = pl.program_id(1)

    @pl.when(t < nused_ref[0])
    def _():
        @pl.when(_expert_changed(texp_ref, t))
        def _():
            wbf_ref[0] = wg_ref[0].astype(BF16)
            wbf_ref[1] = wu_ref[0].astype(BF16)

        def sub(j, carry):
            r0 = pl.multiple_of(j * ts, ts)
            x = x_ref[pl.ds(r0, ts), :]
            g = jnp.dot(x, wbf_ref[0], preferred_element_type=F32) + bg_ref[0]
            lin = jnp.dot(x, wbf_ref[1], preferred_element_type=F32) + bu_ref[0]
            g = jnp.minimum(g, SWIGLU_LIMIT)
            lin = jnp.clip(lin, -SWIGLU_LIMIT, SWIGLU_LIMIT)
            a = g * jax.nn.sigmoid(SWIGLU_ALPHA * g) * (lin + 1.0)
            a_ref[pl.ds(r0, ts), :] = a.astype(a_ref.dtype)
            return carry

        lax.fori_loop(0, tns_ref[t], sub, 0)


def _moe_down_body(texp_ref, tns_ref, nused_ref, a_ref, wd_ref, bd_ref, y_ref, wbf_ref, *, ts):
    t = pl.program_id(1)

    @pl.when(t < nused_ref[0])
    def _():
        @pl.when(_expert_changed(texp_ref, t))
        def _():
            wbf_ref[...] = wd_ref[0].astype(BF16)

        def sub(j, carry):
            r0 = pl.multiple_of(j * ts, ts)
            y_ref[pl.ds(r0, ts), :] = jnp.dot(a_ref[pl.ds(r0, ts), :], wbf_ref[...],
                                              preferred_element_type=F32) + bd_ref[0]
            return carry

        lax.fori_loop(0, tns_ref[t], sub, 0)


def _moe_experts(xs, tile_exp, tile_nsub, n_used, w_gate_up, b_gate_up, w_down, b_down, *, tm, ts, tf, tn):
    n_rows, d = xs.shape
    n_exp, _, two_ff = w_gate_up.shape
    d_ff = two_ff // 2
    n_f = d_ff // tf
    n_t = n_rows // tm
    bgu = b_gate_up.reshape(n_exp, 1, two_ff)
    bdn = b_down.reshape(n_exp, 1, d)

    def tile(t, nu):
        return jnp.minimum(t, nu[0] - 1)

    up_spec = pltpu.PrefetchScalarGridSpec(
        num_scalar_prefetch=3,
        grid=(n_f, n_t),
        in_specs=[
            pl.BlockSpec((tm, d), lambda f, t, te, ns, nu: (tile(t, nu), 0)),
            pl.BlockSpec((1, d, tf), lambda f, t, te, ns, nu: (te[tile(t, nu)], 0, f)),
            pl.BlockSpec((1, d, tf), lambda f, t, te, ns, nu: (te[tile(t, nu)], 0, n_f + f)),
            pl.BlockSpec((1, 1, tf), lambda f, t, te, ns, nu: (te[tile(t, nu)], 0, f)),
            pl.BlockSpec((1, 1, tf), lambda f, t, te, ns, nu: (te[tile(t, nu)], 0, n_f + f)),
        ],
        out_specs=pl.BlockSpec((tm, tf), lambda f, t, te, ns, nu: (tile(t, nu), f)),
        scratch_shapes=[pltpu.VMEM((2, d, tf), BF16)],
    )
    act = pl.pallas_call(
        functools.partial(_moe_up_body, ts=ts),
        grid_spec=up_spec,
        out_shape=jax.ShapeDtypeStruct((n_rows, d_ff), BF16),
        compiler_params=_params(("arbitrary", "arbitrary")),
        name="moe_up",
    )(tile_exp, tile_nsub, n_used, xs, w_gate_up, w_gate_up, bgu, bgu)

    n_n = d // tn
    down_spec = pltpu.PrefetchScalarGridSpec(
        num_scalar_prefetch=3,
        grid=(n_n, n_t),
        in_specs=[
            pl.BlockSpec((tm, d_ff), lambda n, t, te, ns, nu: (tile(t, nu), 0)),
            pl.BlockSpec((1, d_ff, tn), lambda n, t, te, ns, nu: (te[tile(t, nu)], 0, n)),
            pl.BlockSpec((1, 1, tn), lambda n, t, te, ns, nu: (te[tile(t, nu)], 0, n)),
        ],
        out_specs=pl.BlockSpec((tm, tn), lambda n, t, te, ns, nu: (tile(t, nu), n)),
        scratch_shapes=[pltpu.VMEM((d_ff, tn), BF16)],
    )
    return pl.pallas_call(
        functools.partial(_moe_down_body, ts=ts),
        grid_spec=down_spec,
        out_shape=jax.ShapeDtypeStruct((n_rows, d), F32),
        compiler_params=_params(("arbitrary", "arbitrary")),
        name="moe_down",
    )(tile_exp, tile_nsub, n_used, act, w_down, bdn)


def _combine_body(dest_ref, x1_ref, gate_ref, nw_ref, yb_hbm, out_ref, rows_ref, sem, *, tt, top_k):
    i = pl.program_id(0)

    def row_copy(r, k):
        d = dest_ref[(i * tt + r) * top_k + k]
        return pltpu.make_async_copy(yb_hbm.at[pl.ds(d, 1)], rows_ref.at[k, pl.ds(r, 1)], sem)

    rows_per_iter = GATHER_UNROLL // top_k

    def issue(g, carry):
        for u in range(rows_per_iter):
            for k in range(top_k):
                row_copy(g * rows_per_iter + u, k).start()
        return carry

    def drain(g, carry):
        for u in range(rows_per_iter):
            for k in range(top_k):
                row_copy(g * rows_per_iter + u, k).wait()
        return carry

    lax.fori_loop(0, tt // rows_per_iter, issue, 0)
    lax.fori_loop(0, tt // rows_per_iter, drain, 0)
    gate = gate_ref[...]
    moe = rows_ref[0] * gate[:, 0:1]
    for k in range(1, top_k):
        moe = moe + rows_ref[k] * gate[:, k:k + 1]
    out_ref[...] = _rms(x1_ref[...] + moe, nw_ref[...])


def _combine(dest, x1, gate, norm_w, yb, *, tt):
    n, d = x1.shape
    grid_spec = pltpu.PrefetchScalarGridSpec(
        num_scalar_prefetch=1,
        grid=(n // tt,),
        in_specs=[
            pl.BlockSpec((tt, d), lambda i, ds: (i, 0)),
            pl.BlockSpec((tt, LANES), lambda i, ds: (i, 0)),
            pl.BlockSpec((1, d), lambda i, ds: (0, 0)),
            pl.BlockSpec(memory_space=pl.ANY),
        ],
        out_specs=pl.BlockSpec((tt, d), lambda i, ds: (i, 0)),
        scratch_shapes=[pltpu.VMEM((TOP_K, tt, d), F32), pltpu.SemaphoreType.DMA],
    )
    return pl.pallas_call(
        functools.partial(_combine_body, tt=tt, top_k=TOP_K),
        grid_spec=grid_spec,
        out_shape=jax.ShapeDtypeStruct((n, d), F32),
        compiler_params=_params(("arbitrary",)),
        name="moe_combine",
    )(dest, x1, gate, norm_w, yb)


def _tile(n, pref):
    t = min(n, pref)
    while n % t:
        t //= 2
    return t


def kernel(x_prompt, x_sample, cache_k, cache_v, state_conv, page_table, norm_mix_w, w_in,
           lambda_q1, lambda_k1, lambda_q2, lambda_k2, subln_w, conv_dw_w, conv_dw_b,
           conv_norm_w, conv_norm_b, w_o, norm_ffn_w, w_router, b_router, w_gate_up, b_gate_up,
           w_down, b_down, norm_final_w):
    batch, seq, d = x_prompt.shape
    dec, n_new, _ = x_sample.shape
    depth, n_pool, page, n_maps, d_qk = cache_k.shape
    n_heads, d_v = cache_v.shape[3:]
    aw = n_heads * d_v
    ch = d - aw
    n_pages = page_table.shape[1]
    past = n_pages * page
    rope_dim = d_qk // 4
    half = rope_dim // 2
    scale = d_qk ** -0.5
    n_exp = w_router.shape[2]
    n_p, n_s = batch * seq, dec * n_new

    h_p = x_prompt.reshape(n_p, d)
    h_s = x_sample.reshape(n_s, d)
    tab_p = _rope_tables(jnp.arange(seq, dtype=I32), d_qk, rope_dim)
    tab_s = _rope_tables(past + jnp.arange(n_s, dtype=I32) % n_new, d_qk, rope_dim)
    outs = {k: [] for k in ("kp", "vp", "cp", "ks", "vs", "cs")}

    for l in range(depth):
        lam_init = 0.8 - 0.6 * math.exp(-0.3 * l)
        lams = tuple(v[l].reshape(1, d_qk) for v in (lambda_q1, lambda_k1, lambda_q2, lambda_k2))
        sw = subln_w[l].reshape(1, d_v)
        w_in_bf = w_in[l].astype(BF16)
        w_o_bf = w_o[l].astype(BF16)
        nmw = norm_mix_w[l].reshape(1, d)
        dw_b = conv_dw_b[l].reshape(1, ch)
        ln_w = conv_norm_w[l].reshape(1, ch)
        ln_b = conv_norm_b[l].reshape(1, ch)

        tt_p = _tile(seq, 512)
        q_p, k_p, v_p, u_p = _proj(h_p, nmw, w_in_bf, tab_p, scale=scale, half=half, tt=tt_p)
        tt_s = _tile(n_s, 512)
        q_s, k_s, v_s, u_s = _proj(h_s, nmw, w_in_bf, tab_s, scale=scale, half=half, tt=tt_s)

        o_p = _attn_prompt(q_p, k_p, v_p, lams, sw, batch=batch, seq=seq, n_heads=n_heads,
                           d_qk=d_qk, lam_init=lam_init, tq=_tile(seq, 512))
        q_rows = q_s.reshape(dec, n_new, n_maps, d_qk).transpose(0, 2, 1, 3)
        q_rows = jnp.pad(q_rows, ((0, 0), (0, 0), (0, QR - n_new), (0, 0))).reshape(dec, n_maps * QR, d_qk)
        ck = cache_k.transpose(0, 1, 3, 4, 2).reshape(depth * n_pool, n_maps * d_qk, page)
        cv = cache_v.reshape(depth * n_pool, page * n_heads, d_v)
        o_s = _attn_sample(page_table + l * n_pool, q_rows, k_s.reshape(dec, n_new * n_maps, d_qk),
                           v_s.reshape(dec, n_new * n_heads, d_v), ck, cv, lams, sw, n_new=n_new,
                           n_maps=n_maps, n_heads=n_heads, qr=QR, lam_init=lam_init,
                           n_chunk=_tile(n_pages, 16))
        o_s = o_s.reshape(n_s, aw).astype(BF16)

        c_p = _conv_prompt(u_p, conv_dw_w[l], dw_b, ln_w, ln_b, batch=batch, seq=seq,
                           tt=_tile(seq, 256))
        u_s3 = u_s.reshape(dec, n_new, ch)
        u_pad_s = jnp.concatenate([state_conv[l], u_s3], axis=1)
        c_s = _conv_sample(u_pad_s.transpose(1, 0, 2), conv_dw_w[l], dw_b, ln_w, ln_b,
                           n_new=n_new, tb=_tile(dec, 16))
        c_s = c_s.transpose(1, 0, 2).reshape(n_s, ch)

        nfw = norm_ffn_w[l].reshape(1, d)
        br = b_router[l].reshape(1, n_exp)
        assert n_p % tt_s == 0
        x1_p, xf_all, idx_p, gate_p = _wo_router(o_p, c_p, h_p, w_o_bf, nfw, w_router[l], br, tt=tt_p,
                                                 xf_rows=n_p + n_s)
        x1_s, xf_all, idx_s, gate_s = _wo_router(o_s, c_s, h_s, w_o_bf, nfw, w_router[l], br, tt=tt_s,
                                                 xf_rows=n_p + n_s, xf_row0=n_p, xf_buf=xf_all)

        idx_all = jnp.concatenate([idx_p[:, :TOP_K], idx_s[:, :TOP_K]], axis=0)
        dest, tile_exp, tile_nsub, tile_rows, n_used, buf_tok = _route_plan(
            idx_all, n_exp, MOE_TILE, MOE_SUBTILE)
        xs = _gather_rows(xf_all, buf_tok, tile_nsub, tile_rows, n_used, tm=MOE_TILE, ts=MOE_SUBTILE)
        yb = _moe_experts(xs, tile_exp, tile_nsub, n_used, w_gate_up[l], b_gate_up[l], w_down[l],
                          b_down[l], tm=MOE_TILE, ts=MOE_SUBTILE, tf=_tile(w_down.shape[2], 1024),
                          tn=_tile(d, 1024))

        last = l == depth - 1
        nw_out = norm_final_w.reshape(1, d) if last else jnp.ones((1, d), F32)
        assert last, "the combine kernel fuses the final norm; deeper stacks need an un-normed variant"
        h_p = _combine(dest[:n_p * TOP_K], x1_p, gate_p, nw_out, yb, tt=_tile(n_p, 512))
        h_s = _combine(dest[n_p * TOP_K:], x1_s, gate_s, nw_out, yb, tt=_tile(n_s, 512))

        outs["kp"].append(k_p.reshape(batch, seq, n_maps, d_qk))
        outs["vp"].append(v_p.reshape(batch, seq, n_heads, d_v))
        outs["cp"].append(u_p.reshape(batch, seq, ch)[:, seq - (conv_dw_w.shape[1] - 1):])
        outs["ks"].append(k_s.reshape(dec, n_new, n_maps, d_qk))
        outs["vs"].append(v_s.reshape(dec, n_new, n_heads, d_v))
        outs["cs"].append(u_pad_s[:, n_new:])

    return (h_p.reshape(batch, seq, d), h_s.reshape(dec, n_new, d),
            jnp.stack(outs["kp"]), jnp.stack(outs["vp"]), jnp.stack(outs["cp"]),
            jnp.stack(outs["ks"]), jnp.stack(outs["vs"]), jnp.stack(outs["cs"]))
```

```python
import functools
import math

import jax
import jax.numpy as jnp
from jax import lax
from jax.experimental import pallas as pl
from jax.experimental.pallas import tpu as pltpu

F32 = jnp.float32
BF16 = jnp.bfloat16
I32 = jnp.int32

EPS = 1e-5
ROPE_THETA = 500000.0
TOP_K = 4
SWIGLU_LIMIT = 7.0
SWIGLU_ALPHA = 1.702
LANES = 128
VMEM_LIMIT = 56 * 1024 * 1024
QR = 8
MOE_TILE = 768
MOE_SUBTILE = 256
GATHER_UNROLL = 8


def _params(sem, vmem=VMEM_LIMIT):
    return pltpu.CompilerParams(dimension_semantics=sem, vmem_limit_bytes=vmem)


def _nt_dot(a, b):
    return lax.dot_general(a, b, (((1,), (1,)), ((), ())), preferred_element_type=F32)


def _rms(x, w):
    ms = jnp.mean(x * x, axis=-1, keepdims=True)
    return x * lax.rsqrt(ms + EPS) * w


def _lambda(lq1, lk1, lq2, lk2, lam_init):
    a = jnp.sum(lq1 * lk1, axis=-1, keepdims=True)
    b = jnp.sum(lq2 * lk2, axis=-1, keepdims=True)
    return jnp.exp(a) - jnp.exp(b) + lam_init


def _proj_body(x_ref, nw_ref, wa_ref, wb_ref, cs_ref, s1_ref, s2_ref,
               q_ref, k_ref, v_ref, u_ref, xn_ref, *, scale, half, k_transposed):
    j = pl.program_id(1)

    @pl.when(j == 0)
    def _():
        xn_ref[...] = _rms(x_ref[...], nw_ref[...]).astype(BF16)

    def rope_store(dst_ref, p, mul, transposed=False):
        cs, s1, s2 = cs_ref[...], s1_ref[...], s2_ref[...]
        for c in range(p.shape[1] // LANES):
            seg = p[:, c * LANES:(c + 1) * LANES]
            r = (seg * cs + pltpu.roll(seg, LANES - half, 1) * s1
                 + pltpu.roll(seg, half, 1) * s2)
            if mul is not None:
                r = r * mul
            if transposed:
                dst_ref[0, c * LANES:(c + 1) * LANES, :] = r.T.astype(dst_ref.dtype)
            else:
                dst_ref[:, c * LANES:(c + 1) * LANES] = r.astype(dst_ref.dtype)

    @pl.when(j == 0)
    def _():
        p = jnp.dot(xn_ref[...], wa_ref[...], preferred_element_type=F32)
        rope_store(q_ref, p, scale)

    @pl.when(j == 1)
    def _():
        p = jnp.dot(xn_ref[...], wa_ref[...], preferred_element_type=F32)
        rope_store(k_ref, p, None, transposed=k_transposed)

    @pl.when(j == 2)
    def _():
        v_ref[...] = jnp.dot(xn_ref[...], wa_ref[...], preferred_element_type=F32)

    @pl.when(j == 3)
    def _():
        xn = xn_ref[...]
        val = jnp.dot(xn, wa_ref[...], preferred_element_type=F32)
        gate = jnp.dot(xn, wb_ref[...], preferred_element_type=F32)
        u_ref[...] = val * jax.nn.sigmoid(gate)


def _proj(x2d, norm_w, w_in_bf, tables, *, scale, half, tt, seq_t=None):
    n, d = x2d.shape
    w = w_in_bf.shape[1] // 5
    if seq_t is None:
        k_shape, k_spec = (n, w), pl.BlockSpec((tt, w), lambda i, j: (i, 0))
    else:
        nt = seq_t // tt
        k_shape, k_spec = (n // seq_t, w, seq_t), pl.BlockSpec((1, w, tt), lambda i, j: (i // nt, 0, i % nt))
    cs, s1, s2 = tables
    n_pos_blocks = cs.shape[0] // tt
    tab_spec = pl.BlockSpec((tt, LANES), lambda i, j: (i % n_pos_blocks, 0))
    out_spec = pl.BlockSpec((tt, w), lambda i, j: (i, 0))
    return pl.pallas_call(
        functools.partial(_proj_body, scale=scale, half=half, k_transposed=seq_t is not None),
        grid=(n // tt, 4),
        in_specs=[
            pl.BlockSpec((tt, d), lambda i, j: (i, 0)),
            pl.BlockSpec((1, d), lambda i, j: (0, 0)),
            pl.BlockSpec((d, w), lambda i, j: (0, j)),
            pl.BlockSpec((d, w), lambda i, j: (0, 4)),
            tab_spec, tab_spec, tab_spec,
        ],
        out_specs=[out_spec, k_spec, out_spec, out_spec],
        out_shape=[
            jax.ShapeDtypeStruct((n, w), BF16),
            jax.ShapeDtypeStruct(k_shape, F32),
            jax.ShapeDtypeStruct((n, w), F32),
            jax.ShapeDtypeStruct((n, w), F32),
        ],
        scratch_shapes=[pltpu.VMEM((tt, d), BF16)],
        compiler_params=_params(("arbitrary", "arbitrary")),
        name="proj",
    )(x2d, norm_w, w_in_bf, w_in_bf, cs, s1, s2)


def _rope_tables(pos, d_qk, rope_dim):
    half = rope_dim // 2
    inv = ROPE_THETA ** (-jnp.arange(0, rope_dim, 2, dtype=F32) / rope_dim)
    ang = pos.astype(F32)[:, None] * inv[None, :]
    cos, sin = jnp.cos(ang), jnp.sin(ang)
    p = pos.shape[0]
    zh = jnp.zeros((p, half), F32)
    zr = jnp.zeros((p, d_qk - rope_dim), F32)
    cs = jnp.concatenate([cos, cos, jnp.ones((p, d_qk - rope_dim), F32)], axis=-1)
    s1 = jnp.concatenate([-sin, zh, zr], axis=-1)
    s2 = jnp.concatenate([zh, sin, zr], axis=-1)
    rep = LANES // d_qk
    return tuple(jnp.tile(t, (1, rep)) for t in (cs, s1, s2))


def _attn_prompt_body(q_ref, k_ref, v_ref, lq1_ref, lk1_ref, lq2_ref, lk2_ref, sw_ref,
                      o_ref, kb_ref, vb_ref, m_ref, l_ref, acc_ref, *, tq, tk, d_qk, lam_init):
    qi = pl.program_id(2)

    @pl.when(qi == 0)
    def _():
        kb_ref[...] = k_ref[0].astype(BF16)
        vb_ref[...] = v_ref[...].astype(BF16)

    q = q_ref[...]
    lane = lax.broadcasted_iota(I32, q.shape, 1)
    zero = jnp.zeros_like(q)
    qm = (jnp.where(lane < d_qk, q, zero), jnp.where(lane >= d_qk, q, zero))
    m_ref[...] = jnp.full(m_ref.shape, -jnp.inf, F32)
    l_ref[...] = jnp.zeros(l_ref.shape, F32)
    acc_ref[...] = jnp.zeros(acc_ref.shape, F32)

    def step(c, masked):
        k0 = pl.multiple_of(c * tk, tk)
        kc_t = kb_ref[:, pl.ds(k0, tk)]
        vc = vb_ref[pl.ds(k0, tk), :]
        for mi in range(2):
            s = jnp.dot(qm[mi], kc_t, preferred_element_type=F32)
            if masked:
                row = lax.broadcasted_iota(I32, (tq, tk), 0)
                col = lax.broadcasted_iota(I32, (tq, tk), 1)
                s = jnp.where(col + c * tk <= row + qi * tq, s, -jnp.inf)
            m_p = m_ref[mi]
            m_n = jnp.maximum(m_p, jnp.max(s, axis=1, keepdims=True))
            p = jnp.exp(s - jnp.concatenate([m_n] * (tk // LANES), axis=1))
            al = jnp.exp(m_p - m_n)
            l_ref[mi] = al * l_ref[mi] + jnp.sum(p, axis=1, keepdims=True)
            acc_ref[mi] = al * acc_ref[mi] + jnp.dot(p.astype(BF16), vc, preferred_element_type=F32)
            m_ref[mi] = m_n

    def full_step(c, carry):
        step(c, False)
        return carry

    n_full = qi * (tq // tk)
    lax.fori_loop(0, n_full, full_step, 0)
    for dgl in range(tq // tk):
        step(n_full + dgl, True)

    lam = _lambda(lq1_ref[...], lk1_ref[...], lq2_ref[...], lk2_ref[...], lam_init)
    o = acc_ref[0] / l_ref[0] - lam * (acc_ref[1] / l_ref[1])
    o_ref[...] = (_rms(o, sw_ref[...]) * (1.0 - lam_init)).astype(o_ref.dtype)


def _attn_prompt(q, k, v, lams, subln_w, *, batch, seq, n_heads, d_qk, lam_init, tq):
    n, w = q.shape
    d_v = w // n_heads
    nq = seq // tq
    vec = pl.BlockSpec((1, d_qk), lambda b, h, i: (0, 0))
    return pl.pallas_call(
        functools.partial(_attn_prompt_body, tq=tq, tk=tq, d_qk=d_qk, lam_init=lam_init),
        grid=(batch, n_heads, nq),
        in_specs=[
            pl.BlockSpec((tq, d_v), lambda b, h, i: (b * nq + i, h)),
            pl.BlockSpec((1, d_v, seq), lambda b, h, i: (b, h, 0)),
            pl.BlockSpec((seq, d_v), lambda b, h, i: (b, h)),
            vec, vec, vec, vec,
            pl.BlockSpec((1, d_v), lambda b, h, i: (0, 0)),
        ],
        out_specs=pl.BlockSpec((tq, d_v), lambda b, h, i: (b * nq + i, h)),
        out_shape=jax.ShapeDtypeStruct((n, w), BF16),
        scratch_shapes=[pltpu.VMEM((d_v, seq), BF16), pltpu.VMEM((seq, d_v), BF16),
                        pltpu.VMEM((2, tq, LANES), F32), pltpu.VMEM((2, tq, LANES), F32),
                        pltpu.VMEM((2, tq, d_v), F32)],
        compiler_params=_params(("arbitrary", "arbitrary", "arbitrary")),
        name="attn_prompt",
    )(q, k, v, *lams, subln_w)


def _attn_sample_body(pt_ref, q_ref, kn_ref, vn_ref, lq1_ref, lk1_ref, lq2_ref, lk2_ref, sw_ref,
                      *rest, n_chunk, n_new, n_maps, n_heads, qr, lam_init):
    del pt_ref
    k_refs = rest[:n_chunk]
    v_refs = rest[n_chunk:2 * n_chunk]
    o_ref = rest[2 * n_chunk]
    m_ref, l_ref, acc_ref, qbd_ref = rest[2 * n_chunk + 1:]
    c = pl.program_id(1)
    page = k_refs[0].shape[2]
    d_qk = k_refs[0].shape[1] // n_maps
    hr = 2 * qr

    @pl.when(c == 0)
    def _():
        m_ref[...] = jnp.full(m_ref.shape, -jnp.inf, F32)
        l_ref[...] = jnp.zeros(l_ref.shape, F32)
        acc_ref[...] = jnp.zeros(acc_ref.shape, F32)

    q_all = q_ref[0]

    @pl.when(c == 0)
    def _():
        q_rep = jnp.concatenate([q_all] * n_maps, axis=1)
        row_map = lax.broadcasted_iota(I32, q_rep.shape, 0) // qr
        col_map = lax.broadcasted_iota(I32, q_rep.shape, 1) // d_qk
        qbd_ref[...] = jnp.where(row_map == col_map, q_rep, jnp.zeros_like(q_rep))

    m_p, l_p, acc = m_ref[...], l_ref[...], acc_ref[...]
    k_cat = jnp.concatenate([k_refs[i][0].astype(BF16) for i in range(n_chunk)], axis=1)
    s = jnp.dot(qbd_ref[...], k_cat, preferred_element_type=F32)
    m_n = jnp.maximum(m_p, jnp.max(s, axis=1, keepdims=True))
    p = jnp.exp(s - m_n)
    al = jnp.exp(m_p - m_n)
    l_p = al * l_p + jnp.sum(p, axis=1, keepdims=True)
    pb = p.astype(BF16)
    pv = []
    for h in range(n_heads):
        vh = jnp.concatenate(
            [v_refs[i][0, pl.ds(h, page, stride=n_heads), :].astype(BF16) for i in range(n_chunk)],
            axis=0)
        pv.append(jnp.dot(pb[h * hr:(h + 1) * hr], vh, preferred_element_type=F32))
    acc = al * acc + jnp.concatenate(pv, axis=0)
    m_p = m_n
    m_ref[...] = m_p
    l_ref[...] = l_p
    acc_ref[...] = acc

    @pl.when(c == pl.num_programs(1) - 1)
    def _():
        qf = q_all.astype(F32)
        kn, vn = kn_ref[0], vn_ref[0]
        q_tok = lax.broadcasted_iota(I32, (qf.shape[0], 1), 0) % qr
        s_new = []
        for j in range(n_new):
            parts = []
            for m in range(n_maps):
                krow = kn[j * n_maps + m:j * n_maps + m + 1, :]
                parts.append(jnp.sum(qf[m * qr:(m + 1) * qr] * krow, axis=1, keepdims=True))
            s_new.append(jnp.where(q_tok >= j, jnp.concatenate(parts, axis=0), -jnp.inf))
        m_n = m_p
        for sj in s_new:
            m_n = jnp.maximum(m_n, sj)
        al = jnp.exp(m_p - m_n)
        l_n = al * l_p
        a_n = al * acc
        for j in range(n_new):
            pj = jnp.exp(s_new[j] - m_n)
            l_n = l_n + pj
            a_n = a_n + jnp.concatenate(
                [pj[h * hr:(h + 1) * hr] * vn[j * n_heads + h:j * n_heads + h + 1, :]
                 for h in range(n_heads)], axis=0)
        a_n = a_n / l_n
        lam = _lambda(lq1_ref[...], lk1_ref[...], lq2_ref[...], lk2_ref[...], lam_init)
        d_v = a_n.shape[1]
        for h in range(n_heads):
            o = a_n[h * hr:h * hr + n_new] - lam * a_n[h * hr + qr:h * hr + qr + n_new]
            o_ref[0, :, h * d_v:(h + 1) * d_v] = (
                _rms(o, sw_ref[...]) * (1.0 - lam_init)).astype(o_ref.dtype)


def _attn_sample(pages, q_rows, k_new, v_new, ck, cv, lams, subln_w, *, n_new, n_maps, n_heads, qr,
                 lam_init, n_chunk):
    dec, n_pages = pages.shape
    d_qk = ck.shape[1] // n_maps
    d_v = cv.shape[2]
    rows = n_maps * qr
    pt_flat = pages.reshape(-1)

    def page_map(i, b, c, pt):
        return (pt[b * n_pages + c * n_chunk + i], 0, 0)

    vec = pl.BlockSpec((1, d_qk), lambda b, c, pt: (0, 0))
    k_specs = [pl.BlockSpec((1,) + ck.shape[1:], functools.partial(page_map, i)) for i in range(n_chunk)]
    v_specs = [pl.BlockSpec((1,) + cv.shape[1:], functools.partial(page_map, i)) for i in range(n_chunk)]
    grid_spec = pltpu.PrefetchScalarGridSpec(
        num_scalar_prefetch=1,
        grid=(dec, n_pages // n_chunk),
        in_specs=[
            pl.BlockSpec((1, rows, d_qk), lambda b, c, pt: (b, 0, 0)),
            pl.BlockSpec((1,) + k_new.shape[1:], lambda b, c, pt: (b, 0, 0)),
            pl.BlockSpec((1,) + v_new.shape[1:], lambda b, c, pt: (b, 0, 0)),
            vec, vec, vec, vec,
            pl.BlockSpec((1, d_v), lambda b, c, pt: (0, 0)),
        ] + k_specs + v_specs,
        out_specs=pl.BlockSpec((1, n_new, n_heads * d_v), lambda b, c, pt: (b, 0, 0)),
        scratch_shapes=[pltpu.VMEM((rows, 1), F32), pltpu.VMEM((rows, 1), F32),
                        pltpu.VMEM((rows, d_v), F32), pltpu.VMEM((rows, n_maps * d_qk), BF16)],
    )
    return pl.pallas_call(
        functools.partial(_attn_sample_body, n_chunk=n_chunk, n_new=n_new, n_maps=n_maps,
                          n_heads=n_heads, qr=qr, lam_init=lam_init),
        grid_spec=grid_spec,
        out_shape=jax.ShapeDtypeStruct((dec, n_new, n_heads * d_v), F32),
        compiler_params=_params(("arbitrary", "arbitrary")),
        name="attn_sample",
    )(pt_flat, q_rows, k_new, v_new, *lams, subln_w, *([ck] * n_chunk), *([cv] * n_chunk))


def _ln_swish(y, lw, lb):
    mu = jnp.mean(y, axis=-1, keepdims=True)
    yc = y - mu
    var = jnp.mean(yc * yc, axis=-1, keepdims=True)
    z = yc * lax.rsqrt(var + EPS) * lw + lb
    return z * jax.nn.sigmoid(z)


def _conv_prompt_body(u_ref, up_ref, dw_ref, db_ref, lw_ref, lb_ref, c_ref, buf_ref, y_ref,
                      *, tt, width, halo, rb):
    t = pl.program_id(1)
    buf_ref[0:halo, :] = jnp.where(t == 0, 0.0, up_ref[...])
    buf_ref[halo:halo + tt, :] = u_ref[...]
    off = halo - (width - 1)

    def col_body(cc, carry):
        c0 = pl.multiple_of(cc * LANES, LANES)
        wts = dw_ref[:, pl.ds(c0, LANES)]
        bias = db_ref[:, pl.ds(c0, LANES)]
        for r in range(tt // rb):
            acc = jnp.zeros((rb, LANES), F32)
            for w in range(width):
                lo = r * rb + off + w
                acc = acc + buf_ref[lo:lo + rb, pl.ds(c0, LANES)] * wts[w:w + 1, :]
            y_ref[r * rb:(r + 1) * rb, pl.ds(c0, LANES)] = acc + bias
        return carry

    lax.fori_loop(0, u_ref.shape[1] // LANES, col_body, 0)
    c_ref[...] = _ln_swish(y_ref[...], lw_ref[...], lb_ref[...]).astype(c_ref.dtype)


def _conv_prompt(u, dw_w, dw_b, ln_w, ln_b, *, batch, seq, tt):
    n, ch = u.shape
    width = dw_w.shape[0]
    halo = 32
    nt = seq // tt
    per = tt // halo
    row = pl.BlockSpec((1, ch), lambda b, t: (0, 0))
    return pl.pallas_call(
        functools.partial(_conv_prompt_body, tt=tt, width=width, halo=halo, rb=64),
        grid=(batch, nt),
        in_specs=[
            pl.BlockSpec((tt, ch), lambda b, t: (b * nt + t, 0)),
            pl.BlockSpec((halo, ch), lambda b, t: (jnp.maximum((b * nt + t) * per - 1, 0), 0)),
            pl.BlockSpec((width, ch), lambda b, t: (0, 0)),
            row, row, row,
        ],
        out_specs=pl.BlockSpec((tt, ch), lambda b, t: (b * nt + t, 0)),
        out_shape=jax.ShapeDtypeStruct((n, ch), BF16),
        scratch_shapes=[pltpu.VMEM((halo + tt, ch), F32), pltpu.VMEM((tt, ch), F32)],
        compiler_params=_params(("arbitrary", "arbitrary")),
        name="conv_prompt",
    )(u, u, dw_w, dw_b, ln_w, ln_b)


def _conv_sample_body(up_ref, dw_ref, db_ref, lw_ref, lb_ref, c_ref, y_ref, *, width, n_new):
    length, tb, ch = up_ref.shape

    def col_body(cc, carry):
        c0 = pl.multiple_of(cc * LANES, LANES)
        wts = dw_ref[:, pl.ds(c0, LANES)]
        bias = db_ref[:, pl.ds(c0, LANES)]
        accs = [jnp.zeros((tb, LANES), F32) for _ in range(n_new)]
        for j in range(length):
            x = up_ref[j, :, pl.ds(c0, LANES)]
            for t in range(n_new):
                w = j - t
                if 0 <= w < width:
                    accs[t] = accs[t] + x * wts[w:w + 1, :]
        for t in range(n_new):
            y_ref[t, :, pl.ds(c0, LANES)] = accs[t] + bias
        return carry

    lax.fori_loop(0, ch // LANES, col_body, 0)
    c_ref[...] = _ln_swish(y_ref[...], lw_ref[...], lb_ref[...]).astype(c_ref.dtype)


def _conv_sample(u_pad_t, dw_w, dw_b, ln_w, ln_b, *, n_new, tb):
    length, dec, ch = u_pad_t.shape
    width = dw_w.shape[0]
    row = pl.BlockSpec((1, ch), lambda b: (0, 0))
    return pl.pallas_call(
        functools.partial(_conv_sample_body, width=width, n_new=n_new),
        grid=(dec // tb,),
        in_specs=[
            pl.BlockSpec((length, tb, ch), lambda b: (0, b, 0)),
            pl.BlockSpec((width, ch), lambda b: (0, 0)),
            row, row, row,
        ],
        out_specs=pl.BlockSpec((n_new, tb, ch), lambda b: (0, b, 0)),
        out_shape=jax.ShapeDtypeStruct((n_new, dec, ch), BF16),
        scratch_shapes=[pltpu.VMEM((n_new, tb, ch), F32)],
        compiler_params=_params(("arbitrary",)),
        name="conv_sample",
    )(u_pad_t, dw_w, dw_b, ln_w, ln_b)


def _wo_body(o_ref, c_ref, x_ref, wo_ref, nw_ref, wr_ref, br_ref, *rest, aw, top_k):
    x1_ref, xf_ref, idx_ref, gate_ref = rest[-4:]
    attn = (jnp.dot(o_ref[...], wo_ref[0:aw, :], preferred_element_type=F32)
            + jnp.dot(c_ref[...], wo_ref[aw:, :], preferred_element_type=F32))
    x1 = x_ref[...] + attn
    x1_ref[...] = x1
    xf = _rms(x1, nw_ref[...])
    xf_ref[...] = xf
    xh = xf.astype(BF16)
    xl = (xf - xh.astype(F32)).astype(BF16)
    wr = wr_ref[...]
    wh = wr.astype(BF16)
    wl = (wr - wh.astype(F32)).astype(BF16)
    logits = (jnp.dot(xh, wh, preferred_element_type=F32) + jnp.dot(xl, wh, preferred_element_type=F32)
              + jnp.dot(xh, wl, preferred_element_type=F32)) + br_ref[...]
    n_exp = logits.shape[1]
    lane_e = lax.broadcasted_iota(I32, logits.shape, 1)
    lane_o = lax.broadcasted_iota(I32, idx_ref.shape, 1)
    vals = logits
    tops, idxs = [], []
    for _ in range(top_k):
        mx = jnp.max(vals, axis=1, keepdims=True)
        am = jnp.min(jnp.where(vals == mx, lane_e, n_exp), axis=1, keepdims=True)
        tops.append(mx)
        idxs.append(am)
        vals = jnp.where(lane_e == am, -jnp.inf, vals)
    es = [jnp.exp(t - tops[0]) for t in tops]
    den = es[0]
    for e in es[1:]:
        den = den + e
    idx_out = jnp.zeros(idx_ref.shape, I32)
    gate_out = jnp.zeros(gate_ref.shape, F32)
    for k in range(top_k):
        idx_out = jnp.where(lane_o == k, idxs[k], idx_out)
        gate_out = jnp.where(lane_o == k, es[k] / den, gate_out)
    idx_ref[...] = idx_out
    gate_ref[...] = gate_out


def _wo_router(o, c, x2d, w_o_bf, norm_w, w_router, b_router, *, tt, xf_rows, xf_row0=0, xf_buf=None):
    n, d = x2d.shape
    aw = o.shape[1]
    n_exp = w_router.shape[1]
    blk0 = xf_row0 // tt
    row = pl.BlockSpec((tt, d), lambda i: (i, 0))
    half = pl.BlockSpec((tt, aw), lambda i: (i, 0))
    small = pl.BlockSpec((tt, LANES), lambda i: (i, 0))
    in_specs = [
        half, pl.BlockSpec((tt, d - aw), lambda i: (i, 0)), row,
        pl.BlockSpec((d, d), lambda i: (0, 0)),
        pl.BlockSpec((1, d), lambda i: (0, 0)),
        pl.BlockSpec((d, n_exp), lambda i: (0, 0)),
        pl.BlockSpec((1, n_exp), lambda i: (0, 0)),
    ]
    args = [o, c, x2d, w_o_bf, norm_w, w_router, b_router]
    aliases = {}
    if xf_buf is not None:
        in_specs.append(pl.BlockSpec(memory_space=pl.ANY))
        args.append(xf_buf)
        aliases = {len(args) - 1: 1}
    return pl.pallas_call(
        functools.partial(_wo_body, aw=aw, top_k=TOP_K),
        grid=(n // tt,),
        in_specs=in_specs,
        out_specs=[row, pl.BlockSpec((tt, d), lambda i: (blk0 + i, 0)), small, small],
        out_shape=[
            jax.ShapeDtypeStruct((n, d), F32),
            jax.ShapeDtypeStruct((xf_rows, d), F32),
            jax.ShapeDtypeStruct((n, LANES), I32),
            jax.ShapeDtypeStruct((n, LANES), F32),
        ],
        input_output_aliases=aliases,
        compiler_params=_params(("arbitrary",)),
        name="wo_router",
    )(*args)


def _route_plan(idx, n_exp, tm, ts):
    n_tok, top_k = idx.shape
    n_asg = n_tok * top_k
    nsub = tm // ts
    flat_e = idx.reshape(-1)
    onehot = (flat_e[:, None] == jnp.arange(n_exp, dtype=I32)[None, :]).astype(I32)
    csum = jnp.cumsum(onehot, axis=0)
    rank = jnp.sum((csum - onehot) * onehot, axis=1)
    counts = csum[-1]
    ntile = (counts + tm - 1) // tm
    first_rows = counts - jnp.maximum(ntile - 1, 0) * tm
    tile_end = jnp.cumsum(ntile)
    tile_beg = tile_end - ntile
    fr = first_rows[flat_e]
    dest = (tile_beg[flat_e] * tm + jnp.where(rank < fr, rank, rank - fr + tm)).astype(I32)

    n_tiles = -(-n_asg // tm) + n_exp
    t_ids = jnp.arange(n_tiles, dtype=I32)
    tile_exp = jnp.minimum(jnp.sum(t_ids[:, None] >= tile_end[None, :], axis=1), n_exp - 1).astype(I32)
    n_used = tile_end[-1]
    first_sub = (first_rows[tile_exp] + ts - 1) // ts
    tile_nsub = jnp.where(t_ids < n_used, jnp.where(t_ids == tile_beg[tile_exp], first_sub, nsub), 0)
    tile_nsub = tile_nsub.astype(I32)
    tile_rows = jnp.where(t_ids < n_used, jnp.where(t_ids == tile_beg[tile_exp], first_rows[tile_exp], tm), 0)

    flat_tok = jnp.arange(n_asg, dtype=I32) // top_k
    buf_tok = jnp.zeros((n_tiles * tm,), I32).at[dest].set(flat_tok, unique_indices=True)
    return dest, tile_exp, tile_nsub, tile_rows.astype(I32), n_used.astype(I32).reshape(1), buf_tok


def _gather_body(tok_ref, tns_ref, trows_ref, nused_ref, x_hbm, xs_ref, land_ref, sem, *, tm, ts):
    t = pl.program_id(0)
    n = nused_ref[0]

    def row_copy(tile, slot, r):
        tok = tok_ref[tile * tm + r]
        return pltpu.make_async_copy(x_hbm.at[pl.ds(tok, 1)], land_ref.at[slot, pl.ds(r, 1)],
                                     sem.at[slot])

    def groups(tile):
        return lax.shift_right_logical(trows_ref[tile] + (GATHER_UNROLL - 1), GATHER_UNROLL.bit_length() - 1)

    def start_all(tile, slot):
        def body(g, carry):
            for u in range(GATHER_UNROLL):
                row_copy(tile, slot, g * GATHER_UNROLL + u).start()
            return carry
        lax.fori_loop(0, groups(tile), body, 0)

    def wait_all(tile, slot):
        def body(g, carry):
            for u in range(GATHER_UNROLL):
                row_copy(tile, slot, g * GATHER_UNROLL + u).wait()
            return carry
        lax.fori_loop(0, groups(tile), body, 0)

    @pl.when(t == 0)
    def _():
        land_ref[...] = jnp.zeros(land_ref.shape, land_ref.dtype)
        start_all(0, 0)

    @pl.when(t + 1 < n)
    def _():
        start_all(t + 1, (t + 1) % 2)

    @pl.when(t < n)
    def _():
        slot = t % 2
        wait_all(t, slot)

        def convert(j, carry):
            r0 = pl.multiple_of(j * ts, ts)
            xs_ref[pl.ds(r0, ts), :] = land_ref[slot, pl.ds(r0, ts), :].astype(xs_ref.dtype)
            return carry

        lax.fori_loop(0, tns_ref[t], convert, 0)


def _gather_rows(xf, buf_tok, tile_nsub, tile_rows, n_used, *, tm, ts):
    d = xf.shape[1]
    n_rows = buf_tok.shape[0]
    grid_spec = pltpu.PrefetchScalarGridSpec(
        num_scalar_prefetch=4,
        grid=(n_rows // tm,),
        in_specs=[pl.BlockSpec(memory_space=pl.ANY)],
        out_specs=pl.BlockSpec((tm, d), lambda t, tok, tns, trw, nu: (jnp.minimum(t, nu[0] - 1), 0)),
        scratch_shapes=[pltpu.VMEM((2, tm, d), xf.dtype), pltpu.SemaphoreType.DMA((2,))],
    )
    return pl.pallas_call(
        functools.partial(_gather_body, tm=tm, ts=ts),
        grid_spec=grid_spec,
        out_shape=jax.ShapeDtypeStruct((n_rows, d), BF16),
        compiler_params=_params(("arbitrary",)),
        name="moe_gather",
    )(buf_tok, tile_nsub, tile_rows, n_used, xf)


def _expert_changed(texp_ref, t):
    return (t == 0) | (texp_ref[t] != texp_ref[jnp.maximum(t - 1, 0)])


def _moe_up_body(texp_ref, tns_ref, nused_ref, x_ref, wg_ref, wu_ref, bg_ref, bu_ref, a_ref, wbf_ref,
                 *, ts):
    t = pl.program_id(1)

    @pl.when(t < nused_ref[0])
    def _():
        @pl.when(_expert_changed(texp_ref, t))
        def _():
            wbf_ref[0] = wg_ref[0].astype(BF16)
            wbf_ref[1] = wu_ref[0].astype(BF16)

        def sub(j, carry):
            r0 = pl.multiple_of(j * ts, ts)
            x = x_ref[pl.ds(r0, ts), :]
            g = jnp.dot(x, wbf_ref[0], preferred_element_type=F32) + bg_ref[0]
            lin = jnp.dot(x, wbf_ref[1], preferred_element_type=F32) + bu_ref[0]
            g = jnp.minimum(g, SWIGLU_LIMIT)
            lin = jnp.clip(lin, -SWIGLU_LIMIT, SWIGLU_LIMIT)
            a = g * jax.nn.sigmoid(SWIGLU_ALPHA * g) * (lin + 1.0)
            a_ref[pl.ds(r0, ts), :] = a.astype(a_ref.dtype)
            return carry

        lax.fori_loop(0, tns_ref[t], sub, 0)


def _moe_down_body(texp_ref, tns_ref, nused_ref, a_ref, wd_ref, bd_ref, y_ref, wbf_ref, *, ts):
    t = pl.program_id(1)

    @pl.when(t < nused_ref[0])
    def _():
        @pl.when(_expert_changed(texp_ref, t))
        def _():
            wbf_ref[...] = wd_ref[0].astype(BF16)

        def sub(j, carry):
            r0 = pl.multiple_of(j * ts, ts)
            y_ref[pl.ds(r0, ts), :] = jnp.dot(a_ref[pl.ds(r0, ts), :], wbf_ref[...],
                                              preferred_element_type=F32) + bd_ref[0]
            return carry

        lax.fori_loop(0, tns_ref[t], sub, 0)


def _moe_experts(xs, tile_exp, tile_nsub, n_used, w_gate_up, b_gate_up, w_down, b_down, *, tm, ts, tf, tn):
    n_rows, d = xs.shape
    n_exp, _, two_ff = w_gate_up.shape
    d_ff = two_ff // 2
    n_f = d_ff // tf
    n_t = n_rows // tm
    bgu = b_gate_up.reshape(n_exp, 1, two_ff)
    bdn = b_down.reshape(n_exp, 1, d)

    def tile(t, nu):
        return jnp.minimum(t, nu[0] - 1)

    up_spec = pltpu.PrefetchScalarGridSpec(
        num_scalar_prefetch=3,
        grid=(n_f, n_t),
        in_specs=[
            pl.BlockSpec((tm, d), lambda f, t, te, ns, nu: (tile(t, nu), 0)),
            pl.BlockSpec((1, d, tf), lambda f, t, te, ns, nu: (te[tile(t, nu)], 0, f)),
            pl.BlockSpec((1, d, tf), lambda f, t, te, ns, nu: (te[tile(t, nu)], 0, n_f + f)),
            pl.BlockSpec((1, 1, tf), lambda f, t, te, ns, nu: (te[tile(t, nu)], 0, f)),
            pl.BlockSpec((1, 1, tf), lambda f, t, te, ns, nu: (te[tile(t, nu)], 0, n_f + f)),
        ],
        out_specs=pl.BlockSpec((tm, tf), lambda f, t, te, ns, nu: (tile(t, nu), f)),
        scratch_shapes=[pltpu.VMEM((2, d, tf), BF16)],
    )
    act = pl.pallas_call(
        functools.partial(_moe_up_body, ts=ts),
        grid_spec=up_spec,
        out_shape=jax.ShapeDtypeStruct((n_rows, d_ff), BF16),
        compiler_params=_params(("arbitrary", "arbitrary")),
        name="moe_up",
    )(tile_exp, tile_nsub, n_used, xs, w_gate_up, w_gate_up, bgu, bgu)

    n_n = d // tn
    down_spec = pltpu.PrefetchScalarGridSpec(
        num_scalar_prefetch=3,
        grid=(n_n, n_t),
        in_specs=[
            pl.BlockSpec((tm, d_ff), lambda n, t, te, ns, nu: (tile(t, nu), 0)),
            pl.BlockSpec((1, d_ff, tn), lambda n, t, te, ns, nu: (te[tile(t, nu)], 0, n)),
            pl.BlockSpec((1, 1, tn), lambda n, t, te, ns, nu: (te[tile(t, nu)], 0, n)),
        ],
        out_specs=pl.BlockSpec((tm, tn), lambda n, t, te, ns, nu: (tile(t, nu), n)),
        scratch_shapes=[pltpu.VMEM((d_ff, tn), BF16)],
    )
    return pl.pallas_call(
        functools.partial(_moe_down_body, ts=ts),
        grid_spec=down_spec,
        out_shape=jax.ShapeDtypeStruct((n_rows, d), F32),
        compiler_params=_params(("arbitrary", "arbitrary")),
        name="moe_down",
    )(tile_exp, tile_nsub, n_used, act, w_down, bdn)


def _combine_body(dest_ref, x1_ref, gate_ref, nw_ref, yb_hbm, out_ref, rows_ref, sem, *, tt, top_k):
    i = pl.program_id(0)

    def row_copy(r, k):
        d = dest_ref[(i * tt + r) * top_k + k]
        return pltpu.make_async_copy(yb_hbm.at[pl.ds(d, 1)], rows_ref.at[k, pl.ds(r, 1)], sem)

    rows_per_iter = GATHER_UNROLL // top_k

    def issue(g, carry):
        for u in range(rows_per_iter):
            for k in range(top_k):
                row_copy(g * rows_per_iter + u, k).start()
        return carry

    def drain(g, carry):
        for u in range(rows_per_iter):
            for k in range(top_k):
                row_copy(g * rows_per_iter + u, k).wait()
        return carry

    lax.fori_loop(0, tt // rows_per_iter, issue, 0)
    lax.fori_loop(0, tt // rows_per_iter, drain, 0)
    gate = gate_ref[...]
    moe = rows_ref[0] * gate[:, 0:1]
    for k in range(1, top_k):
        moe = moe + rows_ref[k] * gate[:, k:k + 1]
    out_ref[...] = _rms(x1_ref[...] + moe, nw_ref[...])


def _combine(dest, x1, gate, norm_w, yb, *, tt):
    n, d = x1.shape
    grid_spec = pltpu.PrefetchScalarGridSpec(
        num_scalar_prefetch=1,
        grid=(n // tt,),
        in_specs=[
            pl.BlockSpec((tt, d), lambda i, ds: (i, 0)),
            pl.BlockSpec((tt, LANES), lambda i, ds: (i, 0)),
            pl.BlockSpec((1, d), lambda i, ds: (0, 0)),
            pl.BlockSpec(memory_space=pl.ANY),
        ],
        out_specs=pl.BlockSpec((tt, d), lambda i, ds: (i, 0)),
        scratch_shapes=[pltpu.VMEM((TOP_K, tt, d), F32), pltpu.SemaphoreType.DMA],
    )
    return pl.pallas_call(
        functools.partial(_combine_body, tt=tt, top_k=TOP_K),
        grid_spec=grid_spec,
        out_shape=jax.ShapeDtypeStruct((n, d), F32),
        compiler_params=_params(("arbitrary",)),
        name="moe_combine",
    )(dest, x1, gate, norm_w, yb)


def _tile(n, pref):
    t = min(n, pref)
    while n % t:
        t //= 2
    return t


def kernel(x_prompt, x_sample, cache_k, cache_v, state_conv, page_table, norm_mix_w, w_in,
           lambda_q1, lambda_k1, lambda_q2, lambda_k2, subln_w, conv_dw_w, conv_dw_b,
           conv_norm_w, conv_norm_b, w_o, norm_ffn_w, w_router, b_router, w_gate_up, b_gate_up,
           w_down, b_down, norm_final_w):
    batch, seq, d = x_prompt.shape
    dec, n_new, _ = x_sample.shape
    depth, n_pool, page, n_maps, d_qk = cache_k.shape
    n_heads, d_v = cache_v.shape[3:]
    aw = n_heads * d_v
    ch = d - aw
    n_pages = page_table.shape[1]
    past = n_pages * page
    rope_dim = d_qk // 4
    half = rope_dim // 2
    scale = d_qk ** -0.5
    n_exp = w_router.shape[2]
    n_p, n_s = batch * seq, dec * n_new

    h_p = x_prompt.reshape(n_p, d)
    h_s = x_sample.reshape(n_s, d)
    tab_p = _rope_tables(jnp.arange(seq, dtype=I32), d_qk, rope_dim)
    tab_s = _rope_tables(past + jnp.arange(n_s, dtype=I32) % n_new, d_qk, rope_dim)
    outs = {k: [] for k in ("kp", "vp", "cp", "ks", "vs", "cs")}

    for l in range(depth):
        lam_init = 0.8 - 0.6 * math.exp(-0.3 * l)
        lams = tuple(v[l].reshape(1, d_qk) for v in (lambda_q1, lambda_k1, lambda_q2, lambda_k2))
        sw = subln_w[l].reshape(1, d_v)
        w_in_bf = w_in[l].astype(BF16)
        w_o_bf = w_o[l].astype(BF16)
        nmw = norm_mix_w[l].reshape(1, d)
        dw_b = conv_dw_b[l].reshape(1, ch)
        ln_w = conv_norm_w[l].reshape(1, ch)
        ln_b = conv_norm_b[l].reshape(1, ch)

        tt_p = _tile(seq, 512)
        q_p, k_p, v_p, u_p = _proj(h_p, nmw, w_in_bf, tab_p, scale=scale, half=half, tt=tt_p, seq_t=seq)
        tt_s = _tile(n_s, 512)
        q_s, k_s, v_s, u_s = _proj(h_s, nmw, w_in_bf, tab_s, scale=scale, half=half, tt=tt_s)

        o_p = _attn_prompt(q_p, k_p, v_p, lams, sw, batch=batch, seq=seq, n_heads=n_heads,
                           d_qk=d_qk, lam_init=lam_init, tq=_tile(seq, 512))
        q_rows = q_s.reshape(dec, n_new, n_maps, d_qk).transpose(0, 2, 1, 3)
        q_rows = jnp.pad(q_rows, ((0, 0), (0, 0), (0, QR - n_new), (0, 0))).reshape(dec, n_maps * QR, d_qk)
        ck = cache_k.transpose(0, 1, 3, 4, 2).reshape(depth * n_pool, n_maps * d_qk, page)
        cv = cache_v.reshape(depth * n_pool, page * n_heads, d_v)
        o_s = _attn_sample(page_table + l * n_pool, q_rows, k_s.reshape(dec, n_new * n_maps, d_qk),
                           v_s.reshape(dec, n_new * n_heads, d_v), ck, cv, lams, sw, n_new=n_new,
                           n_maps=n_maps, n_heads=n_heads, qr=QR, lam_init=lam_init,
                           n_chunk=_tile(n_pages, 16))
        o_s = o_s.reshape(n_s, aw).astype(BF16)

        c_p = _conv_prompt(u_p, conv_dw_w[l], dw_b, ln_w, ln_b, batch=batch, seq=seq,
                           tt=_tile(seq, 256))
        u_s3 = u_s.reshape(dec, n_new, ch)
        u_pad_s = jnp.concatenate([state_conv[l], u_s3], axis=1)
        c_s = _conv_sample(u_pad_s.transpose(1, 0, 2), conv_dw_w[l], dw_b, ln_w, ln_b,
                           n_new=n_new, tb=_tile(dec, 16))
        c_s = c_s.transpose(1, 0, 2).reshape(n_s, ch)

        nfw = norm_ffn_w[l].reshape(1, d)
        br = b_router[l].reshape(1, n_exp)
        assert n_p % tt_s == 0
        x1_p, xf_all, idx_p, gate_p = _wo_router(o_p, c_p, h_p, w_o_bf, nfw, w_router[l], br, tt=tt_p,
                                                 xf_rows=n_p + n_s)
        x1_s, xf_all, idx_s, gate_s = _wo_router(o_s, c_s, h_s, w_o_bf, nfw, w_router[l], br, tt=tt_s,
                                                 xf_rows=n_p + n_s, xf_row0=n_p, xf_buf=xf_all)

        idx_all = jnp.concatenate([idx_p[:, :TOP_K], idx_s[:, :TOP_K]], axis=0)
        dest, tile_exp, tile_nsub, tile_rows, n_used, buf_tok = _route_plan(
            idx_all, n_exp, MOE_TILE, MOE_SUBTILE)
        xs = _gather_rows(xf_all, buf_tok, tile_nsub, tile_rows, n_used, tm=MOE_TILE, ts=MOE_SUBTILE)
        yb = _moe_experts(xs, tile_exp, tile_nsub, n_used, w_gate_up[l], b_gate_up[l], w_down[l],
                          b_down[l], tm=MOE_TILE, ts=MOE_SUBTILE, tf=_tile(w_down.shape[2], 1024),
                          tn=_tile(d, 1024))

        last = l == depth - 1
        nw_out = norm_final_w.reshape(1, d) if last else jnp.ones((1, d), F32)
        assert last, "the combine kernel fuses the final norm; deeper stacks need an un-normed variant"
        h_p = _combine(dest[:n_p * TOP_K], x1_p, gate_p, nw_out, yb, tt=_tile(n_p, 512))
        h_s = _combine(dest[n_p * TOP_K:], x1_s, gate_s, nw_out, yb, tt=_tile(n_s, 512))

        outs["kp"].append(k_p.reshape(batch, n_maps, d_qk, seq).transpose(0, 3, 1, 2))
        outs["vp"].append(v_p.reshape(batch, seq, n_heads, d_v))
        outs["cp"].append(u_p.reshape(batch, seq, ch)[:, seq - (conv_dw_w.shape[1] - 1):])
        outs["ks"].append(k_s.reshape(dec, n_new, n_maps, d_qk))
        outs["vs"].append(v_s.reshape(dec, n_new, n_heads, d_v))
        outs["cs"].append(u_pad_s[:, n_new:])

    return (h_p.reshape(batch, seq, d), h_s.reshape(dec, n_new, d),
            jnp.stack(outs["kp"]), jnp.stack(outs["vp"]), jnp.stack(outs["cp"]),
            jnp.stack(outs["ks"]), jnp.stack(outs["vs"]), jnp.stack(outs["cs"]))
```

```python
import functools
import math

import jax
import jax.numpy as jnp
from jax import lax
from jax.experimental import pallas as pl
from jax.experimental.pallas import tpu as pltpu

F32 = jnp.float32
BF16 = jnp.bfloat16
I32 = jnp.int32

EPS = 1e-5
ROPE_THETA = 500000.0
TOP_K = 4
SWIGLU_LIMIT = 7.0
SWIGLU_ALPHA = 1.702
LANES = 128
VMEM_LIMIT = 56 * 1024 * 1024
QR = 8
MOE_TILE = 768
MOE_SUBTILE = 256
GATHER_UNROLL = 8


def _params(sem, vmem=VMEM_LIMIT):
    return pltpu.CompilerParams(dimension_semantics=sem, vmem_limit_bytes=vmem)


def _nt_dot(a, b):
    return lax.dot_general(a, b, (((1,), (1,)), ((), ())), preferred_element_type=F32)


def _rms(x, w):
    ms = jnp.mean(x * x, axis=-1, keepdims=True)
    return x * lax.rsqrt(ms + EPS) * w


def _lambda(lq1, lk1, lq2, lk2, lam_init):
    a = jnp.sum(lq1 * lk1, axis=-1, keepdims=True)
    b = jnp.sum(lq2 * lk2, axis=-1, keepdims=True)
    return jnp.exp(a) - jnp.exp(b) + lam_init


def _proj_body(x_ref, nw_ref, wa_ref, wb_ref, cs_ref, s1_ref, s2_ref,
               q_ref, k_ref, v_ref, u_ref, xn_ref, *, scale, half, k_transposed):
    j = pl.program_id(1)

    @pl.when(j == 0)
    def _():
        xn_ref[...] = _rms(x_ref[...], nw_ref[...]).astype(BF16)

    def rope_store(dst_ref, p, mul, transposed=False):
        cs, s1, s2 = cs_ref[...], s1_ref[...], s2_ref[...]
        for c in range(p.shape[1] // LANES):
            seg = p[:, c * LANES:(c + 1) * LANES]
            r = (seg * cs + pltpu.roll(seg, LANES - half, 1) * s1
                 + pltpu.roll(seg, half, 1) * s2)
            if mul is not None:
                r = r * mul
            if transposed:
                dst_ref[0, c * LANES:(c + 1) * LANES, :] = r.T.astype(dst_ref.dtype)
            else:
                dst_ref[:, c * LANES:(c + 1) * LANES] = r.astype(dst_ref.dtype)

    @pl.when(j == 0)
    def _():
        p = jnp.dot(xn_ref[...], wa_ref[...], preferred_element_type=F32)
        rope_store(q_ref, p, scale)

    @pl.when(j == 1)
    def _():
        p = jnp.dot(xn_ref[...], wa_ref[...], preferred_element_type=F32)
        rope_store(k_ref, p, None, transposed=k_transposed)

    @pl.when(j == 2)
    def _():
        v_ref[...] = jnp.dot(xn_ref[...], wa_ref[...], preferred_element_type=F32)

    @pl.when(j == 3)
    def _():
        xn = xn_ref[...]
        val = jnp.dot(xn, wa_ref[...], preferred_element_type=F32)
        gate = jnp.dot(xn, wb_ref[...], preferred_element_type=F32)
        u_ref[...] = val * jax.nn.sigmoid(gate)


def _proj(x2d, norm_w, w_in_bf, tables, *, scale, half, tt, seq_t=None):
    n, d = x2d.shape
    w = w_in_bf.shape[1] // 5
    if seq_t is None:
        k_shape, k_spec = (n, w), pl.BlockSpec((tt, w), lambda i, j: (i, 0))
    else:
        nt = seq_t // tt
        k_shape, k_spec = (n // seq_t, w, seq_t), pl.BlockSpec((1, w, tt), lambda i, j: (i // nt, 0, i % nt))
    cs, s1, s2 = tables
    n_pos_blocks = cs.shape[0] // tt
    tab_spec = pl.BlockSpec((tt, LANES), lambda i, j: (i % n_pos_blocks, 0))
    out_spec = pl.BlockSpec((tt, w), lambda i, j: (i, 0))
    return pl.pallas_call(
        functools.partial(_proj_body, scale=scale, half=half, k_transposed=seq_t is not None),
        grid=(n // tt, 4),
        in_specs=[
            pl.BlockSpec((tt, d), lambda i, j: (i, 0)),
            pl.BlockSpec((1, d), lambda i, j: (0, 0)),
            pl.BlockSpec((d, w), lambda i, j: (0, j)),
            pl.BlockSpec((d, w), lambda i, j: (0, 4)),
            tab_spec, tab_spec, tab_spec,
        ],
        out_specs=[out_spec, k_spec, out_spec, out_spec],
        out_shape=[
            jax.ShapeDtypeStruct((n, w), BF16),
            jax.ShapeDtypeStruct(k_shape, F32),
            jax.ShapeDtypeStruct((n, w), F32),
            jax.ShapeDtypeStruct((n, w), F32),
        ],
        scratch_shapes=[pltpu.VMEM((tt, d), BF16)],
        compiler_params=_params(("arbitrary", "arbitrary")),
        name="proj",
    )(x2d, norm_w, w_in_bf, w_in_bf, cs, s1, s2)


def _rope_tables(pos, d_qk, rope_dim):
    half = rope_dim // 2
    inv = ROPE_THETA ** (-jnp.arange(0, rope_dim, 2, dtype=F32) / rope_dim)
    ang = pos.astype(F32)[:, None] * inv[None, :]
    cos, sin = jnp.cos(ang), jnp.sin(ang)
    p = pos.shape[0]
    zh = jnp.zeros((p, half), F32)
    zr = jnp.zeros((p, d_qk - rope_dim), F32)
    cs = jnp.concatenate([cos, cos, jnp.ones((p, d_qk - rope_dim), F32)], axis=-1)
    s1 = jnp.concatenate([-sin, zh, zr], axis=-1)
    s2 = jnp.concatenate([zh, sin, zr], axis=-1)
    rep = LANES // d_qk
    return tuple(jnp.tile(t, (1, rep)) for t in (cs, s1, s2))


def _attn_prompt_body(q_ref, k_ref, v_ref, lq1_ref, lk1_ref, lq2_ref, lk2_ref, sw_ref,
                      o_ref, kb_ref, vb_ref, m_ref, l_ref, acc_ref, *, tq, tk, d_qk, lam_init):
    qi = pl.program_id(2)

    @pl.when(qi == 0)
    def _():
        kb_ref[...] = k_ref[0].astype(BF16)
        vb_ref[...] = v_ref[...].astype(BF16)

    q = q_ref[...]
    lane = lax.broadcasted_iota(I32, q.shape, 1)
    zero = jnp.zeros_like(q)
    qm = (jnp.where(lane < d_qk, q, zero), jnp.where(lane >= d_qk, q, zero))
    m_ref[...] = jnp.full(m_ref.shape, -jnp.inf, F32)
    l_ref[...] = jnp.zeros(l_ref.shape, F32)
    acc_ref[...] = jnp.zeros(acc_ref.shape, F32)

    def step(c, masked):
        k0 = pl.multiple_of(c * tk, tk)
        kc_t = kb_ref[:, pl.ds(k0, tk)]
        vc = vb_ref[pl.ds(k0, tk), :]
        for mi in range(2):
            s = jnp.dot(qm[mi], kc_t, preferred_element_type=F32)
            if masked:
                row = lax.broadcasted_iota(I32, (tq, tk), 0)
                col = lax.broadcasted_iota(I32, (tq, tk), 1)
                s = jnp.where(col + c * tk <= row + qi * tq, s, -jnp.inf)
            m_p = m_ref[mi]
            m_n = jnp.maximum(m_p, jnp.max(s, axis=1, keepdims=True))
            p = jnp.exp(s - jnp.concatenate([m_n] * (tk // LANES), axis=1))
            al = jnp.exp(m_p - m_n)
            l_ref[mi] = al * l_ref[mi] + jnp.sum(p, axis=1, keepdims=True)
            acc_ref[mi] = al * acc_ref[mi] + jnp.dot(p.astype(BF16), vc, preferred_element_type=F32)
            m_ref[mi] = m_n

    def full_step(c, carry):
        step(c, False)
        return carry

    n_full = qi * (tq // tk)
    lax.fori_loop(0, n_full, full_step, 0)
    for dgl in range(tq // tk):
        step(n_full + dgl, True)

    lam = _lambda(lq1_ref[...], lk1_ref[...], lq2_ref[...], lk2_ref[...], lam_init)
    o = acc_ref[0] / l_ref[0] - lam * (acc_ref[1] / l_ref[1])
    o_ref[...] = (_rms(o, sw_ref[...]) * (1.0 - lam_init)).astype(o_ref.dtype)


def _attn_prompt(q, k, v, lams, subln_w, *, batch, seq, n_heads, d_qk, lam_init, tq):
    n, w = q.shape
    d_v = w // n_heads
    nq = seq // tq
    vec = pl.BlockSpec((1, d_qk), lambda b, h, i: (0, 0))
    return pl.pallas_call(
        functools.partial(_attn_prompt_body, tq=tq, tk=tq, d_qk=d_qk, lam_init=lam_init),
        grid=(batch, n_heads, nq),
        in_specs=[
            pl.BlockSpec((tq, d_v), lambda b, h, i: (b * nq + i, h)),
            pl.BlockSpec((1, d_v, seq), lambda b, h, i: (b, h, 0)),
            pl.BlockSpec((seq, d_v), lambda b, h, i: (b, h)),
            vec, vec, vec, vec,
            pl.BlockSpec((1, d_v), lambda b, h, i: (0, 0)),
        ],
        out_specs=pl.BlockSpec((tq, d_v), lambda b, h, i: (b * nq + i, h)),
        out_shape=jax.ShapeDtypeStruct((n, w), BF16),
        scratch_shapes=[pltpu.VMEM((d_v, seq), BF16), pltpu.VMEM((seq, d_v), BF16),
                        pltpu.VMEM((2, tq, LANES), F32), pltpu.VMEM((2, tq, LANES), F32),
                        pltpu.VMEM((2, tq, d_v), F32)],
        compiler_params=_params(("arbitrary", "arbitrary", "arbitrary")),
        name="attn_prompt",
    )(q, k, v, *lams, subln_w)


def _attn_sample_body(pt_ref, q_ref, kn_ref, vn_ref, lq1_ref, lk1_ref, lq2_ref, lk2_ref, sw_ref,
                      *rest, n_chunk, n_new, n_maps, n_heads, qr, lam_init):
    del pt_ref
    k_refs = rest[:n_chunk]
    v_refs = rest[n_chunk:2 * n_chunk]
    o_ref = rest[2 * n_chunk]
    m_ref, l_ref, acc_ref, qbd_ref = rest[2 * n_chunk + 1:]
    c = pl.program_id(1)
    page = k_refs[0].shape[2]
    d_qk = k_refs[0].shape[1] // n_maps
    hr = 2 * qr

    @pl.when(c == 0)
    def _():
        m_ref[...] = jnp.full(m_ref.shape, -jnp.inf, F32)
        l_ref[...] = jnp.zeros(l_ref.shape, F32)
        acc_ref[...] = jnp.zeros(acc_ref.shape, F32)

    q_all = q_ref[0]

    @pl.when(c == 0)
    def _():
        q_rep = jnp.concatenate([q_all] * n_maps, axis=1)
        row_map = lax.broadcasted_iota(I32, q_rep.shape, 0) // qr
        col_map = lax.broadcasted_iota(I32, q_rep.shape, 1) // d_qk
        qbd_ref[...] = jnp.where(row_map == col_map, q_rep, jnp.zeros_like(q_rep))

    m_p, l_p, acc = m_ref[...], l_ref[...], acc_ref[...]
    k_cat = jnp.concatenate([k_refs[i][0].astype(BF16) for i in range(n_chunk)], axis=1)
    s = jnp.dot(qbd_ref[...], k_cat, preferred_element_type=F32)
    m_n = jnp.maximum(m_p, jnp.max(s, axis=1, keepdims=True))
    p = jnp.exp(s - m_n)
    al = jnp.exp(m_p - m_n)
    l_p = al * l_p + jnp.sum(p, axis=1, keepdims=True)
    pb = p.astype(BF16)
    pv = []
    for h in range(n_heads):
        vh = jnp.concatenate(
            [v_refs[i][0, pl.ds(h, page, stride=n_heads), :].astype(BF16) for i in range(n_chunk)],
            axis=0)
        pv.append(jnp.dot(pb[h * hr:(h + 1) * hr], vh, preferred_element_type=F32))
    acc = al * acc + jnp.concatenate(pv, axis=0)
    m_p = m_n
    m_ref[...] = m_p
    l_ref[...] = l_p
    acc_ref[...] = acc

    @pl.when(c == pl.num_programs(1) - 1)
    def _():
        qf = q_all.astype(F32)
        kn, vn = kn_ref[0], vn_ref[0]
        q_tok = lax.broadcasted_iota(I32, (qf.shape[0], 1), 0) % qr
        s_new = []
        for j in range(n_new):
            parts = []
            for m in range(n_maps):
                krow = kn[j * n_maps + m:j * n_maps + m + 1, :]
                parts.append(jnp.sum(qf[m * qr:(m + 1) * qr] * krow, axis=1, keepdims=True))
            s_new.append(jnp.where(q_tok >= j, jnp.concatenate(parts, axis=0), -jnp.inf))
        m_n = m_p
        for sj in s_new:
            m_n = jnp.maximum(m_n, sj)
        al = jnp.exp(m_p - m_n)
        l_n = al * l_p
        a_n = al * acc
        for j in range(n_new):
            pj = jnp.exp(s_new[j] - m_n)
            l_n = l_n + pj
            a_n = a_n + jnp.concatenate(
                [pj[h * hr:(h + 1) * hr] * vn[j * n_heads + h:j * n_heads + h + 1, :]
                 for h in range(n_heads)], axis=0)
        a_n = a_n / l_n
        lam = _lambda(lq1_ref[...], lk1_ref[...], lq2_ref[...], lk2_ref[...], lam_init)
        d_v = a_n.shape[1]
        for h in range(n_heads):
            o = a_n[h * hr:h * hr + n_new] - lam * a_n[h * hr + qr:h * hr + qr + n_new]
            o_ref[0, :, h * d_v:(h + 1) * d_v] = (
                _rms(o, sw_ref[...]) * (1.0 - lam_init)).astype(o_ref.dtype)


def _attn_sample(pages, q_rows, k_new, v_new, ck, cv, lams, subln_w, *, n_new, n_maps, n_heads, qr,
                 lam_init, n_chunk):
    dec, n_pages = pages.shape
    d_qk = ck.shape[1] // n_maps
    d_v = cv.shape[2]
    rows = n_maps * qr
    pt_flat = pages.reshape(-1)

    def page_map(i, b, c, pt):
        return (pt[b * n_pages + c * n_chunk + i], 0, 0)

    vec = pl.BlockSpec((1, d_qk), lambda b, c, pt: (0, 0))
    k_specs = [pl.BlockSpec((1,) + ck.shape[1:], functools.partial(page_map, i)) for i in range(n_chunk)]
    v_specs = [pl.BlockSpec((1,) + cv.shape[1:], functools.partial(page_map, i)) for i in range(n_chunk)]
    grid_spec = pltpu.PrefetchScalarGridSpec(
        num_scalar_prefetch=1,
        grid=(dec, n_pages // n_chunk),
        in_specs=[
            pl.BlockSpec((1, rows, d_qk), lambda b, c, pt: (b, 0, 0)),
            pl.BlockSpec((1,) + k_new.shape[1:], lambda b, c, pt: (b, 0, 0)),
            pl.BlockSpec((1,) + v_new.shape[1:], lambda b, c, pt: (b, 0, 0)),
            vec, vec, vec, vec,
            pl.BlockSpec((1, d_v), lambda b, c, pt: (0, 0)),
        ] + k_specs + v_specs,
        out_specs=pl.BlockSpec((1, n_new, n_heads * d_v), lambda b, c, pt: (b, 0, 0)),
        scratch_shapes=[pltpu.VMEM((rows, 1), F32), pltpu.VMEM((rows, 1), F32),
                        pltpu.VMEM((rows, d_v), F32), pltpu.VMEM((rows, n_maps * d_qk), BF16)],
    )
    return pl.pallas_call(
        functools.partial(_attn_sample_body, n_chunk=n_chunk, n_new=n_new, n_maps=n_maps,
                          n_heads=n_heads, qr=qr, lam_init=lam_init),
        grid_spec=grid_spec,
        out_shape=jax.ShapeDtypeStruct((dec, n_new, n_heads * d_v), F32),
        compiler_params=_params(("arbitrary", "arbitrary")),
        name="attn_sample",
    )(pt_flat, q_rows, k_new, v_new, *lams, subln_w, *([ck] * n_chunk), *([cv] * n_chunk))


def _ln_swish(y, lw, lb):
    mu = jnp.mean(y, axis=-1, keepdims=True)
    yc = y - mu
    var = jnp.mean(yc * yc, axis=-1, keepdims=True)
    z = yc * lax.rsqrt(var + EPS) * lw + lb
    return z * jax.nn.sigmoid(z)


def _conv_prompt_body(u_ref, up_ref, dw_ref, db_ref, lw_ref, lb_ref, c_ref, buf_ref, y_ref,
                      *, tt, width, halo, rb):
    t = pl.program_id(1)
    buf_ref[0:halo, :] = jnp.where(t == 0, 0.0, up_ref[...])
    buf_ref[halo:halo + tt, :] = u_ref[...]
    off = halo - (width - 1)

    def col_body(cc, carry):
        c0 = pl.multiple_of(cc * LANES, LANES)
        wts = dw_ref[:, pl.ds(c0, LANES)]
        bias = db_ref[:, pl.ds(c0, LANES)]
        for r in range(tt // rb):
            acc = jnp.zeros((rb, LANES), F32)
            for w in range(width):
                lo = r * rb + off + w
                acc = acc + buf_ref[lo:lo + rb, pl.ds(c0, LANES)] * wts[w:w + 1, :]
            y_ref[r * rb:(r + 1) * rb, pl.ds(c0, LANES)] = acc + bias
        return carry

    lax.fori_loop(0, u_ref.shape[1] // LANES, col_body, 0)
    c_ref[...] = _ln_swish(y_ref[...], lw_ref[...], lb_ref[...]).astype(c_ref.dtype)


def _conv_prompt(u, dw_w, dw_b, ln_w, ln_b, *, batch, seq, tt):
    n, ch = u.shape
    width = dw_w.shape[0]
    halo = 32
    nt = seq // tt
    per = tt // halo
    row = pl.BlockSpec((1, ch), lambda b, t: (0, 0))
    return pl.pallas_call(
        functools.partial(_conv_prompt_body, tt=tt, width=width, halo=halo, rb=64),
        grid=(batch, nt),
        in_specs=[
            pl.BlockSpec((tt, ch), lambda b, t: (b * nt + t, 0)),
            pl.BlockSpec((halo, ch), lambda b, t: (jnp.maximum((b * nt + t) * per - 1, 0), 0)),
            pl.BlockSpec((width, ch), lambda b, t: (0, 0)),
            row, row, row,
        ],
        out_specs=pl.BlockSpec((tt, ch), lambda b, t: (b * nt + t, 0)),
        out_shape=jax.ShapeDtypeStruct((n, ch), BF16),
        scratch_shapes=[pltpu.VMEM((halo + tt, ch), F32), pltpu.VMEM((tt, ch), F32)],
        compiler_params=_params(("arbitrary", "arbitrary")),
        name="conv_prompt",
    )(u, u, dw_w, dw_b, ln_w, ln_b)


def _conv_sample_body(up_ref, dw_ref, db_ref, lw_ref, lb_ref, c_ref, y_ref, *, width, n_new):
    length, tb, ch = up_ref.shape

    def col_body(cc, carry):
        c0 = pl.multiple_of(cc * LANES, LANES)
        wts = dw_ref[:, pl.ds(c0, LANES)]
        bias = db_ref[:, pl.ds(c0, LANES)]
        accs = [jnp.zeros((tb, LANES), F32) for _ in range(n_new)]
        for j in range(length):
            x = up_ref[j, :, pl.ds(c0, LANES)]
            for t in range(n_new):
                w = j - t
                if 0 <= w < width:
                    accs[t] = accs[t] + x * wts[w:w + 1, :]
        for t in range(n_new):
            y_ref[t, :, pl.ds(c0, LANES)] = accs[t] + bias
        return carry

    lax.fori_loop(0, ch // LANES, col_body, 0)
    c_ref[...] = _ln_swish(y_ref[...], lw_ref[...], lb_ref[...]).astype(c_ref.dtype)


def _conv_sample(u_pad_t, dw_w, dw_b, ln_w, ln_b, *, n_new, tb):
    length, dec, ch = u_pad_t.shape
    width = dw_w.shape[0]
    row = pl.BlockSpec((1, ch), lambda b: (0, 0))
    return pl.pallas_call(
        functools.partial(_conv_sample_body, width=width, n_new=n_new),
        grid=(dec // tb,),
        in_specs=[
            pl.BlockSpec((length, tb, ch), lambda b: (0, b, 0)),
            pl.BlockSpec((width, ch), lambda b: (0, 0)),
            row, row, row,
        ],
        out_specs=pl.BlockSpec((n_new, tb, ch), lambda b: (0, b, 0)),
        out_shape=jax.ShapeDtypeStruct((n_new, dec, ch), BF16),
        scratch_shapes=[pltpu.VMEM((n_new, tb, ch), F32)],
        compiler_params=_params(("arbitrary",)),
        name="conv_sample",
    )(u_pad_t, dw_w, dw_b, ln_w, ln_b)


def _wo_body(o_ref, c_ref, x_ref, wo_ref, nw_ref, wr_ref, br_ref, *rest, aw, top_k):
    x1_ref, xf_ref, idx_ref, gate_ref = rest[-4:]
    attn = (jnp.dot(o_ref[...], wo_ref[0:aw, :], preferred_element_type=F32)
            + jnp.dot(c_ref[...], wo_ref[aw:, :], preferred_element_type=F32))
    x1 = x_ref[...] + attn
    x1_ref[...] = x1
    xf = _rms(x1, nw_ref[...])
    xf_ref[...] = xf
    xh = xf.astype(BF16)
    xl = (xf - xh.astype(F32)).astype(BF16)
    wr = wr_ref[...]
    wh = wr.astype(BF16)
    wl = (wr - wh.astype(F32)).astype(BF16)
    logits = (jnp.dot(xh, wh, preferred_element_type=F32) + jnp.dot(xl, wh, preferred_element_type=F32)
              + jnp.dot(xh, wl, preferred_element_type=F32)) + br_ref[...]
    n_exp = logits.shape[1]
    lane_e = lax.broadcasted_iota(I32, logits.shape, 1)
    lane_o = lax.broadcasted_iota(I32, idx_ref.shape, 1)
    vals = logits
    tops, idxs = [], []
    for _ in range(top_k):
        mx = jnp.max(vals, axis=1, keepdims=True)
        am = jnp.min(jnp.where(vals == mx, lane_e, n_exp), axis=1, keepdims=True)
        tops.append(mx)
        idxs.append(am)
        vals = jnp.where(lane_e == am, -jnp.inf, vals)
    es = [jnp.exp(t - tops[0]) for t in tops]
    den = es[0]
    for e in es[1:]:
        den = den + e
    idx_out = jnp.zeros(idx_ref.shape, I32)
    gate_out = jnp.zeros(gate_ref.shape, F32)
    for k in range(top_k):
        idx_out = jnp.where(lane_o == k, idxs[k], idx_out)
        gate_out = jnp.where(lane_o == k, es[k] / den, gate_out)
    idx_ref[...] = idx_out
    gate_ref[...] = gate_out


def _wo_router(o, c, x2d, w_o_bf, norm_w, w_router, b_router, *, tt, xf_rows, xf_row0=0, xf_buf=None):
    n, d = x2d.shape
    aw = o.shape[1]
    n_exp = w_router.shape[1]
    blk0 = xf_row0 // tt
    row = pl.BlockSpec((tt, d), lambda i: (i, 0))
    half = pl.BlockSpec((tt, aw), lambda i: (i, 0))
    small = pl.BlockSpec((tt, LANES), lambda i: (i, 0))
    in_specs = [
        half, pl.BlockSpec((tt, d - aw), lambda i: (i, 0)), row,
        pl.BlockSpec((d, d), lambda i: (0, 0)),
        pl.BlockSpec((1, d), lambda i: (0, 0)),
        pl.BlockSpec((d, n_exp), lambda i: (0, 0)),
        pl.BlockSpec((1, n_exp), lambda i: (0, 0)),
    ]
    args = [o, c, x2d, w_o_bf, norm_w, w_router, b_router]
    aliases = {}
    if xf_buf is not None:
        in_specs.append(pl.BlockSpec(memory_space=pl.ANY))
        args.append(xf_buf)
        aliases = {len(args) - 1: 1}
    return pl.pallas_call(
        functools.partial(_wo_body, aw=aw, top_k=TOP_K),
        grid=(n // tt,),
        in_specs=in_specs,
        out_specs=[row, pl.BlockSpec((tt, d), lambda i: (blk0 + i, 0)), small, small],
        out_shape=[
            jax.ShapeDtypeStruct((n, d), F32),
            jax.ShapeDtypeStruct((xf_rows, d), F32),
            jax.ShapeDtypeStruct((n, LANES), I32),
            jax.ShapeDtypeStruct((n, LANES), F32),
        ],
        input_output_aliases=aliases,
        compiler_params=_params(("arbitrary",)),
        name="wo_router",
    )(*args)


def _route_plan(idx, n_exp, tm, ts):
    n_tok, top_k = idx.shape
    n_asg = n_tok * top_k
    nsub = tm // ts
    flat_e = idx.reshape(-1)
    onehot = (flat_e[:, None] == jnp.arange(n_exp, dtype=I32)[None, :]).astype(I32)
    csum = jnp.cumsum(onehot, axis=0)
    rank = jnp.sum((csum - onehot) * onehot, axis=1)
    counts = csum[-1]
    ntile = (counts + tm - 1) // tm
    first_rows = counts - jnp.maximum(ntile - 1, 0) * tm
    tile_end = jnp.cumsum(ntile)
    tile_beg = tile_end - ntile
    fr = first_rows[flat_e]
    dest = (tile_beg[flat_e] * tm + jnp.where(rank < fr, rank, rank - fr + tm)).astype(I32)

    n_tiles = -(-n_asg // tm) + n_exp
    t_ids = jnp.arange(n_tiles, dtype=I32)
    tile_exp = jnp.minimum(jnp.sum(t_ids[:, None] >= tile_end[None, :], axis=1), n_exp - 1).astype(I32)
    n_used = tile_end[-1]
    first_sub = (first_rows[tile_exp] + ts - 1) // ts
    tile_nsub = jnp.where(t_ids < n_used, jnp.where(t_ids == tile_beg[tile_exp], first_sub, nsub), 0)
    tile_nsub = tile_nsub.astype(I32)
    tile_rows = jnp.where(t_ids < n_used, jnp.where(t_ids == tile_beg[tile_exp], first_rows[tile_exp], tm), 0)

    return dest, tile_exp, tile_nsub, tile_rows.astype(I32), n_used.astype(I32).reshape(1)


def _gather_body(dest_ref, tns_ref, trows_ref, nused_ref, x_hbm, xs_ref, land_ref, sem, tok_ref,
                 *, tm, ts, top_k):
    t = pl.program_id(0)
    n = nused_ref[0]

    @pl.when(t == 0)
    def _():
        def clear(tile, carry):
            for u in range(GATHER_UNROLL):
                tok_ref[tile * tm + trows_ref[tile] + u] = 0
            return carry
        lax.fori_loop(0, n, clear, 0)

        def invert(g, carry):
            for u in range(GATHER_UNROLL):
                a = g * GATHER_UNROLL + u
                tok_ref[dest_ref[a]] = lax.div(a, top_k)
            return carry
        lax.fori_loop(0, dest_ref.shape[0] // GATHER_UNROLL, invert, 0)

    def row_copy(tile, slot, r):
        tok = tok_ref[tile * tm + r]
        return pltpu.make_async_copy(x_hbm.at[pl.ds(tok, 1)], land_ref.at[slot, pl.ds(r, 1)],
                                     sem.at[slot])

    def groups(tile):
        return lax.shift_right_logical(trows_ref[tile] + (GATHER_UNROLL - 1), GATHER_UNROLL.bit_length() - 1)

    def start_all(tile, slot):
        def body(g, carry):
            for u in range(GATHER_UNROLL):
                row_copy(tile, slot, g * GATHER_UNROLL + u).start()
            return carry
        lax.fori_loop(0, groups(tile), body, 0)

    def wait_all(tile, slot):
        def body(g, carry):
            for u in range(GATHER_UNROLL):
                row_copy(tile, slot, g * GATHER_UNROLL + u).wait()
            return carry
        lax.fori_loop(0, groups(tile), body, 0)

    @pl.when(t == 0)
    def _():
        land_ref[...] = jnp.zeros(land_ref.shape, land_ref.dtype)
        start_all(0, 0)

    @pl.when(t + 1 < n)
    def _():
        start_all(t + 1, (t + 1) % 2)

    @pl.when(t < n)
    def _():
        slot = t % 2
        wait_all(t, slot)

        def convert(j, carry):
            r0 = pl.multiple_of(j * ts, ts)
            xs_ref[pl.ds(r0, ts), :] = land_ref[slot, pl.ds(r0, ts), :].astype(xs_ref.dtype)
            return carry

        lax.fori_loop(0, tns_ref[t], convert, 0)


def _gather_rows(xf, dest, tile_nsub, tile_rows, n_used, *, tm, ts, top_k):
    d = xf.shape[1]
    n_rows = tile_nsub.shape[0] * tm
    assert dest.shape[0] % GATHER_UNROLL == 0
    grid_spec = pltpu.PrefetchScalarGridSpec(
        num_scalar_prefetch=4,
        grid=(n_rows // tm,),
        in_specs=[pl.BlockSpec(memory_space=pl.ANY)],
        out_specs=pl.BlockSpec((tm, d), lambda t, tok, tns, trw, nu: (jnp.minimum(t, nu[0] - 1), 0)),
        scratch_shapes=[pltpu.VMEM((2, tm, d), xf.dtype), pltpu.SemaphoreType.DMA((2,)),
                        pltpu.SMEM((n_rows + GATHER_UNROLL,), I32)],
    )
    return pl.pallas_call(
        functools.partial(_gather_body, tm=tm, ts=ts, top_k=top_k),
        grid_spec=grid_spec,
        out_shape=jax.ShapeDtypeStruct((n_rows, d), BF16),
        compiler_params=_params(("arbitrary",)),
        name="moe_gather",
    )(dest, tile_nsub, tile_rows, n_used, xf)


def _expert_changed(texp_ref, t):
    return (t == 0) | (texp_ref[t] != texp_ref[jnp.maximum(t - 1, 0)])


def _moe_up_body(texp_ref, tns_ref, nused_ref, x_ref, wg_ref, wu_ref, bg_ref, bu_ref, a_ref, wbf_ref,
                 *, ts):
    t = pl.program_id(1)

    @pl.when(t < nused_ref[0])
    def _():
        @pl.when(_expert_changed(texp_ref, t))
        def _():
            wbf_ref[0] = wg_ref[0].astype(BF16)
            wbf_ref[1] = wu_ref[0].astype(BF16)

        def sub(j, carry):
            r0 = pl.multiple_of(j * ts, ts)
            x = x_ref[pl.ds(r0, ts), :]
            g = jnp.dot(x, wbf_ref[0], preferred_element_type=F32) + bg_ref[0]
            lin = jnp.dot(x, wbf_ref[1], preferred_element_type=F32) + bu_ref[0]
            g = jnp.minimum(g, SWIGLU_LIMIT)
            lin = jnp.clip(lin, -SWIGLU_LIMIT, SWIGLU_LIMIT)
            a = g * jax.nn.sigmoid(SWIGLU_ALPHA * g) * (lin + 1.0)
            a_ref[pl.ds(r0, ts), :] = a.astype(a_ref.dtype)
            return carry

        lax.fori_loop(0, tns_ref[t], sub, 0)


def _moe_down_body(texp_ref, tns_ref, nused_ref, a_ref, wd_ref, bd_ref, y_ref, wbf_ref, *, ts):
    t = pl.program_id(1)

    @pl.when(t < nused_ref[0])
    def _():
        @pl.when(_expert_changed(texp_ref, t))
        def _():
            wbf_ref[...] = wd_ref[0].astype(BF16)

        def sub(j, carry):
            r0 = pl.multiple_of(j * ts, ts)
            y_ref[pl.ds(r0, ts), :] = jnp.dot(a_ref[pl.ds(r0, ts), :], wbf_ref[...],
                                              preferred_element_type=F32) + bd_ref[0]
            return carry

        lax.fori_loop(0, tns_ref[t], sub, 0)


def _moe_experts(xs, tile_exp, tile_nsub, n_used, w_gate_up, b_gate_up, w_down, b_down, *, tm, ts, tf, tn):
    n_rows, d = xs.shape
    n_exp, _, two_ff = w_gate_up.shape
    d_ff = two_ff // 2
    n_f = d_ff // tf
    n_t = n_rows // tm
    bgu = b_gate_up.reshape(n_exp, 1, two_ff)
    bdn = b_down.reshape(n_exp, 1, d)

    def tile(t, nu):
        return jnp.minimum(t, nu[0] - 1)

    up_spec = pltpu.PrefetchScalarGridSpec(
        num_scalar_prefetch=3,
        grid=(n_f, n_t),
        in_specs=[
            pl.BlockSpec((tm, d), lambda f, t, te, ns, nu: (tile(t, nu), 0)),
            pl.BlockSpec((1, d, tf), lambda f, t, te, ns, nu: (te[tile(t, nu)], 0, f)),
            pl.BlockSpec((1, d, tf), lambda f, t, te, ns, nu: (te[tile(t, nu)], 0, n_f + f)),
            pl.BlockSpec((1, 1, tf), lambda f, t, te, ns, nu: (te[tile(t, nu)], 0, f)),
            pl.BlockSpec((1, 1, tf), lambda f, t, te, ns, nu: (te[tile(t, nu)], 0, n_f + f)),
        ],
        out_specs=pl.BlockSpec((tm, tf), lambda f, t, te, ns, nu: (tile(t, nu), f)),
        scratch_shapes=[pltpu.VMEM((2, d, tf), BF16)],
    )
    act = pl.pallas_call(
        functools.partial(_moe_up_body, ts=ts),
        grid_spec=up_spec,
        out_shape=jax.ShapeDtypeStruct((n_rows, d_ff), BF16),
        compiler_params=_params(("arbitrary", "arbitrary")),
        name="moe_up",
    )(tile_exp, tile_nsub, n_used, xs, w_gate_up, w_gate_up, bgu, bgu)

    n_n = d // tn
    down_spec = pltpu.PrefetchScalarGridSpec(
        num_scalar_prefetch=3,
        grid=(n_n, n_t),
        in_specs=[
            pl.BlockSpec((tm, d_ff), lambda n, t, te, ns, nu: (tile(t, nu), 0)),
            pl.BlockSpec((1, d_ff, tn), lambda n, t, te, ns, nu: (te[tile(t, nu)], 0, n)),
            pl.BlockSpec((1, 1, tn), lambda n, t, te, ns, nu: (te[tile(t, nu)], 0, n)),
        ],
        out_specs=pl.BlockSpec((tm, tn), lambda n, t, te, ns, nu: (tile(t, nu), n)),
        scratch_shapes=[pltpu.VMEM((d_ff, tn), BF16)],
    )
    return pl.pallas_call(
        functools.partial(_moe_down_body, ts=ts),
        grid_spec=down_spec,
        out_shape=jax.ShapeDtypeStruct((n_rows, d), F32),
        compiler_params=_params(("arbitrary", "arbitrary")),
        name="moe_down",
    )(tile_exp, tile_nsub, n_used, act, w_down, bdn)


def _combine_body(dest_ref, x1_ref, gate_ref, nw_ref, yb_hbm, out_ref, rows_ref, sem, *, tt, top_k):
    i = pl.program_id(0)

    def row_copy(r, k):
        d = dest_ref[(i * tt + r) * top_k + k]
        return pltpu.make_async_copy(yb_hbm.at[pl.ds(d, 1)], rows_ref.at[k, pl.ds(r, 1)], sem)

    rows_per_iter = GATHER_UNROLL // top_k

    def issue(g, carry):
        for u in range(rows_per_iter):
            for k in range(top_k):
                row_copy(g * rows_per_iter + u, k).start()
        return carry

    def drain(g, carry):
        for u in range(rows_per_iter):
            for k in range(top_k):
                row_copy(g * rows_per_iter + u, k).wait()
        return carry

    lax.fori_loop(0, tt // rows_per_iter, issue, 0)
    lax.fori_loop(0, tt // rows_per_iter, drain, 0)
    gate = gate_ref[...]
    moe = rows_ref[0] * gate[:, 0:1]
    for k in range(1, top_k):
        moe = moe + rows_ref[k] * gate[:, k:k + 1]
    out_ref[...] = _rms(x1_ref[...] + moe, nw_ref[...])


def _combine(dest, x1, gate, norm_w, yb, *, tt):
    n, d = x1.shape
    grid_spec = pltpu.PrefetchScalarGridSpec(
        num_scalar_prefetch=1,
        grid=(n // tt,),
        in_specs=[
            pl.BlockSpec((tt, d), lambda i, ds: (i, 0)),
            pl.BlockSpec((tt, LANES), lambda i, ds: (i, 0)),
            pl.BlockSpec((1, d), lambda i, ds: (0, 0)),
            pl.BlockSpec(memory_space=pl.ANY),
        ],
        out_specs=pl.BlockSpec((tt, d), lambda i, ds: (i, 0)),
        scratch_shapes=[pltpu.VMEM((TOP_K, tt, d), F32), pltpu.SemaphoreType.DMA],
    )
    return pl.pallas_call(
        functools.partial(_combine_body, tt=tt, top_k=TOP_K),
        grid_spec=grid_spec,
        out_shape=jax.ShapeDtypeStruct((n, d), F32),
        compiler_params=_params(("arbitrary",)),
        name="moe_combine",
    )(dest, x1, gate, norm_w, yb)


def _tile(n, pref):
    t = min(n, pref)
    while n % t:
        t //= 2
    return t


def kernel(x_prompt, x_sample, cache_k, cache_v, state_conv, page_table, norm_mix_w, w_in,
           lambda_q1, lambda_k1, lambda_q2, lambda_k2, subln_w, conv_dw_w, conv_dw_b,
           conv_norm_w, conv_norm_b, w_o, norm_ffn_w, w_router, b_router, w_gate_up, b_gate_up,
           w_down, b_down, norm_final_w):
    batch, seq, d = x_prompt.shape
    dec, n_new, _ = x_sample.shape
    depth, n_pool, page, n_maps, d_qk = cache_k.shape
    n_heads, d_v = cache_v.shape[3:]
    aw = n_heads * d_v
    ch = d - aw
    n_pages = page_table.shape[1]
    past = n_pages * page
    rope_dim = d_qk // 4
    half = rope_dim // 2
    scale = d_qk ** -0.5
    n_exp = w_router.shape[2]
    n_p, n_s = batch * seq, dec * n_new

    h_p = x_prompt.reshape(n_p, d)
    h_s = x_sample.reshape(n_s, d)
    tab_p = _rope_tables(jnp.arange(seq, dtype=I32), d_qk, rope_dim)
    tab_s = _rope_tables(past + jnp.arange(n_s, dtype=I32) % n_new, d_qk, rope_dim)
    outs = {k: [] for k in ("kp", "vp", "cp", "ks", "vs", "cs")}

    for l in range(depth):
        lam_init = 0.8 - 0.6 * math.exp(-0.3 * l)
        lams = tuple(v[l].reshape(1, d_qk) for v in (lambda_q1, lambda_k1, lambda_q2, lambda_k2))
        sw = subln_w[l].reshape(1, d_v)
        w_in_bf = w_in[l].astype(BF16)
        w_o_bf = w_o[l].astype(BF16)
        nmw = norm_mix_w[l].reshape(1, d)
        dw_b = conv_dw_b[l].reshape(1, ch)
        ln_w = conv_norm_w[l].reshape(1, ch)
        ln_b = conv_norm_b[l].reshape(1, ch)

        tt_p = _tile(seq, 512)
        q_p, k_p, v_p, u_p = _proj(h_p, nmw, w_in_bf, tab_p, scale=scale, half=half, tt=tt_p, seq_t=seq)
        tt_s = _tile(n_s, 512)
        q_s, k_s, v_s, u_s = _proj(h_s, nmw, w_in_bf, tab_s, scale=scale, half=half, tt=tt_s)

        o_p = _attn_prompt(q_p, k_p, v_p, lams, sw, batch=batch, seq=seq, n_heads=n_heads,
                           d_qk=d_qk, lam_init=lam_init, tq=_tile(seq, 512))
        q_rows = q_s.reshape(dec, n_new, n_maps, d_qk).transpose(0, 2, 1, 3)
        q_rows = jnp.pad(q_rows, ((0, 0), (0, 0), (0, QR - n_new), (0, 0))).reshape(dec, n_maps * QR, d_qk)
        ck = cache_k.transpose(0, 1, 3, 4, 2).reshape(depth * n_pool, n_maps * d_qk, page)
        cv = cache_v.reshape(depth * n_pool, page * n_heads, d_v)
        o_s = _attn_sample(page_table + l * n_pool, q_rows, k_s.reshape(dec, n_new * n_maps, d_qk),
                           v_s.reshape(dec, n_new * n_heads, d_v), ck, cv, lams, sw, n_new=n_new,
                           n_maps=n_maps, n_heads=n_heads, qr=QR, lam_init=lam_init,
                           n_chunk=_tile(n_pages, 16))
        o_s = o_s.reshape(n_s, aw).astype(BF16)

        c_p = _conv_prompt(u_p, conv_dw_w[l], dw_b, ln_w, ln_b, batch=batch, seq=seq,
                           tt=_tile(seq, 256))
        u_s3 = u_s.reshape(dec, n_new, ch)
        u_pad_s = jnp.concatenate([state_conv[l], u_s3], axis=1)
        c_s = _conv_sample(u_pad_s.transpose(1, 0, 2), conv_dw_w[l], dw_b, ln_w, ln_b,
                           n_new=n_new, tb=_tile(dec, 16))
        c_s = c_s.transpose(1, 0, 2).reshape(n_s, ch)

        nfw = norm_ffn_w[l].reshape(1, d)
        br = b_router[l].reshape(1, n_exp)
        assert n_p % tt_s == 0
        x1_p, xf_all, idx_p, gate_p = _wo_router(o_p, c_p, h_p, w_o_bf, nfw, w_router[l], br, tt=tt_p,
                                                 xf_rows=n_p + n_s)
        x1_s, xf_all, idx_s, gate_s = _wo_router(o_s, c_s, h_s, w_o_bf, nfw, w_router[l], br, tt=tt_s,
                                                 xf_rows=n_p + n_s, xf_row0=n_p, xf_buf=xf_all)

        idx_all = jnp.concatenate([idx_p[:, :TOP_K], idx_s[:, :TOP_K]], axis=0)
        dest, tile_exp, tile_nsub, tile_rows, n_used = _route_plan(idx_all, n_exp, MOE_TILE, MOE_SUBTILE)
        xs = _gather_rows(xf_all, dest, tile_nsub, tile_rows, n_used, tm=MOE_TILE, ts=MOE_SUBTILE,
                          top_k=TOP_K)
        yb = _moe_experts(xs, tile_exp, tile_nsub, n_used, w_gate_up[l], b_gate_up[l], w_down[l],
                          b_down[l], tm=MOE_TILE, ts=MOE_SUBTILE, tf=_tile(w_down.shape[2], 1024),
                          tn=_tile(d, 1024))

        last = l == depth - 1
        nw_out = norm_final_w.reshape(1, d) if last else jnp.ones((1, d), F32)
        assert last, "the combine kernel fuses the final norm; deeper stacks need an un-normed variant"
        h_p = _combine(dest[:n_p * TOP_K], x1_p, gate_p, nw_out, yb, tt=_tile(n_p, 512))
        h_s = _combine(dest[n_p * TOP_K:], x1_s, gate_s, nw_out, yb, tt=_tile(n_s, 512))

        outs["kp"].append(k_p.reshape(batch, n_maps, d_qk, seq).transpose(0, 3, 1, 2))
        outs["vp"].append(v_p.reshape(batch, seq, n_heads, d_v))
        outs["cp"].append(u_p.reshape(batch, seq, ch)[:, seq - (conv_dw_w.shape[1] - 1):])
        outs["ks"].append(k_s.reshape(dec, n_new, n_maps, d_qk))
        outs["vs"].append(v_s.reshape(dec, n_new, n_heads, d_v))
        outs["cs"].append(u_pad_s[:, n_new:])

    return (h_p.reshape(batch, seq, d), h_s.reshape(dec, n_new, d),
            jnp.stack(outs["kp"]), jnp.stack(outs["vp"]), jnp.stack(outs["cp"]),
            jnp.stack(outs["ks"]), jnp.stack(outs["vs"]), jnp.stack(outs["cs"]))
```

```python
import functools
import math

import jax
import jax.numpy as jnp
from jax import lax
from jax.experimental import pallas as pl
from jax.experimental.pallas import tpu as pltpu

F32 = jnp.float32
BF16 = jnp.bfloat16
I32 = jnp.int32

EPS = 1e-5
ROPE_THETA = 500000.0
TOP_K = 4
SWIGLU_LIMIT = 7.0
SWIGLU_ALPHA = 1.702
LANES = 128
VMEM_LIMIT = 56 * 1024 * 1024
QR = 8
MOE_TILE = 768
MOE_SUBTILE = 256
GATHER_UNROLL = 8


def _params(sem, vmem=VMEM_LIMIT):
    return pltpu.CompilerParams(dimension_semantics=sem, vmem_limit_bytes=vmem)


def _nt_dot(a, b):
    return lax.dot_general(a, b, (((1,), (1,)), ((), ())), preferred_element_type=F32)


def _rms(x, w):
    ms = jnp.mean(x * x, axis=-1, keepdims=True)
    return x * lax.rsqrt(ms + EPS) * w


def _lambda(lq1, lk1, lq2, lk2, lam_init):
    a = jnp.sum(lq1 * lk1, axis=-1, keepdims=True)
    b = jnp.sum(lq2 * lk2, axis=-1, keepdims=True)
    return jnp.exp(a) - jnp.exp(b) + lam_init


def _proj_body(x_ref, nw_ref, wa_ref, wb_ref, cs_ref, s1_ref, s2_ref,
               q_ref, k_ref, v_ref, u_ref, xn_ref, *, scale, half, k_transposed):
    j = pl.program_id(1)

    @pl.when(j == 0)
    def _():
        xn_ref[...] = _rms(x_ref[...], nw_ref[...]).astype(BF16)

    def rope_store(dst_ref, p, mul, transposed=False):
        cs, s1, s2 = cs_ref[...], s1_ref[...], s2_ref[...]
        for c in range(p.shape[1] // LANES):
            seg = p[:, c * LANES:(c + 1) * LANES]
            r = (seg * cs + pltpu.roll(seg, LANES - half, 1) * s1
                 + pltpu.roll(seg, half, 1) * s2)
            if mul is not None:
                r = r * mul
            if transposed:
                dst_ref[0, c * LANES:(c + 1) * LANES, :] = r.T.astype(dst_ref.dtype)
            else:
                dst_ref[:, c * LANES:(c + 1) * LANES] = r.astype(dst_ref.dtype)

    @pl.when(j == 0)
    def _():
        p = jnp.dot(xn_ref[...], wa_ref[...], preferred_element_type=F32)
        rope_store(q_ref, p, scale)

    @pl.when(j == 1)
    def _():
        p = jnp.dot(xn_ref[...], wa_ref[...], preferred_element_type=F32)
        rope_store(k_ref, p, None, transposed=k_transposed)

    @pl.when(j == 2)
    def _():
        v_ref[...] = jnp.dot(xn_ref[...], wa_ref[...], preferred_element_type=F32)

    @pl.when(j == 3)
    def _():
        xn = xn_ref[...]
        val = jnp.dot(xn, wa_ref[...], preferred_element_type=F32)
        gate = jnp.dot(xn, wb_ref[...], preferred_element_type=F32)
        u_ref[...] = val * jax.nn.sigmoid(gate)


def _proj(x2d, norm_w, w_in_bf, tables, *, scale, half, tt, seq_t=None):
    n, d = x2d.shape
    w = w_in_bf.shape[1] // 5
    if seq_t is None:
        k_shape, k_spec = (n, w), pl.BlockSpec((tt, w), lambda i, j: (i, 0))
    else:
        nt = seq_t // tt
        k_shape, k_spec = (n // seq_t, w, seq_t), pl.BlockSpec((1, w, tt), lambda i, j: (i // nt, 0, i % nt))
    cs, s1, s2 = tables
    n_pos_blocks = cs.shape[0] // tt
    tab_spec = pl.BlockSpec((tt, LANES), lambda i, j: (i % n_pos_blocks, 0))
    out_spec = pl.BlockSpec((tt, w), lambda i, j: (i, 0))
    return pl.pallas_call(
        functools.partial(_proj_body, scale=scale, half=half, k_transposed=seq_t is not None),
        grid=(n // tt, 4),
        in_specs=[
            pl.BlockSpec((tt, d), lambda i, j: (i, 0)),
            pl.BlockSpec((1, d), lambda i, j: (0, 0)),
            pl.BlockSpec((d, w), lambda i, j: (0, j)),
            pl.BlockSpec((d, w), lambda i, j: (0, 4)),
            tab_spec, tab_spec, tab_spec,
        ],
        out_specs=[out_spec, k_spec, out_spec, out_spec],
        out_shape=[
            jax.ShapeDtypeStruct((n, w), BF16),
            jax.ShapeDtypeStruct(k_shape, F32),
            jax.ShapeDtypeStruct((n, w), F32),
            jax.ShapeDtypeStruct((n, w), F32),
        ],
        scratch_shapes=[pltpu.VMEM((tt, d), BF16)],
        compiler_params=_params(("arbitrary", "arbitrary")),
        name="proj",
    )(x2d, norm_w, w_in_bf, w_in_bf, cs, s1, s2)


def _rope_tables(pos, d_qk, rope_dim):
    half = rope_dim // 2
    inv = ROPE_THETA ** (-jnp.arange(0, rope_dim, 2, dtype=F32) / rope_dim)
    ang = pos.astype(F32)[:, None] * inv[None, :]
    cos, sin = jnp.cos(ang), jnp.sin(ang)
    p = pos.shape[0]
    zh = jnp.zeros((p, half), F32)
    zr = jnp.zeros((p, d_qk - rope_dim), F32)
    cs = jnp.concatenate([cos, cos, jnp.ones((p, d_qk - rope_dim), F32)], axis=-1)
    s1 = jnp.concatenate([-sin, zh, zr], axis=-1)
    s2 = jnp.concatenate([zh, sin, zr], axis=-1)
    rep = LANES // d_qk
    return tuple(jnp.tile(t, (1, rep)) for t in (cs, s1, s2))


def _attn_prompt_body(q_ref, k_ref, v_ref, lq1_ref, lk1_ref, lq2_ref, lk2_ref, sw_ref,
                      o_ref, kb_ref, vb_ref, m_ref, l_ref, acc_ref, *, tq, tk, d_qk, lam_init):
    qi = pl.program_id(2)

    @pl.when(qi == 0)
    def _():
        kb_ref[...] = k_ref[0].astype(BF16)
        vb_ref[...] = v_ref[...].astype(BF16)

    q = q_ref[...]
    lane = lax.broadcasted_iota(I32, q.shape, 1)
    zero = jnp.zeros_like(q)
    qm = (jnp.where(lane < d_qk, q, zero), jnp.where(lane >= d_qk, q, zero))
    m_ref[...] = jnp.full(m_ref.shape, -jnp.inf, F32)
    l_ref[...] = jnp.zeros(l_ref.shape, F32)
    acc_ref[...] = jnp.zeros(acc_ref.shape, F32)

    def step(c, masked):
        k0 = pl.multiple_of(c * tk, tk)
        kc_t = kb_ref[:, pl.ds(k0, tk)]
        vc = vb_ref[pl.ds(k0, tk), :]
        for mi in range(2):
            s = jnp.dot(qm[mi], kc_t, preferred_element_type=F32)
            if masked:
                row = lax.broadcasted_iota(I32, (tq, tk), 0)
                col = lax.broadcasted_iota(I32, (tq, tk), 1)
                s = jnp.where(col + c * tk <= row + qi * tq, s, -jnp.inf)
            m_p = m_ref[mi]
            m_n = jnp.maximum(m_p, jnp.max(s, axis=1, keepdims=True))
            p = jnp.exp(s - jnp.concatenate([m_n] * (tk // LANES), axis=1))
            al = jnp.exp(m_p - m_n)
            l_ref[mi] = al * l_ref[mi] + jnp.sum(p, axis=1, keepdims=True)
            acc_ref[mi] = al * acc_ref[mi] + jnp.dot(p.astype(BF16), vc, preferred_element_type=F32)
            m_ref[mi] = m_n

    def full_step(c, carry):
        step(c, False)
        return carry

    n_full = qi * (tq // tk)
    lax.fori_loop(0, n_full, full_step, 0)
    for dgl in range(tq // tk):
        step(n_full + dgl, True)

    lam = _lambda(lq1_ref[...], lk1_ref[...], lq2_ref[...], lk2_ref[...], lam_init)
    o = acc_ref[0] / l_ref[0] - lam * (acc_ref[1] / l_ref[1])
    o_ref[...] = (_rms(o, sw_ref[...]) * (1.0 - lam_init)).astype(o_ref.dtype)


def _attn_prompt(q, k, v, lams, subln_w, *, batch, seq, n_heads, d_qk, lam_init, tq):
    n, w = q.shape
    d_v = w // n_heads
    nq = seq // tq
    vec = pl.BlockSpec((1, d_qk), lambda b, h, i: (0, 0))
    return pl.pallas_call(
        functools.partial(_attn_prompt_body, tq=tq, tk=tq, d_qk=d_qk, lam_init=lam_init),
        grid=(batch, n_heads, nq),
        in_specs=[
            pl.BlockSpec((tq, d_v), lambda b, h, i: (b * nq + i, h)),
            pl.BlockSpec((1, d_v, seq), lambda b, h, i: (b, h, 0)),
            pl.BlockSpec((seq, d_v), lambda b, h, i: (b, h)),
            vec, vec, vec, vec,
            pl.BlockSpec((1, d_v), lambda b, h, i: (0, 0)),
        ],
        out_specs=pl.BlockSpec((tq, d_v), lambda b, h, i: (b * nq + i, h)),
        out_shape=jax.ShapeDtypeStruct((n, w), BF16),
        scratch_shapes=[pltpu.VMEM((d_v, seq), BF16), pltpu.VMEM((seq, d_v), BF16),
                        pltpu.VMEM((2, tq, LANES), F32), pltpu.VMEM((2, tq, LANES), F32),
                        pltpu.VMEM((2, tq, d_v), F32)],
        compiler_params=_params(("arbitrary", "arbitrary", "arbitrary")),
        name="attn_prompt",
    )(q, k, v, *lams, subln_w)


def _attn_sample_body(pt_ref, q_ref, kn_ref, vn_ref, lq1_ref, lk1_ref, lq2_ref, lk2_ref, sw_ref,
                      *rest, n_chunk, n_new, n_maps, n_heads, qr, lam_init):
    del pt_ref
    k_refs = rest[:n_chunk]
    v_refs = rest[n_chunk:2 * n_chunk]
    o_ref = rest[2 * n_chunk]
    m_ref, l_ref, acc_ref, qbd_ref = rest[2 * n_chunk + 1:]
    c = pl.program_id(1)
    page = k_refs[0].shape[2]
    d_qk = k_refs[0].shape[1] // n_maps
    hr = 2 * qr

    @pl.when(c == 0)
    def _():
        m_ref[...] = jnp.full(m_ref.shape, -jnp.inf, F32)
        l_ref[...] = jnp.zeros(l_ref.shape, F32)
        acc_ref[...] = jnp.zeros(acc_ref.shape, F32)

    q_all = q_ref[0]

    @pl.when(c == 0)
    def _():
        q_rep = jnp.concatenate([q_all] * n_maps, axis=1)
        row_map = lax.broadcasted_iota(I32, q_rep.shape, 0) // qr
        col_map = lax.broadcasted_iota(I32, q_rep.shape, 1) // d_qk
        qbd_ref[...] = jnp.where(row_map == col_map, q_rep, jnp.zeros_like(q_rep))

    m_p, l_p, acc = m_ref[...], l_ref[...], acc_ref[...]
    k_cat = jnp.concatenate([k_refs[i][0].astype(BF16) for i in range(n_chunk)], axis=1)
    s = jnp.dot(qbd_ref[...], k_cat, preferred_element_type=F32)
    m_n = jnp.maximum(m_p, jnp.max(s, axis=1, keepdims=True))
    p = jnp.exp(s - m_n)
    al = jnp.exp(m_p - m_n)
    l_p = al * l_p + jnp.sum(p, axis=1, keepdims=True)
    pb = p.astype(BF16)
    pv = []
    for h in range(n_heads):
        vh = jnp.concatenate(
            [v_refs[i][0, pl.ds(h, page, stride=n_heads), :].astype(BF16) for i in range(n_chunk)],
            axis=0)
        pv.append(jnp.dot(pb[h * hr:(h + 1) * hr], vh, preferred_element_type=F32))
    acc = al * acc + jnp.concatenate(pv, axis=0)
    m_p = m_n
    m_ref[...] = m_p
    l_ref[...] = l_p
    acc_ref[...] = acc

    @pl.when(c == pl.num_programs(1) - 1)
    def _():
        qf = q_all.astype(F32)
        kn, vn = kn_ref[0], vn_ref[0]
        q_tok = lax.broadcasted_iota(I32, (qf.shape[0], 1), 0) % qr
        s_new = []
        for j in range(n_new):
            parts = []
            for m in range(n_maps):
                krow = kn[j * n_maps + m:j * n_maps + m + 1, :]
                parts.append(jnp.sum(qf[m * qr:(m + 1) * qr] * krow, axis=1, keepdims=True))
            s_new.append(jnp.where(q_tok >= j, jnp.concatenate(parts, axis=0), -jnp.inf))
        m_n = m_p
        for sj in s_new:
            m_n = jnp.maximum(m_n, sj)
        al = jnp.exp(m_p - m_n)
        l_n = al * l_p
        a_n = al * acc
        for j in range(n_new):
            pj = jnp.exp(s_new[j] - m_n)
            l_n = l_n + pj
            a_n = a_n + jnp.concatenate(
                [pj[h * hr:(h + 1) * hr] * vn[j * n_heads + h:j * n_heads + h + 1, :]
                 for h in range(n_heads)], axis=0)
        a_n = a_n / l_n
        lam = _lambda(lq1_ref[...], lk1_ref[...], lq2_ref[...], lk2_ref[...], lam_init)
        d_v = a_n.shape[1]
        for h in range(n_heads):
            o = a_n[h * hr:h * hr + n_new] - lam * a_n[h * hr + qr:h * hr + qr + n_new]
            o_ref[0, :, h * d_v:(h + 1) * d_v] = (
                _rms(o, sw_ref[...]) * (1.0 - lam_init)).astype(o_ref.dtype)


def _attn_sample(pages, q_rows, k_new, v_new, ck, cv, lams, subln_w, *, n_new, n_maps, n_heads, qr,
                 lam_init, n_chunk):
    dec, n_pages = pages.shape
    d_qk = ck.shape[1] // n_maps
    d_v = cv.shape[2]
    rows = n_maps * qr
    pt_flat = pages.reshape(-1)

    def page_map(i, b, c, pt):
        return (pt[b * n_pages + c * n_chunk + i], 0, 0)

    vec = pl.BlockSpec((1, d_qk), lambda b, c, pt: (0, 0))
    k_specs = [pl.BlockSpec((1,) + ck.shape[1:], functools.partial(page_map, i)) for i in range(n_chunk)]
    v_specs = [pl.BlockSpec((1,) + cv.shape[1:], functools.partial(page_map, i)) for i in range(n_chunk)]
    grid_spec = pltpu.PrefetchScalarGridSpec(
        num_scalar_prefetch=1,
        grid=(dec, n_pages // n_chunk),
        in_specs=[
            pl.BlockSpec((1, rows, d_qk), lambda b, c, pt: (b, 0, 0)),
            pl.BlockSpec((1,) + k_new.shape[1:], lambda b, c, pt: (b, 0, 0)),
            pl.BlockSpec((1,) + v_new.shape[1:], lambda b, c, pt: (b, 0, 0)),
            vec, vec, vec, vec,
            pl.BlockSpec((1, d_v), lambda b, c, pt: (0, 0)),
        ] + k_specs + v_specs,
        out_specs=pl.BlockSpec((1, n_new, n_heads * d_v), lambda b, c, pt: (b, 0, 0)),
        scratch_shapes=[pltpu.VMEM((rows, 1), F32), pltpu.VMEM((rows, 1), F32),
                        pltpu.VMEM((rows, d_v), F32), pltpu.VMEM((rows, n_maps * d_qk), BF16)],
    )
    return pl.pallas_call(
        functools.partial(_attn_sample_body, n_chunk=n_chunk, n_new=n_new, n_maps=n_maps,
                          n_heads=n_heads, qr=qr, lam_init=lam_init),
        grid_spec=grid_spec,
        out_shape=jax.ShapeDtypeStruct((dec, n_new, n_heads * d_v), F32),
        compiler_params=_params(("arbitrary", "arbitrary")),
        name="attn_sample",
    )(pt_flat, q_rows, k_new, v_new, *lams, subln_w, *([ck] * n_chunk), *([cv] * n_chunk))


def _ln_swish(y, lw, lb):
    mu = jnp.mean(y, axis=-1, keepdims=True)
    yc = y - mu
    var = jnp.mean(yc * yc, axis=-1, keepdims=True)
    z = yc * lax.rsqrt(var + EPS) * lw + lb
    return z * jax.nn.sigmoid(z)


def _conv_prompt_body(u_ref, up_ref, dw_ref, db_ref, lw_ref, lb_ref, c_ref, buf_ref, y_ref,
                      *, tt, width, halo, rb):
    t = pl.program_id(1)
    buf_ref[0:halo, :] = jnp.where(t == 0, 0.0, up_ref[...])
    buf_ref[halo:halo + tt, :] = u_ref[...]
    off = halo - (width - 1)

    def col_body(cc, carry):
        c0 = pl.multiple_of(cc * LANES, LANES)
        wts = dw_ref[:, pl.ds(c0, LANES)]
        bias = db_ref[:, pl.ds(c0, LANES)]
        for r in range(tt // rb):
            acc = jnp.zeros((rb, LANES), F32)
            for w in range(width):
                lo = r * rb + off + w
                acc = acc + buf_ref[lo:lo + rb, pl.ds(c0, LANES)] * wts[w:w + 1, :]
            y_ref[r * rb:(r + 1) * rb, pl.ds(c0, LANES)] = acc + bias
        return carry

    lax.fori_loop(0, u_ref.shape[1] // LANES, col_body, 0)
    c_ref[...] = _ln_swish(y_ref[...], lw_ref[...], lb_ref[...]).astype(c_ref.dtype)


def _conv_prompt(u, dw_w, dw_b, ln_w, ln_b, *, batch, seq, tt):
    n, ch = u.shape
    width = dw_w.shape[0]
    halo = 32
    nt = seq // tt
    per = tt // halo
    row = pl.BlockSpec((1, ch), lambda b, t: (0, 0))
    return pl.pallas_call(
        functools.partial(_conv_prompt_body, tt=tt, width=width, halo=halo, rb=64),
        grid=(batch, nt),
        in_specs=[
            pl.BlockSpec((tt, ch), lambda b, t: (b * nt + t, 0)),
            pl.BlockSpec((halo, ch), lambda b, t: (jnp.maximum((b * nt + t) * per - 1, 0), 0)),
            pl.BlockSpec((width, ch), lambda b, t: (0, 0)),
            row, row, row,
        ],
        out_specs=pl.BlockSpec((tt, ch), lambda b, t: (b * nt + t, 0)),
        out_shape=jax.ShapeDtypeStruct((n, ch), BF16),
        scratch_shapes=[pltpu.VMEM((halo + tt, ch), F32), pltpu.VMEM((tt, ch), F32)],
        compiler_params=_params(("arbitrary", "arbitrary")),
        name="conv_prompt",
    )(u, u, dw_w, dw_b, ln_w, ln_b)


def _conv_sample_body(up_ref, dw_ref, db_ref, lw_ref, lb_ref, c_ref, y_ref, *, width, n_new):
    length, tb, ch = up_ref.shape

    def col_body(cc, carry):
        c0 = pl.multiple_of(cc * LANES, LANES)
        wts = dw_ref[:, pl.ds(c0, LANES)]
        bias = db_ref[:, pl.ds(c0, LANES)]
        accs = [jnp.zeros((tb, LANES), F32) for _ in range(n_new)]
        for j in range(length):
            x = up_ref[j, :, pl.ds(c0, LANES)]
            for t in range(n_new):
                w = j - t
                if 0 <= w < width:
                    accs[t] = accs[t] + x * wts[w:w + 1, :]
        for t in range(n_new):
            y_ref[t, :, pl.ds(c0, LANES)] = accs[t] + bias
        return carry

    lax.fori_loop(0, ch // LANES, col_body, 0)
    c_ref[...] = _ln_swish(y_ref[...], lw_ref[...], lb_ref[...]).astype(c_ref.dtype)


def _conv_sample(u_pad_t, dw_w, dw_b, ln_w, ln_b, *, n_new, tb):
    length, dec, ch = u_pad_t.shape
    width = dw_w.shape[0]
    row = pl.BlockSpec((1, ch), lambda b: (0, 0))
    return pl.pallas_call(
        functools.partial(_conv_sample_body, width=width, n_new=n_new),
        grid=(dec // tb,),
        in_specs=[
            pl.BlockSpec((length, tb, ch), lambda b: (0, b, 0)),
            pl.BlockSpec((width, ch), lambda b: (0, 0)),
            row, row, row,
        ],
        out_specs=pl.BlockSpec((n_new, tb, ch), lambda b: (0, b, 0)),
        out_shape=jax.ShapeDtypeStruct((n_new, dec, ch), BF16),
        scratch_shapes=[pltpu.VMEM((n_new, tb, ch), F32)],
        compiler_params=_params(("arbitrary",)),
        name="conv_sample",
    )(u_pad_t, dw_w, dw_b, ln_w, ln_b)


def _wo_body(o_ref, c_ref, x_ref, wo_ref, nw_ref, wr_ref, br_ref, *rest, aw, top_k):
    x1_ref, xf_ref, idx_ref, gate_ref = rest[-4:]
    attn = (jnp.dot(o_ref[...], wo_ref[0:aw, :], preferred_element_type=F32)
            + jnp.dot(c_ref[...], wo_ref[aw:, :], preferred_element_type=F32))
    x1 = x_ref[...] + attn
    x1_ref[...] = x1
    xf = _rms(x1, nw_ref[...])
    xf_ref[...] = xf
    xh = xf.astype(BF16)
    xl = (xf - xh.astype(F32)).astype(BF16)
    wr = wr_ref[...]
    wh = wr.astype(BF16)
    wl = (wr - wh.astype(F32)).astype(BF16)
    logits = (jnp.dot(xh, wh, preferred_element_type=F32) + jnp.dot(xl, wh, preferred_element_type=F32)
              + jnp.dot(xh, wl, preferred_element_type=F32)) + br_ref[...]
    n_exp = logits.shape[1]
    lane_e = lax.broadcasted_iota(I32, logits.shape, 1)
    lane_o = lax.broadcasted_iota(I32, idx_ref.shape, 1)
    vals = logits
    tops, idxs = [], []
    for _ in range(top_k):
        mx = jnp.max(vals, axis=1, keepdims=True)
        am = jnp.min(jnp.where(vals == mx, lane_e, n_exp), axis=1, keepdims=True)
        tops.append(mx)
        idxs.append(am)
        vals = jnp.where(lane_e == am, -jnp.inf, vals)
    es = [jnp.exp(t - tops[0]) for t in tops]
    den = es[0]
    for e in es[1:]:
        den = den + e
    idx_out = jnp.zeros(idx_ref.shape, I32)
    gate_out = jnp.zeros(gate_ref.shape, F32)
    for k in range(top_k):
        idx_out = jnp.where(lane_o == k, idxs[k], idx_out)
        gate_out = jnp.where(lane_o == k, es[k] / den, gate_out)
    idx_ref[...] = idx_out
    gate_ref[...] = gate_out


def _wo_router(o, c, x2d, w_o_bf, norm_w, w_router, b_router, *, tt, xf_rows, xf_row0=0, xf_buf=None):
    n, d = x2d.shape
    aw = o.shape[1]
    n_exp = w_router.shape[1]
    blk0 = xf_row0 // tt
    row = pl.BlockSpec((tt, d), lambda i: (i, 0))
    half = pl.BlockSpec((tt, aw), lambda i: (i, 0))
    small = pl.BlockSpec((tt, LANES), lambda i: (i, 0))
    in_specs = [
        half, pl.BlockSpec((tt, d - aw), lambda i: (i, 0)), row,
        pl.BlockSpec((d, d), lambda i: (0, 0)),
        pl.BlockSpec((1, d), lambda i: (0, 0)),
        pl.BlockSpec((d, n_exp), lambda i: (0, 0)),
        pl.BlockSpec((1, n_exp), lambda i: (0, 0)),
    ]
    args = [o, c, x2d, w_o_bf, norm_w, w_router, b_router]
    aliases = {}
    if xf_buf is not None:
        in_specs.append(pl.BlockSpec(memory_space=pl.ANY))
        args.append(xf_buf)
        aliases = {len(args) - 1: 1}
    return pl.pallas_call(
        functools.partial(_wo_body, aw=aw, top_k=TOP_K),
        grid=(n // tt,),
        in_specs=in_specs,
        out_specs=[row, pl.BlockSpec((tt, d), lambda i: (blk0 + i, 0)), small, small],
        out_shape=[
            jax.ShapeDtypeStruct((n, d), F32),
            jax.ShapeDtypeStruct((xf_rows, d), F32),
            jax.ShapeDtypeStruct((n, LANES), I32),
            jax.ShapeDtypeStruct((n, LANES), F32),
        ],
        input_output_aliases=aliases,
        compiler_params=_params(("arbitrary",)),
        name="wo_router",
    )(*args)


def _route_plan(idx, n_exp, tm, ts):
    n_tok, top_k = idx.shape
    n_asg = n_tok * top_k
    nsub = tm // ts
    flat_e = idx.reshape(-1)
    onehot = (flat_e[:, None] == jnp.arange(n_exp, dtype=I32)[None, :]).astype(I32)
    csum = jnp.cumsum(onehot, axis=0)
    rank = jnp.sum((csum - onehot) * onehot, axis=1)
    counts = csum[-1]
    ntile = (counts + tm - 1) // tm
    first_rows = counts - jnp.maximum(ntile - 1, 0) * tm
    tile_end = jnp.cumsum(ntile)
    tile_beg = tile_end - ntile
    fr = first_rows[flat_e]
    dest = (tile_beg[flat_e] * tm + jnp.where(rank < fr, rank, rank - fr + tm)).astype(I32)

    n_tiles = -(-n_asg // tm) + n_exp
    t_ids = jnp.arange(n_tiles, dtype=I32)
    tile_exp = jnp.minimum(jnp.sum(t_ids[:, None] >= tile_end[None, :], axis=1), n_exp - 1).astype(I32)
    n_used = tile_end[-1]
    first_sub = (first_rows[tile_exp] + ts - 1) // ts
    tile_nsub = jnp.where(t_ids < n_used, jnp.where(t_ids == tile_beg[tile_exp], first_sub, nsub), 0)
    tile_nsub = tile_nsub.astype(I32)
    tile_rows = jnp.where(t_ids < n_used, jnp.where(t_ids == tile_beg[tile_exp], first_rows[tile_exp], tm), 0)

    return dest, tile_exp, tile_nsub, tile_rows.astype(I32), n_used.astype(I32).reshape(1)


def _gather_body(dest_ref, tns_ref, trows_ref, nused_ref, x_hbm, xs_ref, land_ref, sem, tok_ref,
                 *, tm, ts, top_k):
    t = pl.program_id(0)
    n = nused_ref[0]

    @pl.when(t == 0)
    def _():
        def clear(tile, carry):
            for u in range(GATHER_UNROLL):
                tok_ref[tile * tm + trows_ref[tile] + u] = 0
            return carry
        lax.fori_loop(0, n, clear, 0)

        per_group = GATHER_UNROLL // top_k

        def invert(g, carry):
            for u in range(GATHER_UNROLL):
                tok_ref[dest_ref[g * GATHER_UNROLL + u]] = g * per_group + u // top_k
            return carry
        lax.fori_loop(0, dest_ref.shape[0] // GATHER_UNROLL, invert, 0)

    def row_copy(tile, slot, r):
        tok = tok_ref[tile * tm + r]
        return pltpu.make_async_copy(x_hbm.at[pl.ds(tok, 1)], land_ref.at[slot, pl.ds(r, 1)],
                                     sem.at[slot])

    def groups(tile):
        return lax.shift_right_logical(trows_ref[tile] + (GATHER_UNROLL - 1), GATHER_UNROLL.bit_length() - 1)

    def start_all(tile, slot):
        def body(g, carry):
            for u in range(GATHER_UNROLL):
                row_copy(tile, slot, g * GATHER_UNROLL + u).start()
            return carry
        lax.fori_loop(0, groups(tile), body, 0)

    def wait_all(tile, slot):
        def body(g, carry):
            for u in range(GATHER_UNROLL):
                row_copy(tile, slot, g * GATHER_UNROLL + u).wait()
            return carry
        lax.fori_loop(0, groups(tile), body, 0)

    @pl.when(t == 0)
    def _():
        land_ref[...] = jnp.zeros(land_ref.shape, land_ref.dtype)
        start_all(0, 0)

    @pl.when(t + 1 < n)
    def _():
        start_all(t + 1, (t + 1) % 2)

    @pl.when(t < n)
    def _():
        slot = t % 2
        wait_all(t, slot)

        def convert(j, carry):
            r0 = pl.multiple_of(j * ts, ts)
            xs_ref[pl.ds(r0, ts), :] = land_ref[slot, pl.ds(r0, ts), :].astype(xs_ref.dtype)
            return carry

        lax.fori_loop(0, tns_ref[t], convert, 0)


def _gather_rows(xf, dest, tile_nsub, tile_rows, n_used, *, tm, ts, top_k):
    d = xf.shape[1]
    n_rows = tile_nsub.shape[0] * tm
    assert dest.shape[0] % GATHER_UNROLL == 0 and GATHER_UNROLL % top_k == 0
    grid_spec = pltpu.PrefetchScalarGridSpec(
        num_scalar_prefetch=4,
        grid=(n_rows // tm,),
        in_specs=[pl.BlockSpec(memory_space=pl.ANY)],
        out_specs=pl.BlockSpec((tm, d), lambda t, tok, tns, trw, nu: (jnp.minimum(t, nu[0] - 1), 0)),
        scratch_shapes=[pltpu.VMEM((2, tm, d), xf.dtype), pltpu.SemaphoreType.DMA((2,)),
                        pltpu.SMEM((n_rows + GATHER_UNROLL,), I32)],
    )
    return pl.pallas_call(
        functools.partial(_gather_body, tm=tm, ts=ts, top_k=top_k),
        grid_spec=grid_spec,
        out_shape=jax.ShapeDtypeStruct((n_rows, d), BF16),
        compiler_params=_params(("arbitrary",)),
        name="moe_gather",
    )(dest, tile_nsub, tile_rows, n_used, xf)


def _expert_changed(texp_ref, t):
    return (t == 0) | (texp_ref[t] != texp_ref[jnp.maximum(t - 1, 0)])


def _moe_up_body(texp_ref, tns_ref, nused_ref, x_ref, wg_ref, wu_ref, bg_ref, bu_ref, a_ref, wbf_ref,
                 *, ts):
    t = pl.program_id(1)

    @pl.when(t < nused_ref[0])
    def _():
        @pl.when(_expert_changed(texp_ref, t))
        def _():
            wbf_ref[0] = wg_ref[0].astype(BF16)
            wbf_ref[1] = wu_ref[0].astype(BF16)

        def sub(j, carry):
            r0 = pl.multiple_of(j * ts, ts)
            x = x_ref[pl.ds(r0, ts), :]
            g = jnp.dot(x, wbf_ref[0], preferred_element_type=F32) + bg_ref[0]
            lin = jnp.dot(x, wbf_ref[1], preferred_element_type=F32) + bu_ref[0]
            g = jnp.minimum(g, SWIGLU_LIMIT)
            lin = jnp.clip(lin, -SWIGLU_LIMIT, SWIGLU_LIMIT)
            a = g * jax.nn.sigmoid(SWIGLU_ALPHA * g) * (lin + 1.0)
            a_ref[pl.ds(r0, ts), :] = a.astype(a_ref.dtype)
            return carry

        lax.fori_loop(0, tns_ref[t], sub, 0)


def _moe_down_body(texp_ref, tns_ref, nused_ref, a_ref, wd_ref, bd_ref, y_ref, wbf_ref, *, ts):
    t = pl.program_id(1)

    @pl.when(t < nused_ref[0])
    def _():
        @pl.when(_expert_changed(texp_ref, t))
        def _():
            wbf_ref[...] = wd_ref[0].astype(BF16)

        def sub(j, carry):
            r0 = pl.multiple_of(j * ts, ts)
            y_ref[pl.ds(r0, ts), :] = jnp.dot(a_ref[pl.ds(r0, ts), :], wbf_ref[...],
                                              preferred_element_type=F32) + bd_ref[0]
            return carry

        lax.fori_loop(0, tns_ref[t], sub, 0)


def _moe_experts(xs, tile_exp, tile_nsub, n_used, w_gate_up, b_gate_up, w_down, b_down, *, tm, ts, tf, tn):
    n_rows, d = xs.shape
    n_exp, _, two_ff = w_gate_up.shape
    d_ff = two_ff // 2
    n_f = d_ff // tf
    n_t = n_rows // tm
    bgu = b_gate_up.reshape(n_exp, 1, two_ff)
    bdn = b_down.reshape(n_exp, 1, d)

    def tile(t, nu):
        return jnp.minimum(t, nu[0] - 1)

    up_spec = pltpu.PrefetchScalarGridSpec(
        num_scalar_prefetch=3,
        grid=(n_f, n_t),
        in_specs=[
            pl.BlockSpec((tm, d), lambda f, t, te, ns, nu: (tile(t, nu), 0)),
            pl.BlockSpec((1, d, tf), lambda f, t, te, ns, nu: (te[tile(t, nu)], 0, f)),
            pl.BlockSpec((1, d, tf), lambda f, t, te, ns, nu: (te[tile(t, nu)], 0, n_f + f)),
            pl.BlockSpec((1, 1, tf), lambda f, t, te, ns, nu: (te[tile(t, nu)], 0, f)),
            pl.BlockSpec((1, 1, tf), lambda f, t, te, ns, nu: (te[tile(t, nu)], 0, n_f + f)),
        ],
        out_specs=pl.BlockSpec((tm, tf), lambda f, t, te, ns, nu: (tile(t, nu), f)),
        scratch_shapes=[pltpu.VMEM((2, d, tf), BF16)],
    )
    act = pl.pallas_call(
        functools.partial(_moe_up_body, ts=ts),
        grid_spec=up_spec,
        out_shape=jax.ShapeDtypeStruct((n_rows, d_ff), BF16),
        compiler_params=_params(("arbitrary", "arbitrary")),
        name="moe_up",
    )(tile_exp, tile_nsub, n_used, xs, w_gate_up, w_gate_up, bgu, bgu)

    n_n = d // tn
    down_spec = pltpu.PrefetchScalarGridSpec(
        num_scalar_prefetch=3,
        grid=(n_n, n_t),
        in_specs=[
            pl.BlockSpec((tm, d_ff), lambda n, t, te, ns, nu: (tile(t, nu), 0)),
            pl.BlockSpec((1, d_ff, tn), lambda n, t, te, ns, nu: (te[tile(t, nu)], 0, n)),
            pl.BlockSpec((1, 1, tn), lambda n, t, te, ns, nu: (te[tile(t, nu)], 0, n)),
        ],
        out_specs=pl.BlockSpec((tm, tn), lambda n, t, te, ns, nu: (tile(t, nu), n)),
        scratch_shapes=[pltpu.VMEM((d_ff, tn), BF16)],
    )
    return pl.pallas_call(
        functools.partial(_moe_down_body, ts=ts),
        grid_spec=down_spec,
        out_shape=jax.ShapeDtypeStruct((n_rows, d), F32),
        compiler_params=_params(("arbitrary", "arbitrary")),
        name="moe_down",
    )(tile_exp, tile_nsub, n_used, act, w_down, bdn)


def _combine_body(dest_ref, x1_ref, gate_ref, nw_ref, yb_hbm, out_ref, rows_ref, sem, *, tt, top_k):
    i = pl.program_id(0)

    def row_copy(r, k):
        d = dest_ref[(i * tt + r) * top_k + k]
        return pltpu.make_async_copy(yb_hbm.at[pl.ds(d, 1)], rows_ref.at[k, pl.ds(r, 1)], sem)

    rows_per_iter = GATHER_UNROLL // top_k

    def issue(g, carry):
        for u in range(rows_per_iter):
            for k in range(top_k):
                row_copy(g * rows_per_iter + u, k).start()
        return carry

    def drain(g, carry):
        for u in range(rows_per_iter):
            for k in range(top_k):
                row_copy(g * rows_per_iter + u, k).wait()
        return carry

    lax.fori_loop(0, tt // rows_per_iter, issue, 0)
    lax.fori_loop(0, tt // rows_per_iter, drain, 0)
    gate = gate_ref[...]
    moe = rows_ref[0] * gate[:, 0:1]
    for k in range(1, top_k):
        moe = moe + rows_ref[k] * gate[:, k:k + 1]
    out_ref[...] = _rms(x1_ref[...] + moe, nw_ref[...])


def _combine(dest, x1, gate, norm_w, yb, *, tt):
    n, d = x1.shape
    grid_spec = pltpu.PrefetchScalarGridSpec(
        num_scalar_prefetch=1,
        grid=(n // tt,),
        in_specs=[
            pl.BlockSpec((tt, d), lambda i, ds: (i, 0)),
            pl.BlockSpec((tt, LANES), lambda i, ds: (i, 0)),
            pl.BlockSpec((1, d), lambda i, ds: (0, 0)),
            pl.BlockSpec(memory_space=pl.ANY),
        ],
        out_specs=pl.BlockSpec((tt, d), lambda i, ds: (i, 0)),
        scratch_shapes=[pltpu.VMEM((TOP_K, tt, d), F32), pltpu.SemaphoreType.DMA],
    )
    return pl.pallas_call(
        functools.partial(_combine_body, tt=tt, top_k=TOP_K),
        grid_spec=grid_spec,
        out_shape=jax.ShapeDtypeStruct((n, d), F32),
        compiler_params=_params(("arbitrary",)),
        name="moe_combine",
    )(dest, x1, gate, norm_w, yb)


def _tile(n, pref):
    t = min(n, pref)
    while n % t:
        t //= 2
    return t


def kernel(x_prompt, x_sample, cache_k, cache_v, state_conv, page_table, norm_mix_w, w_in,
           lambda_q1, lambda_k1, lambda_q2, lambda_k2, subln_w, conv_dw_w, conv_dw_b,
           conv_norm_w, conv_norm_b, w_o, norm_ffn_w, w_router, b_router, w_gate_up, b_gate_up,
           w_down, b_down, norm_final_w):
    batch, seq, d = x_prompt.shape
    dec, n_new, _ = x_sample.shape
    depth, n_pool, page, n_maps, d_qk = cache_k.shape
    n_heads, d_v = cache_v.shape[3:]
    aw = n_heads * d_v
    ch = d - aw
    n_pages = page_table.shape[1]
    past = n_pages * page
    rope_dim = d_qk // 4
    half = rope_dim // 2
    scale = d_qk ** -0.5
    n_exp = w_router.shape[2]
    n_p, n_s = batch * seq, dec * n_new

    h_p = x_prompt.reshape(n_p, d)
    h_s = x_sample.reshape(n_s, d)
    tab_p = _rope_tables(jnp.arange(seq, dtype=I32), d_qk, rope_dim)
    tab_s = _rope_tables(past + jnp.arange(n_s, dtype=I32) % n_new, d_qk, rope_dim)
    outs = {k: [] for k in ("kp", "vp", "cp", "ks", "vs", "cs")}

    for l in range(depth):
        lam_init = 0.8 - 0.6 * math.exp(-0.3 * l)
        lams = tuple(v[l].reshape(1, d_qk) for v in (lambda_q1, lambda_k1, lambda_q2, lambda_k2))
        sw = subln_w[l].reshape(1, d_v)
        w_in_bf = w_in[l].astype(BF16)
        w_o_bf = w_o[l].astype(BF16)
        nmw = norm_mix_w[l].reshape(1, d)
        dw_b = conv_dw_b[l].reshape(1, ch)
        ln_w = conv_norm_w[l].reshape(1, ch)
        ln_b = conv_norm_b[l].reshape(1, ch)

        tt_p = _tile(seq, 512)
        q_p, k_p, v_p, u_p = _proj(h_p, nmw, w_in_bf, tab_p, scale=scale, half=half, tt=tt_p, seq_t=seq)
        tt_s = _tile(n_s, 512)
        q_s, k_s, v_s, u_s = _proj(h_s, nmw, w_in_bf, tab_s, scale=scale, half=half, tt=tt_s)

        o_p = _attn_prompt(q_p, k_p, v_p, lams, sw, batch=batch, seq=seq, n_heads=n_heads,
                           d_qk=d_qk, lam_init=lam_init, tq=_tile(seq, 512))
        q_rows = q_s.reshape(dec, n_new, n_maps, d_qk).transpose(0, 2, 1, 3)
        q_rows = jnp.pad(q_rows, ((0, 0), (0, 0), (0, QR - n_new), (0, 0))).reshape(dec, n_maps * QR, d_qk)
        ck = cache_k.transpose(0, 1, 3, 4, 2).reshape(depth * n_pool, n_maps * d_qk, page)
        cv = cache_v.reshape(depth * n_pool, page * n_heads, d_v)
        o_s = _attn_sample(page_table + l * n_pool, q_rows, k_s.reshape(dec, n_new * n_maps, d_qk),
                           v_s.reshape(dec, n_new * n_heads, d_v), ck, cv, lams, sw, n_new=n_new,
                           n_maps=n_maps, n_heads=n_heads, qr=QR, lam_init=lam_init,
                           n_chunk=_tile(n_pages, 16))
        o_s = o_s.reshape(n_s, aw).astype(BF16)

        c_p = _conv_prompt(u_p, conv_dw_w[l], dw_b, ln_w, ln_b, batch=batch, seq=seq,
                           tt=_tile(seq, 256))
        u_s3 = u_s.reshape(dec, n_new, ch)
        u_pad_s = jnp.concatenate([state_conv[l], u_s3], axis=1)
        c_s = _conv_sample(u_pad_s.transpose(1, 0, 2), conv_dw_w[l], dw_b, ln_w, ln_b,
                           n_new=n_new, tb=_tile(dec, 16))
        c_s = c_s.transpose(1, 0, 2).reshape(n_s, ch)

        nfw = norm_ffn_w[l].reshape(1, d)
        br = b_router[l].reshape(1, n_exp)
        assert n_p % tt_s == 0
        x1_p, xf_all, idx_p, gate_p = _wo_router(o_p, c_p, h_p, w_o_bf, nfw, w_router[l], br, tt=tt_p,
                                                 xf_rows=n_p + n_s)
        x1_s, xf_all, idx_s, gate_s = _wo_router(o_s, c_s, h_s, w_o_bf, nfw, w_router[l], br, tt=tt_s,
                                                 xf_rows=n_p + n_s, xf_row0=n_p, xf_buf=xf_all)

        idx_all = jnp.concatenate([idx_p[:, :TOP_K], idx_s[:, :TOP_K]], axis=0)
        dest, tile_exp, tile_nsub, tile_rows, n_used = _route_plan(idx_all, n_exp, MOE_TILE, MOE_SUBTILE)
        xs = _gather_rows(xf_all, dest, tile_nsub, tile_rows, n_used, tm=MOE_TILE, ts=MOE_SUBTILE,
                          top_k=TOP_K)
        yb = _moe_experts(xs, tile_exp, tile_nsub, n_used, w_gate_up[l], b_gate_up[l], w_down[l],
                          b_down[l], tm=MOE_TILE, ts=MOE_SUBTILE, tf=_tile(w_down.shape[2], 1024),
                          tn=_tile(d, 1024))

        last = l == depth - 1
        nw_out = norm_final_w.reshape(1, d) if last else jnp.ones((1, d), F32)
        assert last, "the combine kernel fuses the final norm; deeper stacks need an un-normed variant"
        h_p = _combine(dest[:n_p * TOP_K], x1_p, gate_p, nw_out, yb, tt=_tile(n_p, 512))
        h_s = _combine(dest[n_p * TOP_K:], x1_s, gate_s, nw_out, yb, tt=_tile(n_s, 512))

        outs["kp"].append(k_p.reshape(batch, n_maps, d_qk, seq).transpose(0, 3, 1, 2))
        outs["vp"].append(v_p.reshape(batch, seq, n_heads, d_v))
        outs["cp"].append(u_p.reshape(batch, seq, ch)[:, seq - (conv_dw_w.shape[1] - 1):])
        outs["ks"].append(k_s.reshape(dec, n_new, n_maps, d_qk))
        outs["vs"].append(v_s.reshape(dec, n_new, n_heads, d_v))
        outs["cs"].append(u_pad_s[:, n_new:])

    return (h_p.reshape(batch, seq, d), h_s.reshape(dec, n_new, d),
            jnp.stack(outs["kp"]), jnp.stack(outs["vp"]), jnp.stack(outs["cp"]),
            jnp.stack(outs["ks"]), jnp.stack(outs["vs"]), jnp.stack(outs["cs"]))
```
